```python
import math
import jax, jax.numpy as jnp
from jax import lax
import numpy as np

D_MODEL = 2048
BATCH = 4
SEQ = 2048
DEPTH = 2

CHUNK = 64
EPS = 1e-6
ATT_HEADS = 8
ATT_HEAD_DIM = 128
ATT_WIDTH = ATT_HEADS * ATT_HEAD_DIM
ATT_LEFT_CHUNKS = 8
ATT_BAND = ATT_LEFT_CHUNKS + 1
REL_CLIP = 256
SSM_GROUP = 16
SSM_GROUPS = 48
SSM_WIDTH = SSM_GROUP * SSM_GROUPS
SSM_STATE = 64
SSM_DT_MIN = 1e-3
SSM_DT_MAX = 1e-1
MLSTM_HEADS = 4
MLSTM_HEAD_DIM = 256
MLSTM_WIDTH = MLSTM_HEADS * MLSTM_HEAD_DIM
MLSTM_CONV = 4
MLSTM_NEG = -1e30
N_BRANCH = 3
MEM_TOKENS = 256
MEM_HEADS = 4
MEM_HEAD_DIM = D_MODEL // MEM_HEADS
FFN_HIDDEN = -(-8 * D_MODEL // (3 * 256)) * 256
IN_WIDTH = 3 * ATT_WIDTH + SSM_WIDTH + 3 * MLSTM_WIDTH + 2 * MLSTM_HEADS + N_BRANCH * D_MODEL

kernel_name = "hybrid_gated_chunkattn_s5_mlstm_block"


def _in_split_points():
    widths = (ATT_WIDTH, ATT_WIDTH, ATT_WIDTH, SSM_WIDTH, MLSTM_WIDTH, MLSTM_WIDTH,
              MLSTM_WIDTH, MLSTM_HEADS, MLSTM_HEADS)
    pts, acc = [], 0
    for w in widths:
        acc += w
        pts.append(acc)
    return pts


def _rms(x, g):
    xf = x.astype(jnp.float32)
    y = xf * lax.rsqrt(jnp.mean(xf * xf, axis=-1, keepdims=True) + EPS)
    return (y * g.astype(jnp.float32)).astype(x.dtype)


def _chunk_attention(q, k, v, g_q, g_k, rel_bias):
    b, l, _ = q.shape
    nc = l // CHUNK
    shp = (b, nc, CHUNK, ATT_HEADS, ATT_HEAD_DIM)
    q = _rms(q.reshape(shp), g_q)
    k = _rms(k.reshape(shp), g_k)
    v = v.reshape(shp)
    pad = ((0, 0), (ATT_LEFT_CHUNKS, 0), (0, 0), (0, 0), (0, 0))
    kp, vp = jnp.pad(k, pad), jnp.pad(v, pad)
    kb = jnp.concatenate([kp[:, w:w + nc] for w in range(ATT_BAND)], axis=2)
    vb = jnp.concatenate([vp[:, w:w + nc] for w in range(ATT_BAND)], axis=2)
    s = jnp.einsum('bcqhd,bckhd->bchqk', q, kb).astype(jnp.float32) * (ATT_HEAD_DIM ** -0.5)
    kpos = np.arange(ATT_BAND * CHUNK) - ATT_LEFT_CHUNKS * CHUNK
    rel = np.clip(kpos[None, :] - np.arange(CHUNK)[:, None], -REL_CLIP, REL_CLIP) + REL_CLIP
    bias = rel_bias.astype(jnp.float32)[:, rel]
    kchunk = np.arange(nc)[:, None] + np.arange(ATT_BAND * CHUNK)[None, :] // CHUNK - ATT_LEFT_CHUNKS
    valid = jnp.asarray(kchunk >= 0)
    s = jnp.where(valid[None, :, None, None, :], s + bias[None, None], -jnp.inf)
    p = jax.nn.softmax(s, axis=-1)
    o = jnp.einsum('bchqk,bckhd->bcqhd', p.astype(v.dtype), vb)
    return o.reshape(b, l, ATT_WIDTH)


def _s5(u, lam_re, lam_im, log_dt, b_re, b_im, c_re, c_im, d_skip, w_glu, b_glu):
    bsz, l, _ = u.shape
    f32 = jnp.float32
    uf = u.astype(f32).reshape(bsz, l, SSM_GROUPS, SSM_GROUP)
    lr, li = lam_re.astype(f32), lam_im.astype(f32)
    dt = jnp.exp(log_dt.astype(f32))[:, None]
    mag = jnp.exp(lr * dt)
    ar, ai = mag * jnp.cos(li * dt), mag * jnp.sin(li * dt)
    den = lr * lr + li * li
    nr, ni = ar - 1.0, ai
    zr, zi = (nr * lr + ni * li) / den, (ni * lr - nr * li) / den
    br, bi = b_re.astype(f32), b_im.astype(f32)
    bbr = zr[..., None] * br - zi[..., None] * bi
    bbi = zr[..., None] * bi + zi[..., None] * br
    bu_r = jnp.einsum('blgn,gpn->blgp', uf, bbr)
    bu_i = jnp.einsum('blgn,gpn->blgp', uf, bbi)
    a_r = jnp.broadcast_to(ar, bu_r.shape)
    a_i = jnp.broadcast_to(ai, bu_i.shape)

    def combine(e1, e2):
        a1r, a1i, b1r, b1i = e1
        a2r, a2i, b2r, b2i = e2
        return (a2r * a1r - a2i * a1i, a2r * a1i + a2i * a1r,
                a2r * b1r - a2i * b1i + b2r, a2r * b1i + a2i * b1r + b2i)

    _, _, xr, xi = lax.associative_scan(combine, (a_r, a_i, bu_r, bu_i), axis=1)
    y = (jnp.einsum('blgp,gnp->blgn', xr, c_re.astype(f32))
         - jnp.einsum('blgp,gnp->blgn', xi, c_im.astype(f32)))
    y = y.reshape(bsz, l, SSM_WIDTH) + d_skip.astype(f32) * u.astype(f32)
    y = jax.nn.gelu(y)
    y = y * jax.nn.sigmoid(y @ w_glu.astype(f32) + b_glu.astype(f32))
    return y.astype(u.dtype)


def _causal_conv(x, w, b):
    k = w.shape[0]
    y = lax.conv_general_dilated(x, w[:, None, :], window_strides=(1,), padding=[(k - 1, 0)],
                                 dimension_numbers=('NWC', 'WIO', 'NWC'),
                                 feature_group_count=x.shape[-1])
    return y + b


def _mlstm(x_m, v_m, o_m, i_pre, f_pre, conv_w, conv_b, wq, wk, b_i, b_f, g_h, skip):
    bsz, l, _ = x_m.shape
    nc = l // CHUNK
    f32 = jnp.float32
    H, Dh = MLSTM_HEADS, MLSTM_HEAD_DIM
    xc = jax.nn.silu(_causal_conv(x_m, conv_w, conv_b))
    xh = xc.reshape(bsz, l, H, Dh)
    q = jnp.einsum('blhd,hde->blhe', xh, wq)
    k = jnp.einsum('blhd,hde->blhe', xh, wk) * (Dh ** -0.5)
    v = v_m.reshape(bsz, l, H, Dh)
    ig = (i_pre + b_i).astype(f32)
    lf = jax.nn.log_sigmoid((f_pre + b_f).astype(f32))

    def chunks(t):
        return t.astype(f32).reshape(bsz, nc, CHUNK, H, -1).transpose(1, 0, 3, 2, 4)

    qc, kc, vc = chunks(q), chunks(k), chunks(v)
    ic = ig.reshape(bsz, nc, CHUNK, H).transpose(1, 0, 3, 2)
    fc = lf.reshape(bsz, nc, CHUNK, H).transpose(1, 0, 3, 2)
    tri = jnp.tril(jnp.ones((CHUNK, CHUNK), dtype=bool))

    def step(carry, inp):
        cmat, nvec, m = carry
        qq, kk, vv, ii, ff = inp
        bcum = jnp.cumsum(ff, axis=-1)
        dmat = jnp.where(tri, bcum[..., :, None] - bcum[..., None, :] + ii[..., None, :], -jnp.inf)
        inter = bcum + m[..., None]
        m_row = jnp.maximum(jnp.max(dmat, axis=-1), inter)
        w_intra = jnp.exp(dmat - m_row[..., None])
        w_inter = jnp.exp(inter - m_row)
        s = jnp.einsum('bhtd,bhsd->bhts', qq, kk) * w_intra
        num = (jnp.einsum('bhts,bhsd->bhtd', s, vv)
               + w_inter[..., None] * jnp.einsum('bhvk,bhtk->bhtv', cmat, qq))
        den = jnp.sum(s, axis=-1) + w_inter * jnp.einsum('bhtk,bhk->bht', qq, nvec)
        h = num / jnp.maximum(jnp.abs(den), jnp.exp(-m_row))[..., None]
        b_last = bcum[..., -1]
        g = b_last[..., None] - bcum + ii
        m_new = jnp.maximum(b_last + m, jnp.max(g, axis=-1))
        wg = jnp.exp(g - m_new[..., None])
        decay = jnp.exp(b_last + m - m_new)
        cmat = decay[..., None, None] * cmat + jnp.einsum('bhtv,bhtk->bhvk', vv * wg[..., None], kk)
        nvec = decay[..., None] * nvec + jnp.einsum('bht,bhtk->bhk', wg, kk)
        return (cmat, nvec, m_new), h

    init = (jnp.zeros((bsz, H, Dh, Dh), f32), jnp.zeros((bsz, H, Dh), f32),
            jnp.full((bsz, H), MLSTM_NEG, f32))
    _, hs = lax.scan(step, init, (qc, kc, vc, ic, fc))
    hs = hs.transpose(1, 0, 3, 2, 4).reshape(bsz, l, H, Dh)
    hn = _rms(hs, g_h.reshape(H, Dh)).reshape(bsz, l, MLSTM_WIDTH)
    hn = hn + skip.astype(f32) * xc.astype(f32)
    return (jax.nn.sigmoid(o_m.astype(f32)) * hn).astype(x_m.dtype)


def _mem_attention(h, mem_n, w_q, w_kv, g_q, g_k, w_o):
    b, l, _ = h.shape
    m = mem_n.shape[1]
    q = _rms((h @ w_q).reshape(b, l, MEM_HEADS, MEM_HEAD_DIM), g_q)
    kv = (mem_n @ w_kv).reshape(b, m, 2, MEM_HEADS, MEM_HEAD_DIM)
    k = _rms(kv[:, :, 0], g_k)
    v = kv[:, :, 1]
    s = jnp.einsum('blhd,bmhd->bhlm', q, k).astype(jnp.float32) * (MEM_HEAD_DIM ** -0.5)
    p = jax.nn.softmax(s, axis=-1)
    o = jnp.einsum('bhlm,bmhd->blhd', p.astype(v.dtype), v).reshape(b, l, D_MODEL)
    return o @ w_o


def _swiglu(h, w_gu, w_down):
    g, u = jnp.split(h @ w_gu, 2, axis=-1)
    return (jax.nn.silu(g) * u) @ w_down


def setup_inputs(seed: int = 0) -> dict:
    key = jax.random.key(seed)
    ks = iter(jax.random.split(key, 64))
    f32 = jnp.float32

    def nrm(shape, scale):
        return jax.random.normal(next(ks), shape, f32) * scale

    def gain(shape):
        return 1.0 + nrm(shape, 0.02)

    L, D = DEPTH, D_MODEL
    G, P, N = SSM_GROUPS, SSM_STATE, SSM_GROUP
    H, Dh = MLSTM_HEADS, MLSTM_HEAD_DIM
    log_dt = jax.random.uniform(next(ks), (L, G), f32, math.log(SSM_DT_MIN), math.log(SSM_DT_MAX))
    lam_im = jnp.pi * jnp.arange(P, dtype=f32)[None, None, :] + nrm((L, G, P), 0.01)
    b_f = jnp.linspace(3.0, 6.0, H, dtype=f32)[None, :] + nrm((L, H), 0.01)
    return {
        "x": nrm((BATCH, SEQ, D), 1.0),
        "mem": nrm((BATCH, MEM_TOKENS, D), 1.0),
        "g_mem": gain((D,)),
        "norm_mix": gain((L, D)),
        "w_in": nrm((L, D, IN_WIDTH), D ** -0.5),
        "b_gate": nrm((L, N_BRANCH * D), 0.01),
        "g_qa": gain((L, ATT_HEAD_DIM)),
        "g_ka": gain((L, ATT_HEAD_DIM)),
        "rel_bias": nrm((L, ATT_HEADS, 2 * REL_CLIP + 1), 0.1),
        "lam_re": -0.5 + nrm((L, G, P), 0.01),
        "lam_im": lam_im,
        "log_dt": log_dt,
        "b_re": nrm((L, G, P, N), (2 * N) ** -0.5),
        "b_im": nrm((L, G, P, N), (2 * N) ** -0.5),
        "c_re": nrm((L, G, N, P), (2 * P) ** -0.5),
        "c_im": nrm((L, G, N, P), (2 * P) ** -0.5),
        "d_skip": nrm((L, SSM_WIDTH), 0.5),
        "w_glu": nrm((L, SSM_WIDTH, SSM_WIDTH), SSM_WIDTH ** -0.5),
        "b_glu": nrm((L, SSM_WIDTH), 0.01),
        "conv_w": nrm((L, MLSTM_CONV, MLSTM_WIDTH), MLSTM_CONV ** -0.5),
        "conv_b": nrm((L, MLSTM_WIDTH), 0.01),
        "wq_m": nrm((L, H, Dh, Dh), Dh ** -0.5),
        "wk_m": nrm((L, H, Dh, Dh), Dh ** -0.5),
        "b_i": nrm((L, H), 0.1),
        "b_f": b_f,
        "g_hm": gain((L, MLSTM_WIDTH)),
        "skip_m": gain((L, MLSTM_WIDTH)),
        "w_br_a": nrm((L, ATT_WIDTH, D), ATT_WIDTH ** -0.5),
        "w_br_s": nrm((L, SSM_WIDTH, D), SSM_WIDTH ** -0.5),
        "w_br_m": nrm((L, MLSTM_WIDTH, D), MLSTM_WIDTH ** -0.5),
        "w_out": nrm((L, D, D), D ** -0.5),
        "norm_x": gain((L, D)),
        "w_xq": nrm((L, D, D), D ** -0.5),
        "w_xkv": nrm((L, D, 2 * D), D ** -0.5),
        "g_xq": gain((L, MEM_HEAD_DIM)),
        "g_xk": gain((L, MEM_HEAD_DIM)),
        "w_xo": nrm((L, D, D), D ** -0.5),
        "norm_ffn": gain((L, D)),
        "w_gu": nrm((L, D, 2 * FFN_HIDDEN), D ** -0.5),
        "w_down": nrm((L, FFN_HIDDEN, D), FFN_HIDDEN ** -0.5),
    }


def reference(x, mem, g_mem, norm_mix, w_in, b_gate, g_qa, g_ka, rel_bias, lam_re, lam_im,
              log_dt, b_re, b_im, c_re, c_im, d_skip, w_glu, b_glu, conv_w, conv_b, wq_m,
              wk_m, b_i, b_f, g_hm, skip_m, w_br_a, w_br_s, w_br_m, w_out, norm_x, w_xq,
              w_xkv, g_xq, g_xk, w_xo, norm_ffn, w_gu, w_down):
    b, l, _ = x.shape
    mem_n = _rms(mem, g_mem)
    splits = _in_split_points()
    for i in range(DEPTH):
        h = _rms(x, norm_mix[i])
        qa, ka, va, us, xm, vm, om, im, fm, gpre = jnp.split(h @ w_in[i], splits, axis=-1)
        ya = _chunk_attention(qa, ka, va, g_qa[i], g_ka[i], rel_bias[i])
        ys = _s5(us, lam_re[i], lam_im[i], log_dt[i], b_re[i], b_im[i], c_re[i], c_im[i],
                 d_skip[i], w_glu[i], b_glu[i])
        ym = _mlstm(xm, vm, om, im, fm, conv_w[i], conv_b[i], wq_m[i], wk_m[i], b_i[i], b_f[i],
                    g_hm[i], skip_m[i])
        gates = jax.nn.sigmoid(gpre + b_gate[i]).reshape(b, l, N_BRANCH, D_MODEL)
        merged = (gates[:, :, 0] * (ya @ w_br_a[i])
                  + gates[:, :, 1] * (ys @ w_br_s[i])
                  + gates[:, :, 2] * (ym @ w_br_m[i]))
        x = x + merged @ w_out[i]
        x = x + _mem_attention(_rms(x, norm_x[i]), mem_n, w_xq[i], w_xkv[i], g_xq[i], g_xk[i], w_xo[i])
        x = x + _swiglu(_rms(x, norm_ffn[i]), w_gu[i], w_down[i])
    return x
```

```python
import functools
import math

import jax
import jax.numpy as jnp
from jax import lax
from jax.experimental import pallas as pl
from jax.experimental.pallas import tpu as pltpu

F32 = jnp.float32
BF16 = jnp.bfloat16

EPS = 1e-6
LANES = 128
ATT_CHUNK = 64
ATT_HEADS = 8
ATT_HEAD_DIM = 128
ATT_WIDTH = ATT_HEADS * ATT_HEAD_DIM
ATT_LEFT = 8
REL_CLIP = 256
ATT_QB = 256
ATT_PAD = ATT_LEFT * ATT_CHUNK
ATT_KW = ATT_QB + ATT_PAD
ATT_EXT = 1024
MASK_NEG = -1e30
SSM_GROUP = 16
SSM_GROUPS = 48
SSM_WIDTH = SSM_GROUP * SSM_GROUPS
SSM_STATE = 64
SSM_T = LANES
ML_HEADS = 4
ML_HEAD_DIM = 256
ML_WIDTH = ML_HEADS * ML_HEAD_DIM
ML_CONV = 4
ML_T = 256
ML_NEG = -1e30
MEM_HEADS = 4

VMEM_LIMIT = 56 * 1024 * 1024

_NT = (((1,), (1,)), ((), ()))
_TN = (((0,), (0,)), ((), ()))


def _cparams(sem):
    return pltpu.CompilerParams(dimension_semantics=sem, vmem_limit_bytes=VMEM_LIMIT)


def _rms(x, g):
    return x * lax.rsqrt(jnp.mean(x * x, axis=-1, keepdims=True) + EPS) * g


def _const_spec(shape):
    nd = len(shape)
    return pl.BlockSpec(shape, lambda *_: (0,) * nd, pipeline_mode=pl.Buffered(1))


def _norm_proj_body(x_ref, g_ref, w_ref, o_ref, hn_ref):
    @pl.when(pl.program_id(1) == 0)
    def _():
        hn_ref[...] = _rms(x_ref[...], g_ref[...]).astype(BF16)

    o_ref[...] = jnp.dot(hn_ref[...], w_ref[...], preferred_element_type=F32).astype(o_ref.dtype)


def _norm_proj(x2, g, w, tm, tn, out_dtype=BF16):
    m, d = x2.shape
    n = w.shape[1]
    tm = min(tm, m)
    return pl.pallas_call(
        _norm_proj_body,
        grid=(m // tm, n // tn),
        in_specs=[pl.BlockSpec((tm, d), lambda i, j: (i, 0)),
                  pl.BlockSpec((1, d), lambda i, j: (0, 0)),
                  pl.BlockSpec((d, tn), lambda i, j: (0, j))],
        out_specs=pl.BlockSpec((tm, tn), lambda i, j: (i, j)),
        out_shape=jax.ShapeDtypeStruct((m, n), out_dtype),
        scratch_shapes=[pltpu.VMEM((tm, d), BF16)],
        compiler_params=_cparams(("parallel", "arbitrary")),
        name="norm_proj",
    )(x2, g.reshape(1, d), w)


def _in_proj_body(x_ref, g_ref, w_ref, wus_ref, wif_ref, p_ref, ust_ref, if_ref, hn_ref):
    @pl.when(pl.program_id(1) == 0)
    def _():
        hn = _rms(x_ref[...], g_ref[...]).astype(BF16)
        hn_ref[...] = hn
        ust_ref[...] = lax.dot_general(wus_ref[...], hn, _NT, preferred_element_type=F32)
        if_ref[...] = jnp.dot(hn, wif_ref[...], preferred_element_type=F32)

    p_ref[...] = jnp.dot(hn_ref[...], w_ref[...], preferred_element_type=F32).astype(p_ref.dtype)


def _in_proj(x2, g, w_main, w_us_t, w_if, tm, tn):
    m, d = x2.shape
    n = w_main.shape[1]
    tm = min(tm, m)
    return pl.pallas_call(
        _in_proj_body,
        grid=(m // tm, n // tn),
        in_specs=[pl.BlockSpec((tm, d), lambda i, j: (i, 0)),
                  pl.BlockSpec((1, d), lambda i, j: (0, 0)),
                  pl.BlockSpec((d, tn), lambda i, j: (0, j)),
                  _const_spec((SSM_WIDTH, d)),
                  _const_spec((d, LANES))],
        out_specs=[pl.BlockSpec((tm, tn), lambda i, j: (i, j)),
                   pl.BlockSpec((SSM_WIDTH, tm), lambda i, j: (0, i)),
                   pl.BlockSpec((tm, LANES), lambda i, j: (i, 0))],
        out_shape=[jax.ShapeDtypeStruct((m, n), BF16),
                   jax.ShapeDtypeStruct((SSM_WIDTH, m), F32),
                   jax.ShapeDtypeStruct((m, LANES), F32)],
        scratch_shapes=[pltpu.VMEM((tm, d), BF16)],
        compiler_params=_cparams(("parallel", "arbitrary")),
        name="in_proj",
    )(x2, g.reshape(1, d), w_main, w_us_t, w_if)


def _att_bias_body(ext_ref, o_ref):
    ext = ext_ref[0]
    base = pltpu.roll(ext, ATT_EXT - (ATT_QB - 1), axis=1)
    slab = jnp.broadcast_to(base, (ATT_QB, ATT_EXT))
    tab = pltpu.roll(slab, 0, 1, stride=1, stride_axis=0)[:, :ATT_KW]
    qc = lax.broadcasted_iota(jnp.int32, (ATT_QB, ATT_KW), 0) // ATT_CHUNK
    kc = lax.broadcasted_iota(jnp.int32, (ATT_QB, ATT_KW), 1) // ATT_CHUNK
    o_ref[0] = jnp.where((kc >= qc) & (kc <= qc + ATT_LEFT), tab, MASK_NEG)


def _att_bias(rel_bias):
    h = rel_bias.shape[0]
    ext = jnp.concatenate(
        [jnp.broadcast_to(rel_bias[:, :1], (h, ATT_EXT - 2 * REL_CLIP - 1)), rel_bias], axis=1)
    return pl.pallas_call(
        _att_bias_body,
        grid=(h,),
        in_specs=[pl.BlockSpec((1, 1, ATT_EXT), lambda i: (i, 0, 0))],
        out_specs=pl.BlockSpec((1, ATT_QB, ATT_KW), lambda i: (i, 0, 0)),
        out_shape=jax.ShapeDtypeStruct((h, ATT_QB, ATT_KW), F32),
        compiler_params=_cparams(("parallel",)),
        name="att_bias",
    )(ext.reshape(h, 1, ATT_EXT))


def _attn_body(q_ref, k_ref, v_ref, gq_ref, gk_ref, bias_ref, o_ref, kpad, vpad, *, seq):
    kpad[0:ATT_PAD, :] = jnp.zeros((ATT_PAD, ATT_HEAD_DIM), BF16)
    vpad[0:ATT_PAD, :] = jnp.zeros((ATT_PAD, ATT_HEAD_DIM), BF16)
    kpad[ATT_PAD:, :] = _rms(k_ref[0].astype(F32), gk_ref[...]).astype(BF16)
    vpad[ATT_PAD:, :] = v_ref[0]
    bias = bias_ref[0]
    scale = ATT_HEAD_DIM ** -0.5
    for qb in range(seq // ATT_QB):
        r0 = qb * ATT_QB
        qn = _rms(q_ref[0, r0:r0 + ATT_QB, :].astype(F32), gq_ref[...]).astype(BF16)
        kw = kpad[r0:r0 + ATT_KW, :]
        s = lax.dot_general(qn, kw, _NT, preferred_element_type=F32) * scale + bias
        if r0 < ATT_PAD:
            col = lax.broadcasted_iota(jnp.int32, (ATT_QB, ATT_KW), 1)
            s = jnp.where(col + r0 >= ATT_PAD, s, MASK_NEG)
        m = jnp.max(s, axis=-1, keepdims=True)
        p = jnp.exp(s - m)
        l = jnp.sum(p, axis=-1, keepdims=True)
        o = jnp.dot(p.astype(BF16), vpad[r0:r0 + ATT_KW, :], preferred_element_type=F32)
        o_ref[0, r0:r0 + ATT_QB, :] = (o / l).astype(o_ref.dtype)


def _chunk_attention(proj3, gq, gk, bias):
    b, seq, _ = proj3.shape
    blk = (1, seq, ATT_HEAD_DIM)
    return pl.pallas_call(
        functools.partial(_attn_body, seq=seq),
        grid=(b, ATT_HEADS),
        in_specs=[pl.BlockSpec(blk, lambda i, h: (i, 0, h)),
                  pl.BlockSpec(blk, lambda i, h: (i, 0, ATT_HEADS + h)),
                  pl.BlockSpec(blk, lambda i, h: (i, 0, 2 * ATT_HEADS + h)),
                  pl.BlockSpec((1, ATT_HEAD_DIM), lambda i, h: (0, 0)),
                  pl.BlockSpec((1, ATT_HEAD_DIM), lambda i, h: (0, 0)),
                  pl.BlockSpec((1, ATT_QB, ATT_KW), lambda i, h: (h, 0, 0))],
        out_specs=pl.BlockSpec(blk, lambda i, h: (i, 0, h)),
        out_shape=jax.ShapeDtypeStruct((b, seq, ATT_WIDTH), BF16),
        scratch_shapes=[pltpu.VMEM((seq + ATT_PAD, ATT_HEAD_DIM), BF16),
                        pltpu.VMEM((seq + ATT_PAD, ATT_HEAD_DIM), BF16)],
        compiler_params=_cparams(("parallel", "parallel")),
        name="chunk_attention",
    )(proj3, proj3, proj3, gq.reshape(1, -1), gk.reshape(1, -1), bias)


def _s5_body(ut_ref, lr2_ref, li2_ref, lrc_ref, lic_ref, ldt_ref, brt_ref, bit_ref,
             cr_ref, ci_ref, crt_ref, cit_ref, yt_ref, mfull, fmat, emat, kflat, *, cpb):
    t_len, n_ch, n_st = SSM_T, SSM_GROUP, SSM_STATE
    nch = ut_ref.shape[1]
    dt = jnp.exp(ldt_ref[0])
    lr2, li2 = lr2_ref[0], li2_ref[0]
    lrc, lic = lrc_ref[0], lic_ref[0]
    lane = lax.broadcasted_iota(jnp.int32, (1, 2 * n_st), 1)
    first_half = lane < n_st

    mag = jnp.exp(lr2 * dt)
    ar, ai = mag * jnp.cos(li2 * dt), mag * jnp.sin(li2 * dt)
    den = lr2 * lr2 + li2 * li2
    nr, ni = ar - 1.0, ai
    zr, zi = (nr * lr2 + ni * li2) / den, (ni * lr2 - nr * li2) / den
    bbr = zr * brt_ref[0] - zi * bit_ref[0]
    bbi = zr * bit_ref[0] + zi * brt_ref[0]

    tau = lax.broadcasted_iota(jnp.int32, (2 * n_st, t_len), 1).astype(F32)
    magp = jnp.exp(lrc * dt * tau)
    p0r, p0i = magp * jnp.cos(lic * dt * tau), magp * jnp.sin(lic * dt * tau)
    magq = jnp.exp(lrc * dt * (tau + 1.0))
    p1r = magq * jnp.cos(lic * dt * (tau + 1.0))
    p1i = magq * jnp.sin(lic * dt * (tau + 1.0))

    cr, ci = cr_ref[0], ci_ref[0]
    coef_r, coef_i = [], []
    for mch in range(n_ch):
        br_m, bi_m = bbr[mch:mch + 1, :n_st], bbi[mch:mch + 1, :n_st]
        coef_r.append(cr * br_m - ci * bi_m)
        coef_i.append(cr * bi_m + ci * br_m)
    coef_r = jnp.concatenate(coef_r, axis=0)
    coef_i = jnp.concatenate(coef_i, axis=0)
    kflat[...] = (jnp.dot(coef_r, p0r[:n_st], preferred_element_type=F32,
                          precision=lax.Precision.HIGHEST)
                  - jnp.dot(coef_i, p0i[:n_st], preferred_element_type=F32,
                            precision=lax.Precision.HIGHEST))

    srow = lax.broadcasted_iota(jnp.int32, (t_len, t_len), 0)
    tcol = lax.broadcasted_iota(jnp.int32, (t_len, t_len), 1)
    causal = tcol >= srow

    def build(mch, carry):
        for nn in range(n_ch):
            row = kflat[pl.ds(mch * n_ch + nn, 1), :]
            tz = pltpu.roll(jnp.broadcast_to(row, (t_len, t_len)), 0, 1, stride=1, stride_axis=0)
            mfull[pl.ds(pl.multiple_of(mch * t_len, t_len), t_len), nn * t_len:(nn + 1) * t_len] = (
                jnp.where(causal, tz, 0.0).astype(BF16))
        return carry

    lax.fori_loop(0, n_ch, build, 0)

    e_rem = (t_len - 1.0) - lax.broadcasted_iota(jnp.int32, (t_len, 2 * n_st), 0).astype(F32)
    magf = jnp.exp(lr2 * dt * e_rem)
    qr, qi = magf * jnp.cos(li2 * dt * e_rem), magf * jnp.sin(li2 * dt * e_rem)
    for mch in range(n_ch):
        br_m, bi_m = bbr[mch:mch + 1, :], bbi[mch:mch + 1, :]
        fmat[mch * t_len:(mch + 1) * t_len, :] = jnp.where(
            first_half, qr * br_m - qi * bi_m, qr * bi_m + qi * br_m).astype(BF16)

    crt, cit = crt_ref[0], cit_ref[0]
    top = lax.broadcasted_iota(jnp.int32, (2 * n_st, 1), 0) < n_st
    for nn in range(n_ch):
        c_r, c_i = crt[:, nn:nn + 1], cit[:, nn:nn + 1]
        emat[:, nn * t_len:(nn + 1) * t_len] = jnp.where(
            top, c_r * p1r - c_i * p1i, -(c_r * p1i + c_i * p1r)).astype(BF16)

    xcat = jnp.concatenate([ut_ref[mch].astype(BF16) for mch in range(n_ch)], axis=1)
    s_loc = jnp.dot(xcat, fmat[...], preferred_element_type=F32)

    cidx = lax.broadcasted_iota(jnp.int32, (nch, 1), 0) % cpb
    xin = jnp.where(cidx >= 1, pltpu.roll(s_loc, 1, axis=0), 0.0)
    d = 1
    while d < cpb:
        e_d = float(t_len * d)
        magd = jnp.exp(lr2 * dt * e_d)
        a_r, a_i = magd * jnp.cos(li2 * dt * e_d), magd * jnp.sin(li2 * dt * e_d)
        a_is = jnp.where(first_half, -a_i, a_i)
        sh = pltpu.roll(xin, d, axis=0)
        contrib = sh * a_r + pltpu.roll(sh, n_st, axis=1) * a_is
        xin = xin + jnp.where(cidx >= d, contrib, 0.0)
        d *= 2

    y = (jnp.dot(xcat, mfull[...], preferred_element_type=F32)
         + jnp.dot(xin.astype(BF16), emat[...], preferred_element_type=F32))
    for nn in range(n_ch):
        yt_ref[nn] = y[:, nn * t_len:(nn + 1) * t_len]


def _s5_scan(ust, lam_re, lam_im, log_dt, b_re, b_im, c_re, c_im, *, cpb):
    g, p, n = SSM_GROUPS, SSM_STATE, SSM_GROUP
    tokens = ust.shape[1]
    nch = tokens // SSM_T
    ut3 = ust.reshape(SSM_WIDTH, nch, SSM_T)
    dup = lambda a, axis: jnp.concatenate([a, a], axis=axis)
    lr2 = dup(lam_re, 1).reshape(g, 1, 2 * p)
    li2 = dup(lam_im, 1).reshape(g, 1, 2 * p)
    lrc = dup(lam_re, 1).reshape(g, 2 * p, 1)
    lic = dup(lam_im, 1).reshape(g, 2 * p, 1)
    ldt = log_dt.reshape(g, 1, 1)
    brt = dup(jnp.swapaxes(b_re, 1, 2), 2)
    bit = dup(jnp.swapaxes(b_im, 1, 2), 2)
    crt = dup(jnp.swapaxes(c_re, 1, 2), 1)
    cit = dup(jnp.swapaxes(c_im, 1, 2), 1)
    spec = lambda *s: pl.BlockSpec((1,) + s, lambda i: (i, 0, 0))
    yt3 = pl.pallas_call(
        functools.partial(_s5_body, cpb=cpb),
        grid=(g,),
        in_specs=[pl.BlockSpec((n, nch, SSM_T), lambda i: (i, 0, 0)),
                  spec(1, 2 * p), spec(1, 2 * p), spec(2 * p, 1), spec(2 * p, 1), spec(1, 1),
                  spec(n, 2 * p), spec(n, 2 * p), spec(n, p), spec(n, p),
                  spec(2 * p, n), spec(2 * p, n)],
        out_specs=pl.BlockSpec((n, nch, SSM_T), lambda i: (i, 0, 0)),
        out_shape=jax.ShapeDtypeStruct((SSM_WIDTH, nch, SSM_T), F32),
        scratch_shapes=[pltpu.VMEM((n * SSM_T, n * SSM_T), BF16),
                        pltpu.VMEM((n * SSM_T, 2 * p), BF16),
                        pltpu.VMEM((2 * p, n * SSM_T), BF16),
                        pltpu.VMEM((n * n, SSM_T), F32)],
        compiler_params=_cparams(("parallel",)),
        name="s5_scan",
    )(ut3, lr2, li2, lrc, lic, ldt, brt, bit, c_re, c_im, crt, cit)
    return yt3.reshape(SSM_WIDTH, tokens)


def _s5_post_body(yt_ref, ut_ref, dsk_ref, wgt_ref, bg_ref, o_ref):
    y = jax.nn.gelu(yt_ref[...] + dsk_ref[...] * ut_ref[...])
    z = jnp.dot(wgt_ref[...], y.astype(BF16), preferred_element_type=F32) + bg_ref[...]
    o_ref[...] = (y * jax.nn.sigmoid(z)).T.astype(o_ref.dtype)


def _s5_post(yt, ust, d_skip, w_glu_t, b_glu, tn):
    w, tokens = yt.shape
    tn = min(tn, tokens)
    return pl.pallas_call(
        _s5_post_body,
        grid=(tokens // tn,),
        in_specs=[pl.BlockSpec((w, tn), lambda i: (0, i)),
                  pl.BlockSpec((w, tn), lambda i: (0, i)),
                  pl.BlockSpec((w, 1), lambda i: (0, 0)),
                  pl.BlockSpec((w, w), lambda i: (0, 0)),
                  pl.BlockSpec((w, 1), lambda i: (0, 0))],
        out_specs=pl.BlockSpec((tn, w), lambda i: (i, 0)),
        out_shape=jax.ShapeDtypeStruct((tokens, w), BF16),
        compiler_params=_cparams(("parallel",)),
        name="s5_post",
    )(yt, ust, d_skip.reshape(w, 1), w_glu_t, b_glu.reshape(w, 1))


def _mlstm_body(xm_ref, vm_ref, om_ref, ifc_ref, ifr_ref, bif_ref, cw_ref, cb_ref, wq_ref, wk_ref,
                gh_ref, sk_ref, o_ref, xc_s, q_s, k_s, cmat, nvec, mrun, *, seq):
    t_len, dh = ML_T, ML_HEAD_DIM
    xm = xm_ref[0].astype(F32)
    rowi = lax.broadcasted_iota(jnp.int32, (seq, 1), 0)
    cw = cw_ref[...]
    acc = cw[ML_CONV - 1:ML_CONV, :] * xm + cb_ref[...]
    for j in range(1, ML_CONV):
        shifted = jnp.where(rowi >= j, pltpu.roll(xm, j, axis=0), 0.0)
        acc = acc + cw[ML_CONV - 1 - j:ML_CONV - j, :] * shifted
    xc = acc * jax.nn.sigmoid(acc)
    xc_s[...] = xc
    xcb = xc.astype(BF16)
    q_s[...] = jnp.dot(xcb, wq_ref[0], preferred_element_type=F32).astype(BF16)
    k_s[...] = (jnp.dot(xcb, wk_ref[0], preferred_element_type=F32) * (dh ** -0.5)).astype(BF16)

    cmat[...] = jnp.zeros_like(cmat)
    nvec[...] = jnp.zeros_like(nvec)
    mrun[...] = jnp.full_like(mrun, ML_NEG)
    b_i, b_f = bif_ref[0][:, 0:1], bif_ref[0][:, 1:2]
    iota_r = lax.broadcasted_iota(jnp.int32, (t_len, t_len), 0)
    iota_c = lax.broadcasted_iota(jnp.int32, (t_len, t_len), 1)
    tri = iota_c <= iota_r
    tri_t = iota_r <= iota_c

    def step(c, carry):
        r0 = pl.multiple_of(c * t_len, t_len)
        rows = pl.ds(r0, t_len)
        ifc = ifc_ref[0, 0, rows, :]
        ifr = ifr_ref[0, 0, :, rows]
        i_col, i_row = ifc[:, 0:1] + b_i, ifr[0:1, :] + b_i
        lf_col = jax.nn.log_sigmoid(ifc[:, 1:2] + b_f)
        lf_row = jax.nn.log_sigmoid(ifr[1:2, :] + b_f)
        bcum_col = jnp.sum(jnp.where(tri, lf_row, 0.0), axis=1, keepdims=True)
        bcum_row = jnp.sum(jnp.where(tri_t, lf_col, 0.0), axis=0, keepdims=True)
        b_last = jnp.sum(lf_row, axis=1, keepdims=True)
        m_prev = mrun[...]
        dmat = jnp.where(tri, bcum_col - bcum_row + i_row, -jnp.inf)
        inter = bcum_col + m_prev
        m_row = jnp.maximum(jnp.max(dmat, axis=1, keepdims=True), inter)
        w_intra = jnp.exp(dmat - m_row)
        w_inter = jnp.exp(inter - m_row)
        qq, kk, vv = q_s[rows, :], k_s[rows, :], vm_ref[0, rows, :]
        s = lax.dot_general(qq, kk, _NT, preferred_element_type=F32) * w_intra
        cm = cmat[...]
        num = (jnp.dot(s.astype(BF16), vv, preferred_element_type=F32)
               + w_inter * lax.dot_general(qq, cm.astype(BF16), _NT, preferred_element_type=F32))
        den = (jnp.sum(s, axis=1, keepdims=True)
               + w_inter * jnp.sum(qq.astype(F32) * nvec[...], axis=1, keepdims=True))
        h = num / jnp.maximum(jnp.abs(den), jnp.exp(-m_row))
        g_col = b_last - bcum_col + i_col
        m_new = jnp.maximum(b_last + m_prev, jnp.max(g_col, axis=0, keepdims=True))
        wg = jnp.exp(g_col - m_new)
        decay = jnp.exp(b_last + m_prev - m_new)
        vw = (vv.astype(F32) * wg).astype(BF16)
        cmat[...] = decay * cm + lax.dot_general(vw, kk, _TN, preferred_element_type=F32)
        nvec[...] = decay * nvec[...] + jnp.sum(wg * kk.astype(F32), axis=0, keepdims=True)
        mrun[...] = m_new
        hn = _rms(h, gh_ref[...]) + sk_ref[...] * xc_s[rows, :]
        o_ref[0, rows, :] = (jax.nn.sigmoid(om_ref[0, rows, :].astype(F32)) * hn).astype(o_ref.dtype)
        return carry

    lax.fori_loop(0, seq // t_len, step, 0)


def _mlstm(proj3, ifc, ifr, bif, conv_w, conv_b, wq, wk, g_h, skip, col0):
    b, seq, _ = proj3.shape
    dh = ML_HEAD_DIM
    c0 = col0 // dh
    blk = (1, seq, dh)
    vec = pl.BlockSpec((1, dh), lambda i, h: (0, h))
    return pl.pallas_call(
        functools.partial(_mlstm_body, seq=seq),
        grid=(b, ML_HEADS),
        in_specs=[pl.BlockSpec(blk, lambda i, h: (i, 0, c0 + h)),
                  pl.BlockSpec(blk, lambda i, h: (i, 0, c0 + ML_HEADS + h)),
                  pl.BlockSpec(blk, lambda i, h: (i, 0, c0 + 2 * ML_HEADS + h)),
                  pl.BlockSpec((1, 1, seq, 2), lambda i, h: (i, h, 0, 0)),
                  pl.BlockSpec((1, 1, 2, seq), lambda i, h: (i, h, 0, 0)),
                  pl.BlockSpec((1, 1, 2), lambda i, h: (h, 0, 0)),
                  pl.BlockSpec((ML_CONV, dh), lambda i, h: (0, h)),
                  vec,
                  pl.BlockSpec((1, dh, dh), lambda i, h: (h, 0, 0)),
                  pl.BlockSpec((1, dh, dh), lambda i, h: (h, 0, 0)),
                  vec, vec],
        out_specs=pl.BlockSpec(blk, lambda i, h: (i, 0, h)),
        out_shape=jax.ShapeDtypeStruct((b, seq, ML_WIDTH), BF16),
        scratch_shapes=[pltpu.VMEM((seq, dh), F32),
                        pltpu.VMEM((seq, dh), BF16),
                        pltpu.VMEM((seq, dh), BF16),
                        pltpu.VMEM((dh, dh), F32),
                        pltpu.VMEM((1, dh), F32),
                        pltpu.VMEM((1, 1), F32)],
        compiler_params=_cparams(("parallel", "parallel")),
        name="mlstm",
    )(proj3, proj3, proj3, ifc, ifr, bif, conv_w, conv_b.reshape(1, -1), wq, wk,
      g_h.reshape(1, -1), skip.reshape(1, -1))


def _merge_body(x_ref, ya_ref, ys_ref, ym_ref, ga_ref, gs_ref, gm_ref, bg_ref,
                wa_ref, ws_ref, wm_ref, wo_ref, o_ref):
    d = x_ref.shape[1]
    bg = bg_ref[...]

    def branch(y_ref, w_ref, g_ref, k):
        gate = jax.nn.sigmoid(g_ref[...].astype(F32) + bg[:, k * d:(k + 1) * d])
        return gate * jnp.dot(y_ref[...], w_ref[...], preferred_element_type=F32)

    merged = branch(ya_ref, wa_ref, ga_ref, 0) + branch(ys_ref, ws_ref, gs_ref, 1) \
        + branch(ym_ref, wm_ref, gm_ref, 2)
    o_ref[...] = x_ref[...] + jnp.dot(merged.astype(BF16), wo_ref[...], preferred_element_type=F32)


def _merge(x2, ya, ys, ym, proj, gate_col0, b_gate, w_a, w_s, w_m, w_o, tm):
    m, d = x2.shape
    tm = min(tm, m)
    g0 = gate_col0 // d
    row = lambda w: pl.BlockSpec((tm, w), lambda i: (i, 0))
    gate = lambda k: pl.BlockSpec((tm, d), lambda i: (i, g0 + k))
    return pl.pallas_call(
        _merge_body,
        grid=(m // tm,),
        in_specs=[row(d), row(ya.shape[1]), row(ys.shape[1]), row(ym.shape[1]),
                  gate(0), gate(1), gate(2), _const_spec((1, 3 * d)),
                  _const_spec(w_a.shape), _const_spec(w_s.shape), _const_spec(w_m.shape),
                  _const_spec(w_o.shape)],
        out_specs=row(d),
        out_shape=jax.ShapeDtypeStruct((m, d), F32),
        compiler_params=_cparams(("parallel",)),
        name="merge",
    )(x2, ya, ys, ym, proj, proj, proj, b_gate.reshape(1, -1), w_a, w_s, w_m, w_o)


def _xattn_body(x_ref, q_ref, kv_ref, gq_ref, gk_ref, wo_ref, o_ref, att_s):
    d = x_ref.shape[1]
    dh = d // MEM_HEADS
    scale = dh ** -0.5
    for h in range(MEM_HEADS):
        cols = slice(h * dh, (h + 1) * dh)
        qn = _rms(q_ref[:, cols].astype(F32), gq_ref[...]).astype(BF16)
        kn = _rms(kv_ref[0, :, cols].astype(F32), gk_ref[...]).astype(BF16)
        s = lax.dot_general(qn, kn, _NT, preferred_element_type=F32) * scale
        p = jnp.exp(s - jnp.max(s, axis=-1, keepdims=True))
        l = jnp.sum(p, axis=-1, keepdims=True)
        v = kv_ref[0, :, d + h * dh:d + (h + 1) * dh]
        att_s[:, cols] = (jnp.dot(p.astype(BF16), v, preferred_element_type=F32) / l).astype(BF16)
    o_ref[...] = x_ref[...] + jnp.dot(att_s[...], wo_ref[...], preferred_element_type=F32)


def _xattn(x2, q, kv3, g_q, g_k, w_o, seq, tm):
    m, d = x2.shape
    tm = min(tm, seq)
    per_seq = seq // tm
    return pl.pallas_call(
        _xattn_body,
        grid=(m // tm,),
        in_specs=[pl.BlockSpec((tm, d), lambda i: (i, 0)),
                  pl.BlockSpec((tm, d), lambda i: (i, 0)),
                  pl.BlockSpec((1,) + kv3.shape[1:], lambda i: (i // per_seq, 0, 0)),
                  pl.BlockSpec((1, d // MEM_HEADS), lambda i: (0, 0)),
                  pl.BlockSpec((1, d // MEM_HEADS), lambda i: (0, 0)),
                  _const_spec(w_o.shape)],
        out_specs=pl.BlockSpec((tm, d), lambda i: (i, 0)),
        out_shape=jax.ShapeDtypeStruct((m, d), F32),
        scratch_shapes=[pltpu.VMEM((tm, d), BF16)],
        compiler_params=_cparams(("parallel",)),
        name="mem_attention",
    )(x2, q, kv3, g_q.reshape(1, -1), g_k.reshape(1, -1), w_o)


def _ffn_body(x_ref, g_ref, wg_ref, wu_ref, wd_ref, o_ref, hn_ref, acc_ref):
    j = pl.program_id(1)

    @pl.when(j == 0)
    def _():
        hn_ref[...] = _rms(x_ref[...], g_ref[...]).astype(BF16)
        acc_ref[...] = jnp.zeros_like(acc_ref)

    hn = hn_ref[...]
    gate = jnp.dot(hn, wg_ref[...], preferred_element_type=F32)
    up = jnp.dot(hn, wu_ref[...], preferred_element_type=F32)
    act = (gate * jax.nn.sigmoid(gate) * up).astype(BF16)
    acc_ref[...] += jnp.dot(act, wd_ref[...], preferred_element_type=F32)

    @pl.when(j == pl.num_programs(1) - 1)
    def _():
        o_ref[...] = x_ref[...] + acc_ref[...]


def _ffn(x2, g, w_gu, w_down, tm, tf):
    m, d = x2.shape
    f = w_down.shape[0]
    tm = min(tm, m)
    nf = f // tf
    return pl.pallas_call(
        _ffn_body,
        grid=(m // tm, nf),
        in_specs=[pl.BlockSpec((tm, d), lambda i, j: (i, 0)),
                  pl.BlockSpec((1, d), lambda i, j: (0, 0)),
                  pl.BlockSpec((d, tf), lambda i, j: (0, j)),
                  pl.BlockSpec((d, tf), lambda i, j: (0, nf + j)),
                  pl.BlockSpec((tf, d), lambda i, j: (j, 0))],
        out_specs=pl.BlockSpec((tm, d), lambda i, j: (i, 0)),
        out_shape=jax.ShapeDtypeStruct((m, d), F32),
        scratch_shapes=[pltpu.VMEM((tm, d), BF16), pltpu.VMEM((tm, d), F32)],
        compiler_params=_cparams(("parallel", "arbitrary")),
        name="swiglu",
    )(x2, g.reshape(1, d), w_gu, w_gu, w_down)


def kernel(x, mem, g_mem, norm_mix, w_in, b_gate, g_qa, g_ka, rel_bias, lam_re, lam_im, log_dt, b_re, b_im, c_re, c_im, d_skip, w_glu, b_glu, conv_w, conv_b, wq_m, wk_m, b_i, b_f, g_hm, skip_m, w_br_a, w_br_s, w_br_m, w_out, norm_x, w_xq, w_xkv, g_xq, g_xk, w_xo, norm_ffn, w_gu, w_down):
    b, seq, d = x.shape
    depth = w_in.shape[0]
    tokens = b * seq
    assert seq % ATT_QB == 0 and seq % ML_T == 0 and seq % SSM_T == 0
    c_us = 3 * ATT_WIDTH
    c_m = c_us + SSM_WIDTH
    c_if = c_m + 3 * ML_WIDTH
    c_gate = c_if + 2 * ML_HEADS
    ml_col0 = 3 * ATT_WIDTH
    gate_col0 = ml_col0 + 3 * ML_WIDTH

    x2 = x.reshape(tokens, d)
    mem2 = mem.reshape(b * mem.shape[1], d)
    for i in range(depth):
        w_i = w_in[i]
        w_main = jnp.concatenate([w_i[:, :c_us], w_i[:, c_m:c_if], w_i[:, c_gate:]], axis=1).astype(BF16)
        w_us_t = w_i[:, c_us:c_m].T.astype(BF16)
        w_if = jnp.pad(w_i[:, c_if:c_gate], ((0, 0), (0, LANES - 2 * ML_HEADS))).astype(BF16)
        proj, ust, ifp = _in_proj(x2, norm_mix[i], w_main, w_us_t, w_if, tm=512, tn=768)
        proj3 = proj.reshape(b, seq, -1)

        ya = _chunk_attention(proj3, g_qa[i], g_ka[i], _att_bias(rel_bias[i]))

        yt = _s5_scan(ust, lam_re[i], lam_im[i], log_dt[i], b_re[i], b_im[i], c_re[i], c_im[i],
                      cpb=seq // SSM_T)
        ys = _s5_post(yt, ust, d_skip[i], w_glu[i].T.astype(BF16), b_glu[i], tn=1024)

        if4 = ifp[:, :2 * ML_HEADS].reshape(b, seq, 2, ML_HEADS)
        ifc = if4.transpose(0, 3, 1, 2)
        ifr = if4.transpose(0, 3, 2, 1)
        bif = jnp.stack([b_i[i], b_f[i]], axis=-1).reshape(ML_HEADS, 1, 2)
        ym = _mlstm(proj3, ifc, ifr, bif, conv_w[i], conv_b[i], wq_m[i].astype(BF16),
                    wk_m[i].astype(BF16), g_hm[i], skip_m[i], ml_col0)

        x2 = _merge(x2, ya.reshape(tokens, -1), ys, ym.reshape(tokens, -1), proj, gate_col0, b_gate[i],
                    w_br_a[i].astype(BF16), w_br_s[i].astype(BF16), w_br_m[i].astype(BF16),
                    w_out[i].astype(BF16), tm=256)

        q = _norm_proj(x2, norm_x[i], w_xq[i].astype(BF16), tm=512, tn=1024)
        kv = _norm_proj(mem2, g_mem, w_xkv[i].astype(BF16), tm=512, tn=1024)
        x2 = _xattn(x2, q, kv.reshape(b, mem.shape[1], 2 * d), g_xq[i], g_xk[i],
                    w_xo[i].astype(BF16), seq, tm=512)

        x2 = _ffn(x2, norm_ffn[i], w_gu[i].astype(BF16), w_down[i].astype(BF16), tm=512, tf=512)
    return x2.reshape(b, seq, d)
```

```python
import functools

import jax
import jax.numpy as jnp
from jax import lax
from jax.experimental import pallas as pl
from jax.experimental.pallas import tpu as pltpu

F32 = jnp.float32
BF16 = jnp.bfloat16

EPS = 1e-6
LANES = 128
BF16_ROWS = 16
ATT_CHUNK = 64
ATT_HEADS = 8
ATT_HEAD_DIM = 128
ATT_WIDTH = ATT_HEADS * ATT_HEAD_DIM
ATT_LEFT = 8
REL_CLIP = 256
ATT_QB = 256
ATT_PAD = ATT_LEFT * ATT_CHUNK
ATT_KW = ATT_QB + ATT_PAD
ATT_EXT = 1024
MASK_NEG = -1e30
SSM_GROUP = 16
SSM_GROUPS = 48
SSM_WIDTH = SSM_GROUP * SSM_GROUPS
SSM_STATE = 64
SSM_T = LANES
ML_HEADS = 4
ML_HEAD_DIM = 256
ML_WIDTH = ML_HEADS * ML_HEAD_DIM
ML_CONV = 4
ML_T = 256
ML_NEG = -1e30
MEM_HEADS = 4
ROW_GROUP = 4096
NORM_ROWS = 256
COL_TILE = 256
DOT_ROWS = 1024

VMEM_LIMIT = 56 * 1024 * 1024

_NT = (((1,), (1,)), ((), ()))
_TN = (((0,), (0,)), ((), ()))


def _cparams(sem):
    return pltpu.CompilerParams(dimension_semantics=sem, vmem_limit_bytes=VMEM_LIMIT)


def _rms(x, g):
    return x * lax.rsqrt(jnp.mean(x * x, axis=-1, keepdims=True) + EPS) * g


def _resident(shape, index_map):
    return pl.BlockSpec(shape, index_map, pipeline_mode=pl.Buffered(1))


def _fill_norm(x_ref, g_ref, hn_ref, step):
    r0 = pl.multiple_of(step * NORM_ROWS, NORM_ROWS)
    hn_ref[pl.ds(r0, NORM_ROWS), :] = _rms(x_ref[...], g_ref[...]).astype(BF16)


def _in_proj_body(x_ref, g_ref, wm_ref, wg_ref, we_ref, p_ref, u_ref, gt_ref, if_ref, hn_ref,
                  *, n_fill, n_att, n_ssm, n_ml, n_gate, if_cols):
    rows = hn_ref.shape[0]
    s = pl.program_id(1)
    t = s - n_fill
    t_gate = t - (n_att + n_ssm + n_ml)

    @pl.when(s < n_fill)
    def _():
        _fill_norm(x_ref, g_ref, hn_ref, s)

    def project(w, o_ref):
        for r in range(0, rows, DOT_ROWS):
            o_ref[r:r + DOT_ROWS, :] = jnp.dot(
                hn_ref[r:r + DOT_ROWS, :], w, preferred_element_type=F32).astype(o_ref.dtype)

    in_ssm = (t >= n_att) & (t < n_att + n_ssm)

    @pl.when((t >= 0) & (t_gate < 0) & jnp.logical_not(in_ssm))
    def _():
        project(wm_ref[0].astype(BF16), p_ref)

    @pl.when(in_ssm)
    def _():
        wt = wm_ref[0].astype(BF16).T
        for r in range(0, rows, DOT_ROWS):
            ut = lax.dot_general(wt, hn_ref[r:r + DOT_ROWS, :], _NT, preferred_element_type=F32)
            for c in range(DOT_ROWS // SSM_T):
                u_ref[r // SSM_T + c] = ut[:, c * SSM_T:(c + 1) * SSM_T]

    @pl.when(t_gate == 0)
    def _():
        wif = wg_ref[0][:, :LANES].T[:BF16_ROWS].astype(BF16)
        for r in range(0, rows, DOT_ROWS):
            if_ref[:, r:r + DOT_ROWS] = lax.dot_general(
                wif, hn_ref[r:r + DOT_ROWS, :], _NT, preferred_element_type=F32)

    @pl.when(t_gate >= 0)
    def _():
        wfull = jnp.concatenate([wg_ref[0], we_ref[0]], axis=1)
        project(wfull[:, if_cols:if_cols + COL_TILE].astype(BF16), gt_ref)


def _in_proj(x2, g, w_in, layer):
    m, d = x2.shape
    rows = min(ROW_GROUP, m)
    n_fill = rows // NORM_ROWS
    n_att = 3 * ATT_WIDTH // COL_TILE
    n_ssm = SSM_WIDTH // COL_TILE
    n_ml = 3 * ML_WIDTH // COL_TILE
    n_main = n_att + n_ssm + n_ml
    if_cols = 2 * ML_HEADS
    gate_w = w_in.shape[2] - n_main * COL_TILE - if_cols
    n_gate = gate_w // COL_TILE
    assert gate_w % COL_TILE == 0 and if_cols <= BF16_ROWS
    ext0 = (n_main + 1) * COL_TILE // LANES
    ext_step = COL_TILE // LANES

    def t_of(s):
        return s - n_fill

    def p_col(s):
        t = t_of(s)
        return jnp.where(t < n_att, jnp.clip(t, 0, n_att - 1), jnp.clip(t - n_ssm, n_att, n_att + n_ml - 1))

    def g_idx(s):
        return jnp.clip(t_of(s) - n_main, 0, n_gate - 1)

    body = functools.partial(_in_proj_body, n_fill=n_fill, n_att=n_att, n_ssm=n_ssm, n_ml=n_ml,
                             n_gate=n_gate, if_cols=if_cols)
    return pl.pallas_call(
        body,
        grid=(m // rows, n_fill + n_main + n_gate),
        in_specs=[pl.BlockSpec((NORM_ROWS, d), lambda h, s: (h * n_fill + jnp.minimum(s, n_fill - 1), 0)),
                  pl.BlockSpec((1, d), lambda h, s: (0, 0)),
                  pl.BlockSpec((1, d, COL_TILE), lambda h, s: (layer, 0, jnp.clip(t_of(s), 0, n_main - 1))),
                  pl.BlockSpec((1, d, COL_TILE), lambda h, s: (layer, 0, n_main + g_idx(s))),
                  pl.BlockSpec((1, d, LANES), lambda h, s: (layer, 0, ext0 + ext_step * g_idx(s)))],
        out_specs=[pl.BlockSpec((rows, COL_TILE), lambda h, s: (h, p_col(s))),
                   pl.BlockSpec((rows // SSM_T, COL_TILE, SSM_T),
                                lambda h, s: (h, jnp.clip(t_of(s) - n_att, 0, n_ssm - 1), 0)),
                   pl.BlockSpec((rows, COL_TILE), lambda h, s: (h, g_idx(s))),
                   pl.BlockSpec((BF16_ROWS, rows), lambda h, s: (0, h))],
        out_shape=[jax.ShapeDtypeStruct((m, (n_att + n_ml) * COL_TILE), BF16),
                   jax.ShapeDtypeStruct((m // SSM_T, SSM_WIDTH, SSM_T), F32),
                   jax.ShapeDtypeStruct((m, gate_w), BF16),
                   jax.ShapeDtypeStruct((BF16_ROWS, m), F32)],
        scratch_shapes=[pltpu.VMEM((rows, d), BF16)],
        compiler_params=_cparams(("parallel", "arbitrary")),
        name="in_proj",
    )(x2, g.reshape(1, d), w_in, w_in, w_in)


def _norm_proj_body(x_ref, g_ref, w_ref, o_ref, hn_ref):
    @pl.when(pl.program_id(1) == 0)
    def _():
        hn_ref[...] = _rms(x_ref[...], g_ref[...]).astype(BF16)

    o_ref[...] = jnp.dot(hn_ref[...], w_ref[0], preferred_element_type=F32).astype(o_ref.dtype)


def _norm_proj(x2, g, w, layer, tm, tn):
    m, d = x2.shape
    n = w.shape[2]
    tm = min(tm, m)
    return pl.pallas_call(
        _norm_proj_body,
        grid=(m // tm, n // tn),
        in_specs=[pl.BlockSpec((tm, d), lambda i, j: (i, 0)),
                  pl.BlockSpec((1, d), lambda i, j: (0, 0)),
                  pl.BlockSpec((1, d, tn), lambda i, j: (layer, 0, j))],
        out_specs=pl.BlockSpec((tm, tn), lambda i, j: (i, j)),
        out_shape=jax.ShapeDtypeStruct((m, n), BF16),
        scratch_shapes=[pltpu.VMEM((tm, d), BF16)],
        compiler_params=_cparams(("parallel", "arbitrary")),
        name="norm_proj",
    )(x2, g.reshape(1, d), w)


def _att_bias_body(ext_ref, o_ref):
    ext = ext_ref[0]
    base = pltpu.roll(ext, ATT_EXT - (ATT_QB - 1), axis=1)
    slab = jnp.broadcast_to(base, (ATT_QB, ATT_EXT))
    tab = pltpu.roll(slab, 0, 1, stride=1, stride_axis=0)[:, :ATT_KW]
    qc = lax.broadcasted_iota(jnp.int32, (ATT_QB, ATT_KW), 0) // ATT_CHUNK
    kc = lax.broadcasted_iota(jnp.int32, (ATT_QB, ATT_KW), 1) // ATT_CHUNK
    o_ref[0] = jnp.where((kc >= qc) & (kc <= qc + ATT_LEFT), tab, MASK_NEG)


def _att_bias(rel_bias):
    h = rel_bias.shape[0]
    ext = jnp.concatenate(
        [jnp.broadcast_to(rel_bias[:, :1], (h, ATT_EXT - 2 * REL_CLIP - 1)), rel_bias], axis=1)
    return pl.pallas_call(
        _att_bias_body,
        grid=(h,),
        in_specs=[pl.BlockSpec((1, 1, ATT_EXT), lambda i: (i, 0, 0))],
        out_specs=pl.BlockSpec((1, ATT_QB, ATT_KW), lambda i: (i, 0, 0)),
        out_shape=jax.ShapeDtypeStruct((h, ATT_QB, ATT_KW), F32),
        compiler_params=_cparams(("parallel",)),
        name="att_bias",
    )(ext.reshape(h, 1, ATT_EXT))


def _attn_body(q_ref, k_ref, v_ref, gq_ref, gk_ref, bias_ref, o_ref, kpad, vpad, *, seq):
    kpad[0:ATT_PAD, :] = jnp.zeros((ATT_PAD, ATT_HEAD_DIM), BF16)
    vpad[0:ATT_PAD, :] = jnp.zeros((ATT_PAD, ATT_HEAD_DIM), BF16)
    kpad[ATT_PAD:, :] = _rms(k_ref[0].astype(F32), gk_ref[...]).astype(BF16)
    vpad[ATT_PAD:, :] = v_ref[0]
    bias = bias_ref[0]
    scale = ATT_HEAD_DIM ** -0.5
    for qb in range(seq // ATT_QB):
        r0 = qb * ATT_QB
        qn = _rms(q_ref[0, r0:r0 + ATT_QB, :].astype(F32), gq_ref[...]).astype(BF16)
        kw = kpad[r0:r0 + ATT_KW, :]
        s = lax.dot_general(qn, kw, _NT, preferred_element_type=F32) * scale + bias
        if r0 < ATT_PAD:
            col = lax.broadcasted_iota(jnp.int32, (ATT_QB, ATT_KW), 1)
            s = jnp.where(col + r0 >= ATT_PAD, s, MASK_NEG)
        m = jnp.max(s, axis=-1, keepdims=True)
        p = jnp.exp(s - m)
        l = jnp.sum(p, axis=-1, keepdims=True)
        o = jnp.dot(p.astype(BF16), vpad[r0:r0 + ATT_KW, :], preferred_element_type=F32)
        o_ref[0, r0:r0 + ATT_QB, :] = (o / l).astype(o_ref.dtype)


def _chunk_attention(proj3, gq, gk, bias):
    b, seq, _ = proj3.shape
    blk = (1, seq, ATT_HEAD_DIM)
    return pl.pallas_call(
        functools.partial(_attn_body, seq=seq),
        grid=(b, ATT_HEADS),
        in_specs=[pl.BlockSpec(blk, lambda i, h: (i, 0, h)),
                  pl.BlockSpec(blk, lambda i, h: (i, 0, ATT_HEADS + h)),
                  pl.BlockSpec(blk, lambda i, h: (i, 0, 2 * ATT_HEADS + h)),
                  pl.BlockSpec((1, ATT_HEAD_DIM), lambda i, h: (0, 0)),
                  pl.BlockSpec((1, ATT_HEAD_DIM), lambda i, h: (0, 0)),
                  pl.BlockSpec((1, ATT_QB, ATT_KW), lambda i, h: (h, 0, 0))],
        out_specs=pl.BlockSpec(blk, lambda i, h: (i, 0, h)),
        out_shape=jax.ShapeDtypeStruct((b, seq, ATT_WIDTH), BF16),
        scratch_shapes=[pltpu.VMEM((seq + ATT_PAD, ATT_HEAD_DIM), BF16),
                        pltpu.VMEM((seq + ATT_PAD, ATT_HEAD_DIM), BF16)],
        compiler_params=_cparams(("parallel", "parallel")),
        name="chunk_attention",
    )(proj3, proj3, proj3, gq.reshape(1, -1), gk.reshape(1, -1), bias)


def _cmul(xr, xi, yr, yi):
    return xr * yr - xi * yi, xr * yi + xi * yr


def _s5_body(u_ref, lr2_ref, li2_ref, ldt_ref, brt_ref, bit_ref, cr_ref, ci_ref, crt_ref, cit_ref,
             yt_ref, mfull, fmat, emat, kflat, *, cpb):
    t_len, n_ch, n_st = SSM_T, SSM_GROUP, SSM_STATE
    nch = u_ref.shape[0]
    dt = jnp.exp(ldt_ref[0])
    lr2, li2 = lr2_ref[0], li2_ref[0]
    first_half = lax.broadcasted_iota(jnp.int32, (1, 2 * n_st), 1) < n_st

    mag = jnp.exp(lr2 * dt)
    ar, ai = mag * jnp.cos(li2 * dt), mag * jnp.sin(li2 * dt)
    den = lr2 * lr2 + li2 * li2
    nr, ni = ar - 1.0, ai
    zr, zi = (nr * lr2 + ni * li2) / den, (ni * lr2 - nr * li2) / den
    bbr = zr * brt_ref[0] - zi * bit_ref[0]
    bbi = zr * bit_ref[0] + zi * brt_ref[0]

    n_bits = t_len.bit_length() - 1
    sq = [(ar, ai)]
    while (1 << (len(sq) - 1)) * 2 < t_len * cpb:
        sq.append(_cmul(*sq[-1], *sq[-1]))

    def power_rows(e):
        pr = pi = None
        for k in range(n_bits):
            bit = ((e >> k) & 1) == 1
            fr, fi = jnp.where(bit, sq[k][0], 1.0), jnp.where(bit, sq[k][1], 0.0)
            pr, pi = (fr, fi) if pr is None else _cmul(pr, pi, fr, fi)
        return pr, pi

    e_row = lax.broadcasted_iota(jnp.int32, (t_len, 2 * n_st), 0)
    r0r, r0i = power_rows(e_row)
    qr, qi = power_rows(t_len - 1 - e_row)
    r1r, r1i = _cmul(r0r, r0i, ar, ai)
    p0r, p0i, p1r, p1i = r0r.T, r0i.T, r1r.T, r1i.T

    cr, ci = cr_ref[0], ci_ref[0]
    coef_r, coef_i = [], []
    for mch in range(n_ch):
        br_m, bi_m = bbr[mch:mch + 1, :n_st], bbi[mch:mch + 1, :n_st]
        coef_r.append(cr * br_m - ci * bi_m)
        coef_i.append(cr * bi_m + ci * br_m)
    coef_r = jnp.concatenate(coef_r, axis=0)
    coef_i = jnp.concatenate(coef_i, axis=0)
    kflat[...] = (jnp.dot(coef_r, p0r[:n_st], preferred_element_type=F32,
                          precision=lax.Precision.HIGHEST)
                  - jnp.dot(coef_i, p0i[:n_st], preferred_element_type=F32,
                            precision=lax.Precision.HIGHEST))

    sub = BF16_ROWS
    causal = (lax.broadcasted_iota(jnp.int32, (sub, t_len), 1)
              >= lax.broadcasted_iota(jnp.int32, (sub, t_len), 0))
    lane_u = lax.broadcasted_iota(jnp.int32, (sub // 2, t_len), 1)

    def build(mch, carry):
        for nn in range(n_ch):
            row = kflat[pl.ds(mch * n_ch + nn, 1), :]
            head = pltpu.roll(jnp.broadcast_to(row, (sub, t_len)), 0, 1, stride=1, stride_axis=0)
            head = jnp.where(causal, head, 0.0).astype(BF16)
            cols = slice(nn * t_len, (nn + 1) * t_len)
            mfull[pl.ds(pl.multiple_of(mch * t_len, t_len), sub), cols] = head
            packed = pltpu.bitcast(head, jnp.uint32)
            for j in range(1, t_len // sub):
                moved = jnp.where(lane_u >= sub * j, pltpu.roll(packed, sub * j, axis=1), 0)
                mfull[pl.ds(pl.multiple_of(mch * t_len + sub * j, sub), sub), cols] = (
                    pltpu.bitcast(moved.astype(jnp.uint32), BF16))
        return carry

    lax.fori_loop(0, n_ch, build, 0)

    for mch in range(n_ch):
        br_m, bi_m = bbr[mch:mch + 1, :], bbi[mch:mch + 1, :]
        fmat[mch * t_len:(mch + 1) * t_len, :] = jnp.where(
            first_half, qr * br_m - qi * bi_m, qr * bi_m + qi * br_m).astype(BF16)

    crt, cit = crt_ref[0], cit_ref[0]
    top = lax.broadcasted_iota(jnp.int32, (2 * n_st, 1), 0) < n_st
    for nn in range(n_ch):
        c_r, c_i = crt[:, nn:nn + 1], cit[:, nn:nn + 1]
        emat[:, nn * t_len:(nn + 1) * t_len] = jnp.where(
            top, c_r * p1r - c_i * p1i, -(c_r * p1i + c_i * p1r)).astype(BF16)

    xcat = jnp.concatenate([u_ref[:, mch, :].astype(BF16) for mch in range(n_ch)], axis=1)
    s_loc = jnp.dot(xcat, fmat[...], preferred_element_type=F32)

    cidx = lax.broadcasted_iota(jnp.int32, (nch, 1), 0) % cpb
    xin = jnp.where(cidx >= 1, pltpu.roll(s_loc, 1, axis=0), 0.0)
    d = 1
    while d < cpb:
        a_r, a_i = sq[n_bits + d.bit_length() - 1]
        a_is = jnp.where(first_half, -a_i, a_i)
        sh = pltpu.roll(xin, d, axis=0)
        contrib = sh * a_r + pltpu.roll(sh, n_st, axis=1) * a_is
        xin = xin + jnp.where(cidx >= d, contrib, 0.0)
        d *= 2

    y = (jnp.dot(xcat, mfull[...], preferred_element_type=F32)
         + jnp.dot(xin.astype(BF16), emat[...], preferred_element_type=F32))
    for nn in range(n_ch):
        yt_ref[nn] = y[:, nn * t_len:(nn + 1) * t_len]


def _s5_scan(u3, lam_re, lam_im, log_dt, b_re, b_im, c_re, c_im, *, cpb):
    g, p, n = SSM_GROUPS, SSM_STATE, SSM_GROUP
    nch = u3.shape[0]
    dup = lambda a, axis: jnp.concatenate([a, a], axis=axis)
    lr2 = dup(lam_re, 1).reshape(g, 1, 2 * p)
    li2 = dup(lam_im, 1).reshape(g, 1, 2 * p)
    ldt = log_dt.reshape(g, 1, 1)
    brt = dup(jnp.swapaxes(b_re, 1, 2), 2)
    bit = dup(jnp.swapaxes(b_im, 1, 2), 2)
    crt = dup(jnp.swapaxes(c_re, 1, 2), 1)
    cit = dup(jnp.swapaxes(c_im, 1, 2), 1)
    spec = lambda *s: pl.BlockSpec((1,) + s, lambda i: (i, 0, 0))
    return pl.pallas_call(
        functools.partial(_s5_body, cpb=cpb),
        grid=(g,),
        in_specs=[pl.BlockSpec((nch, n, SSM_T), lambda i: (0, i, 0)),
                  spec(1, 2 * p), spec(1, 2 * p), spec(1, 1),
                  spec(n, 2 * p), spec(n, 2 * p), spec(n, p), spec(n, p),
                  spec(2 * p, n), spec(2 * p, n)],
        out_specs=pl.BlockSpec((n, nch, SSM_T), lambda i: (i, 0, 0)),
        out_shape=jax.ShapeDtypeStruct((SSM_WIDTH, nch, SSM_T), F32),
        scratch_shapes=[pltpu.VMEM((n * SSM_T, n * SSM_T), BF16),
                        pltpu.VMEM((n * SSM_T, 2 * p), BF16),
                        pltpu.VMEM((2 * p, n * SSM_T), BF16),
                        pltpu.VMEM((n * n, SSM_T), F32)],
        compiler_params=_cparams(("parallel",)),
        name="s5_scan",
    )(u3, lr2, li2, ldt, brt, bit, c_re, c_im, crt, cit)


def _s5_post_body(yt_ref, u_ref, dsk_ref, wgt_ref, bg_ref, o_ref):
    nc = u_ref.shape[0]
    yt = jnp.concatenate([yt_ref[:, c, :] for c in range(nc)], axis=1)
    ut = jnp.concatenate([u_ref[c] for c in range(nc)], axis=1)
    y = jax.nn.gelu(yt + dsk_ref[...] * ut)
    z = jnp.dot(wgt_ref[0], y.astype(BF16), preferred_element_type=F32) + bg_ref[...]
    o_ref[...] = (y * jax.nn.sigmoid(z)).T.astype(o_ref.dtype)


def _s5_post(yt3, u3, d_skip, w_glu_t, layer, b_glu, chunks):
    w, nch, t_len = yt3.shape
    chunks = min(chunks, nch)
    return pl.pallas_call(
        _s5_post_body,
        grid=(nch // chunks,),
        in_specs=[pl.BlockSpec((w, chunks, t_len), lambda i: (0, i, 0)),
                  pl.BlockSpec((chunks, w, t_len), lambda i: (i, 0, 0)),
                  pl.BlockSpec((w, 1), lambda i: (0, 0)),
                  pl.BlockSpec((1, w, w), lambda i: (layer, 0, 0)),
                  pl.BlockSpec((w, 1), lambda i: (0, 0))],
        out_specs=pl.BlockSpec((chunks * t_len, w), lambda i: (i, 0)),
        out_shape=jax.ShapeDtypeStruct((nch * t_len, w), BF16),
        compiler_params=_cparams(("parallel",)),
        name="s5_post",
    )(yt3, u3, d_skip.reshape(w, 1), w_glu_t, b_glu.reshape(w, 1))


def _mlstm_body(xm_ref, vm_ref, om_ref, ifc_ref, ifr_ref, bif_ref, cw_ref, cb_ref, wq_ref, wk_ref,
                gh_ref, sk_ref, o_ref, xc_s, q_s, k_s, cmat, nvec, mrun, *, seq):
    t_len, dh = ML_T, ML_HEAD_DIM
    xm = xm_ref[0].astype(F32)
    rowi = lax.broadcasted_iota(jnp.int32, (seq, 1), 0)
    cw = cw_ref[...]
    acc = cw[ML_CONV - 1:ML_CONV, :] * xm + cb_ref[...]
    for j in range(1, ML_CONV):
        shifted = jnp.where(rowi >= j, pltpu.roll(xm, j, axis=0), 0.0)
        acc = acc + cw[ML_CONV - 1 - j:ML_CONV - j, :] * shifted
    xc = acc * jax.nn.sigmoid(acc)
    xc_s[...] = xc
    xcb = xc.astype(BF16)
    q_s[...] = jnp.dot(xcb, wq_ref[0, 0], preferred_element_type=F32).astype(BF16)
    k_s[...] = (jnp.dot(xcb, wk_ref[0, 0], preferred_element_type=F32) * (dh ** -0.5)).astype(BF16)

    cmat[...] = jnp.zeros_like(cmat)
    nvec[...] = jnp.zeros_like(nvec)
    mrun[...] = jnp.full_like(mrun, ML_NEG)
    b_i, b_f = bif_ref[0][:, 0:1], bif_ref[0][:, 1:2]
    iota_r = lax.broadcasted_iota(jnp.int32, (t_len, t_len), 0)
    iota_c = lax.broadcasted_iota(jnp.int32, (t_len, t_len), 1)
    tri = iota_c <= iota_r
    tri_t = iota_r <= iota_c

    def step(c, carry):
        r0 = pl.multiple_of(c * t_len, t_len)
        rows = pl.ds(r0, t_len)
        ifc = ifc_ref[0, 0, rows, :]
        ifr = ifr_ref[0, 0, :, rows]
        i_col, i_row = ifc[:, 0:1] + b_i, ifr[0:1, :] + b_i
        lf_col = jax.nn.log_sigmoid(ifc[:, 1:2] + b_f)
        lf_row = jax.nn.log_sigmoid(ifr[1:2, :] + b_f)
        bcum_col = jnp.sum(jnp.where(tri, lf_row, 0.0), axis=1, keepdims=True)
        bcum_row = jnp.sum(jnp.where(tri_t, lf_col, 0.0), axis=0, keepdims=True)
        b_last = jnp.sum(lf_row, axis=1, keepdims=True)
        m_prev = mrun[...]
        dmat = jnp.where(tri, bcum_col - bcum_row + i_row, -jnp.inf)
        inter = bcum_col + m_prev
        m_row = jnp.maximum(jnp.max(dmat, axis=1, keepdims=True), inter)
        w_intra = jnp.exp(dmat - m_row)
        w_inter = jnp.exp(inter - m_row)
        qq, kk, vv = q_s[rows, :], k_s[rows, :], vm_ref[0, rows, :]
        s = lax.dot_general(qq, kk, _NT, preferred_element_type=F32) * w_intra
        cm = cmat[...]
        num = (jnp.dot(s.astype(BF16), vv, preferred_element_type=F32)
               + w_inter * lax.dot_general(qq, cm.astype(BF16), _NT, preferred_element_type=F32))
        den = (jnp.sum(s, axis=1, keepdims=True)
               + w_inter * jnp.sum(qq.astype(F32) * nvec[...], axis=1, keepdims=True))
        h = num / jnp.maximum(jnp.abs(den), jnp.exp(-m_row))
        g_col = b_last - bcum_col + i_col
        m_new = jnp.maximum(b_last + m_prev, jnp.max(g_col, axis=0, keepdims=True))
        wg = jnp.exp(g_col - m_new)
        decay = jnp.exp(b_last + m_prev - m_new)
        vw = (vv.astype(F32) * wg).astype(BF16)
        cmat[...] = decay * cm + lax.dot_general(vw, kk, _TN, preferred_element_type=F32)
        nvec[...] = decay * nvec[...] + jnp.sum(wg * kk.astype(F32), axis=0, keepdims=True)
        mrun[...] = m_new
        hn = _rms(h, gh_ref[...]) + sk_ref[...] * xc_s[rows, :]
        o_ref[0, rows, :] = (jax.nn.sigmoid(om_ref[0, rows, :].astype(F32)) * hn).astype(o_ref.dtype)
        return carry

    lax.fori_loop(0, seq // t_len, step, 0)


def _mlstm(proj3, ifc, ifr, bif, conv_w, conv_b, wq, wk, layer, g_h, skip, col0):
    b, seq, _ = proj3.shape
    dh = ML_HEAD_DIM
    c0 = col0 // dh
    blk = (1, seq, dh)
    vec = pl.BlockSpec((1, dh), lambda i, h: (0, h))
    wspec = pl.BlockSpec((1, 1, dh, dh), lambda i, h: (layer, h, 0, 0))
    return pl.pallas_call(
        functools.partial(_mlstm_body, seq=seq),
        grid=(b, ML_HEADS),
        in_specs=[pl.BlockSpec(blk, lambda i, h: (i, 0, c0 + h)),
                  pl.BlockSpec(blk, lambda i, h: (i, 0, c0 + ML_HEADS + h)),
                  pl.BlockSpec(blk, lambda i, h: (i, 0, c0 + 2 * ML_HEADS + h)),
                  pl.BlockSpec((1, 1, seq, 2), lambda i, h: (i, h, 0, 0)),
                  pl.BlockSpec((1, 1, 2, seq), lambda i, h: (i, h, 0, 0)),
                  pl.BlockSpec((1, 1, 2), lambda i, h: (h, 0, 0)),
                  pl.BlockSpec((ML_CONV, dh), lambda i, h: (0, h)),
                  vec, wspec, wspec, vec, vec],
        out_specs=pl.BlockSpec(blk, lambda i, h: (i, 0, h)),
        out_shape=jax.ShapeDtypeStruct((b, seq, ML_WIDTH), BF16),
        scratch_shapes=[pltpu.VMEM((seq, dh), F32),
                        pltpu.VMEM((seq, dh), BF16),
                        pltpu.VMEM((seq, dh), BF16),
                        pltpu.VMEM((dh, dh), F32),
                        pltpu.VMEM((1, dh), F32),
                        pltpu.VMEM((1, 1), F32)],
        compiler_params=_cparams(("parallel", "parallel")),
        name="mlstm",
    )(proj3, proj3, proj3, ifc, ifr, bif, conv_w, conv_b.reshape(1, -1), wq, wk,
      g_h.reshape(1, -1), skip.reshape(1, -1))


def _merge_body(x_ref, ya_ref, ys_ref, ym_ref, ga_ref, gs_ref, gm_ref, bg_ref,
                wa_ref, ws_ref, wm_ref, wo_ref, o_ref):
    d = x_ref.shape[1]
    bg = bg_ref[...]

    def branch(y_ref, w_ref, g_ref, k):
        gate = jax.nn.sigmoid(g_ref[...].astype(F32) + bg[:, k * d:(k + 1) * d])
        return gate * jnp.dot(y_ref[...], w_ref[0], preferred_element_type=F32)

    merged = branch(ya_ref, wa_ref, ga_ref, 0) + branch(ys_ref, ws_ref, gs_ref, 1) \
        + branch(ym_ref, wm_ref, gm_ref, 2)
    o_ref[...] = x_ref[...] + jnp.dot(merged.astype(BF16), wo_ref[0], preferred_element_type=F32)


def _merge(x2, ya, ys, ym, gates, b_gate, w_a, w_s, w_m, w_o, layer, tm):
    m, d = x2.shape
    tm = min(tm, m)
    row = lambda w: pl.BlockSpec((tm, w), lambda i: (i, 0))
    gate = lambda k: pl.BlockSpec((tm, d), lambda i: (i, k))
    wres = lambda w: _resident((1,) + w.shape[1:], lambda i: (layer, 0, 0))
    return pl.pallas_call(
        _merge_body,
        grid=(m // tm,),
        in_specs=[row(d), row(ya.shape[1]), row(ys.shape[1]), row(ym.shape[1]),
                  gate(0), gate(1), gate(2), _resident((1, 3 * d), lambda i: (0, 0)),
                  wres(w_a), wres(w_s), wres(w_m), wres(w_o)],
        out_specs=row(d),
        out_shape=jax.ShapeDtypeStruct((m, d), F32),
        compiler_params=_cparams(("parallel",)),
        name="merge",
    )(x2, ya, ys, ym, gates, gates, gates, b_gate.reshape(1, -1), w_a, w_s, w_m, w_o)


def _xattn_body(x_ref, g_ref, wq_ref, kv_ref, gq_ref, gk_ref, wo_ref, o_ref, q_s, att_s):
    d = x_ref.shape[1]
    dh = d // MEM_HEADS
    scale = dh ** -0.5
    hn = _rms(x_ref[...], g_ref[...]).astype(BF16)
    q_s[...] = jnp.dot(hn, wq_ref[0], preferred_element_type=F32)
    for h in range(MEM_HEADS):
        cols = slice(h * dh, (h + 1) * dh)
        qn = _rms(q_s[:, cols], gq_ref[...]).astype(BF16)
        kn = _rms(kv_ref[0, :, cols].astype(F32), gk_ref[...]).astype(BF16)
        s = lax.dot_general(qn, kn, _NT, preferred_element_type=F32) * scale
        p = jnp.exp(s - jnp.max(s, axis=-1, keepdims=True))
        l = jnp.sum(p, axis=-1, keepdims=True)
        v = kv_ref[0, :, d + h * dh:d + (h + 1) * dh]
        att_s[:, cols] = (jnp.dot(p.astype(BF16), v, preferred_element_type=F32) / l).astype(BF16)
    o_ref[...] = x_ref[...] + jnp.dot(att_s[...], wo_ref[0], preferred_element_type=F32)


def _xattn(x2, g, w_q, kv3, g_q, g_k, w_o, layer, seq, tm):
    m, d = x2.shape
    tm = min(tm, seq)
    per_seq = seq // tm
    wres = lambda w: _resident((1,) + w.shape[1:], lambda i: (layer, 0, 0))
    head = pl.BlockSpec((1, d // MEM_HEADS), lambda i: (0, 0))
    return pl.pallas_call(
        _xattn_body,
        grid=(m // tm,),
        in_specs=[pl.BlockSpec((tm, d), lambda i: (i, 0)),
                  pl.BlockSpec((1, d), lambda i: (0, 0)),
                  wres(w_q),
                  pl.BlockSpec((1,) + kv3.shape[1:], lambda i: (i // per_seq, 0, 0)),
                  head, head, wres(w_o)],
        out_specs=pl.BlockSpec((tm, d), lambda i: (i, 0)),
        out_shape=jax.ShapeDtypeStruct((m, d), F32),
        scratch_shapes=[pltpu.VMEM((tm, d), F32), pltpu.VMEM((tm, d), BF16)],
        compiler_params=_cparams(("parallel",)),
        name="mem_attention",
    )(x2, g.reshape(1, d), w_q, kv3, g_q.reshape(1, -1), g_k.reshape(1, -1), w_o)


def _ffn_up_body(x_ref, g_ref, wg_ref, wu_ref, a_ref, hn_ref, *, n_fill):
    rows = hn_ref.shape[0]
    s = pl.program_id(1)

    @pl.when(s < n_fill)
    def _():
        _fill_norm(x_ref, g_ref, hn_ref, s)

    @pl.when(s >= n_fill)
    def _():
        wg, wu = wg_ref[0].astype(BF16), wu_ref[0].astype(BF16)
        for r in range(0, rows, DOT_ROWS):
            hn = hn_ref[r:r + DOT_ROWS, :]
            gate = jnp.dot(hn, wg, preferred_element_type=F32)
            up = jnp.dot(hn, wu, preferred_element_type=F32)
            a_ref[r:r + DOT_ROWS, :] = (gate * jax.nn.sigmoid(gate) * up).astype(a_ref.dtype)


def _ffn_up(x2, g, w_gu, layer):
    m, d = x2.shape
    f = w_gu.shape[2] // 2
    rows = min(ROW_GROUP, m)
    n_fill = rows // NORM_ROWS
    n_tiles = f // COL_TILE
    tile = lambda s: jnp.clip(s - n_fill, 0, n_tiles - 1)
    return pl.pallas_call(
        functools.partial(_ffn_up_body, n_fill=n_fill),
        grid=(m // rows, n_fill + n_tiles),
        in_specs=[pl.BlockSpec((NORM_ROWS, d), lambda h, s: (h * n_fill + jnp.minimum(s, n_fill - 1), 0)),
                  pl.BlockSpec((1, d), lambda h, s: (0, 0)),
                  pl.BlockSpec((1, d, COL_TILE), lambda h, s: (layer, 0, tile(s))),
                  pl.BlockSpec((1, d, COL_TILE), lambda h, s: (layer, 0, n_tiles + tile(s)))],
        out_specs=pl.BlockSpec((rows, COL_TILE), lambda h, s: (h, tile(s))),
        out_shape=jax.ShapeDtypeStruct((m, f), BF16),
        scratch_shapes=[pltpu.VMEM((rows, d), BF16)],
        compiler_params=_cparams(("parallel", "arbitrary")),
        name="swiglu_up",
    )(x2, g.reshape(1, d), w_gu, w_gu)


def _ffn_down_body(x_ref, a_ref, w_ref, o_ref, wb_ref, *, n_load, k_tile):
    s = pl.program_id(0)

    @pl.when(s < n_load)
    def _():
        r0 = pl.multiple_of(s * k_tile, k_tile)
        wb_ref[pl.ds(r0, k_tile), :] = w_ref[0].astype(BF16)

    @pl.when(s >= n_load)
    def _():
        o_ref[...] = x_ref[...] + jnp.dot(a_ref[...], wb_ref[...], preferred_element_type=F32)


def _ffn_down(x2, act, w_down, layer, tm, k_tile):
    m, d = x2.shape
    f = act.shape[1]
    tm = min(tm, m)
    n_load = f // k_tile
    blk = lambda s: jnp.maximum(s - n_load, 0)
    return pl.pallas_call(
        functools.partial(_ffn_down_body, n_load=n_load, k_tile=k_tile),
        grid=(n_load + m // tm,),
        in_specs=[pl.BlockSpec((tm, d), lambda s: (blk(s), 0)),
                  pl.BlockSpec((tm, f), lambda s: (blk(s), 0)),
                  pl.BlockSpec((1, k_tile, d), lambda s: (layer, jnp.minimum(s, n_load - 1), 0))],
        out_specs=pl.BlockSpec((tm, d), lambda s: (blk(s), 0)),
        out_shape=jax.ShapeDtypeStruct((m, d), F32),
        scratch_shapes=[pltpu.VMEM((f, d), BF16)],
        compiler_params=_cparams(("arbitrary",)),
        name="swiglu_down",
    )(x2, act, w_down)


def kernel(x, mem, g_mem, norm_mix, w_in, b_gate, g_qa, g_ka, rel_bias, lam_re, lam_im, log_dt, b_re, b_im, c_re, c_im, d_skip, w_glu, b_glu, conv_w, conv_b, wq_m, wk_m, b_i, b_f, g_hm, skip_m, w_br_a, w_br_s, w_br_m, w_out, norm_x, w_xq, w_xkv, g_xq, g_xk, w_xo, norm_ffn, w_gu, w_down):
    b, seq, d = x.shape
    depth = w_in.shape[0]
    tokens = b * seq
    assert seq % ATT_QB == 0 and seq % ML_T == 0 and seq % SSM_T == 0
    assert tokens % min(ROW_GROUP, tokens) == 0

    w_glu_t = jnp.swapaxes(w_glu, 1, 2).astype(BF16)
    wq_b, wk_b = wq_m.astype(BF16), wk_m.astype(BF16)
    wa_b, ws_b, wm_b, wo_b = (w.astype(BF16) for w in (w_br_a, w_br_s, w_br_m, w_out))
    wxq_b, wxkv_b, wxo_b = w_xq.astype(BF16), w_xkv.astype(BF16), w_xo.astype(BF16)

    x2 = x.reshape(tokens, d)
    mem2 = mem.reshape(b * mem.shape[1], d)
    for i in range(depth):
        proj, u3, gates, ift = _in_proj(x2, norm_mix[i], w_in, i)
        proj3 = proj.reshape(b, seq, -1)

        ya = _chunk_attention(proj3, g_qa[i], g_ka[i], _att_bias(rel_bias[i]))

        yt3 = _s5_scan(u3, lam_re[i], lam_im[i], log_dt[i], b_re[i], b_im[i], c_re[i], c_im[i],
                       cpb=seq // SSM_T)
        ys = _s5_post(yt3, u3, d_skip[i], w_glu_t, i, b_glu[i], chunks=8)

        ifr = ift[:2 * ML_HEADS].reshape(2, ML_HEADS, b, seq).transpose(2, 1, 0, 3)
        ifc = jnp.swapaxes(ifr, 2, 3)
        bif = jnp.stack([b_i[i], b_f[i]], axis=-1).reshape(ML_HEADS, 1, 2)
        ym = _mlstm(proj3, ifc, ifr, bif, conv_w[i], conv_b[i], wq_b, wk_b, i, g_hm[i], skip_m[i],
                    3 * ATT_WIDTH)

        x2 = _merge(x2, ya.reshape(tokens, -1), ys, ym.reshape(tokens, -1), gates, b_gate[i],
                    wa_b, ws_b, wm_b, wo_b, i, tm=256)

        kv = _norm_proj(mem2, g_mem, wxkv_b, i, tm=512, tn=1024)
        x2 = _xattn(x2, norm_x[i], wxq_b, kv.reshape(b, mem.shape[1], 2 * d), g_xq[i], g_xk[i],
                    wxo_b, i, seq, tm=512)

        act = _ffn_up(x2, norm_ffn[i], w_gu, i)
        x2 = _ffn_down(x2, act, w_down, i, tm=256, k_tile=256)
    return x2.reshape(b, seq, d)
```

```python
import functools

import jax
import jax.numpy as jnp
from jax import lax
from jax.experimental import pallas as pl
from jax.experimental.pallas import tpu as pltpu

F32 = jnp.float32
BF16 = jnp.bfloat16

EPS = 1e-6
LANES = 128
BF16_ROWS = 16
ATT_CHUNK = 64
ATT_HEADS = 8
ATT_HEAD_DIM = 128
ATT_WIDTH = ATT_HEADS * ATT_HEAD_DIM
ATT_LEFT = 8
REL_CLIP = 256
ATT_QB = 256
ATT_PAD = ATT_LEFT * ATT_CHUNK
ATT_KW = ATT_QB + ATT_PAD
ATT_EXT = 1024
MASK_NEG = -1e30
SSM_GROUP = 16
SSM_GROUPS = 48
SSM_WIDTH = SSM_GROUP * SSM_GROUPS
SSM_STATE = 64
SSM_T = LANES
ML_HEADS = 4
ML_HEAD_DIM = 256
ML_WIDTH = ML_HEADS * ML_HEAD_DIM
ML_CONV = 4
ML_T = 256
ML_NEG = -1e30
MEM_HEADS = 4
ROW_GROUP = 4096
NORM_ROWS = 256
COL_TILE = 256
DOT_ROWS = 1024

VMEM_LIMIT = 56 * 1024 * 1024

_NT = (((1,), (1,)), ((), ()))
_TN = (((0,), (0,)), ((), ()))


def _cparams(sem):
    return pltpu.CompilerParams(dimension_semantics=sem, vmem_limit_bytes=VMEM_LIMIT)


def _rms(x, g):
    return x * lax.rsqrt(jnp.mean(x * x, axis=-1, keepdims=True) + EPS) * g


def _resident(shape, index_map):
    return pl.BlockSpec(shape, index_map, pipeline_mode=pl.Buffered(1))


def _fill_norm(x_ref, g_ref, hn_ref, step):
    r0 = pl.multiple_of(step * NORM_ROWS, NORM_ROWS)
    hn_ref[pl.ds(r0, NORM_ROWS), :] = _rms(x_ref[...], g_ref[...]).astype(BF16)


def _in_proj_body(x_ref, g_ref, wm_ref, wg_ref, we_ref, p_ref, u_ref, gt_ref, if_ref, hn_ref,
                  *, n_fill, n_att, n_ssm, n_ml, if_cols):
    rows = hn_ref.shape[0]
    s = pl.program_id(1)
    t = s - n_fill
    t_gate = t - (n_att + n_ssm + n_ml)

    @pl.when(s < n_fill)
    def _():
        _fill_norm(x_ref, g_ref, hn_ref, s)

    def project(wt, o_ref):
        for r in range(0, rows, DOT_ROWS):
            o_ref[r:r + DOT_ROWS, :] = lax.dot_general(
                hn_ref[r:r + DOT_ROWS, :], wt, _NT, preferred_element_type=F32).astype(o_ref.dtype)

    in_ssm = (t >= n_att) & (t < n_att + n_ssm)

    @pl.when((t >= 0) & (t_gate < 0) & jnp.logical_not(in_ssm))
    def _():
        project(wm_ref[0].astype(BF16), p_ref)

    @pl.when(in_ssm)
    def _():
        wt = wm_ref[0].astype(BF16)
        for r in range(0, rows, DOT_ROWS):
            ut = lax.dot_general(wt, hn_ref[r:r + DOT_ROWS, :], _NT, preferred_element_type=F32)
            for c in range(DOT_ROWS // SSM_T):
                u_ref[:, r // SSM_T + c, :] = ut[:, c * SSM_T:(c + 1) * SSM_T]

    @pl.when(t_gate == 0)
    def _():
        wif = wg_ref[0][:BF16_ROWS].astype(BF16)
        for r in range(0, rows, DOT_ROWS):
            if_ref[:, r:r + DOT_ROWS] = lax.dot_general(
                wif, hn_ref[r:r + DOT_ROWS, :], _NT, preferred_element_type=F32)

    @pl.when(t_gate >= 0)
    def _():
        wt = jnp.concatenate([wg_ref[0][if_cols:], we_ref[0]], axis=0)
        project(wt.astype(BF16), gt_ref)


def _in_proj(x2, g, w_in_t, layer):
    m, d = x2.shape
    rows = min(ROW_GROUP, m)
    n_fill = rows // NORM_ROWS
    n_att = 3 * ATT_WIDTH // COL_TILE
    n_ssm = SSM_WIDTH // COL_TILE
    n_ml = 3 * ML_WIDTH // COL_TILE
    n_main = n_att + n_ssm + n_ml
    if_cols = 2 * ML_HEADS
    gate_w = w_in_t.shape[1] - n_main * COL_TILE - if_cols
    n_gate = gate_w // COL_TILE
    assert gate_w % COL_TILE == 0 and if_cols <= BF16_ROWS and COL_TILE % if_cols == 0
    ext0 = (n_main + 1) * COL_TILE // if_cols
    ext_step = COL_TILE // if_cols

    def t_of(s):
        return s - n_fill

    def p_col(s):
        t = t_of(s)
        return jnp.where(t < n_att, jnp.clip(t, 0, n_att - 1), jnp.clip(t - n_ssm, n_att, n_att + n_ml - 1))

    def g_idx(s):
        return jnp.clip(t_of(s) - n_main, 0, n_gate - 1)

    body = functools.partial(_in_proj_body, n_fill=n_fill, n_att=n_att, n_ssm=n_ssm, n_ml=n_ml,
                             if_cols=if_cols)
    return pl.pallas_call(
        body,
        grid=(m // rows, n_fill + n_main + n_gate),
        in_specs=[pl.BlockSpec((NORM_ROWS, d), lambda h, s: (h * n_fill + jnp.minimum(s, n_fill - 1), 0)),
                  pl.BlockSpec((1, d), lambda h, s: (0, 0)),
                  pl.BlockSpec((1, COL_TILE, d), lambda h, s: (layer, jnp.clip(t_of(s), 0, n_main - 1), 0)),
                  pl.BlockSpec((1, COL_TILE, d), lambda h, s: (layer, n_main + g_idx(s), 0)),
                  pl.BlockSpec((1, if_cols, d), lambda h, s: (layer, ext0 + ext_step * g_idx(s), 0))],
        out_specs=[pl.BlockSpec((rows, COL_TILE), lambda h, s: (h, p_col(s))),
                   pl.BlockSpec((COL_TILE, rows // SSM_T, SSM_T),
                                lambda h, s: (jnp.clip(t_of(s) - n_att, 0, n_ssm - 1), h, 0)),
                   pl.BlockSpec((rows, COL_TILE), lambda h, s: (h, g_idx(s))),
                   pl.BlockSpec((BF16_ROWS, rows), lambda h, s: (0, h))],
        out_shape=[jax.ShapeDtypeStruct((m, (n_att + n_ml) * COL_TILE), BF16),
                   jax.ShapeDtypeStruct((SSM_WIDTH, m // SSM_T, SSM_T), F32),
                   jax.ShapeDtypeStruct((m, gate_w), BF16),
                   jax.ShapeDtypeStruct((BF16_ROWS, m), F32)],
        scratch_shapes=[pltpu.VMEM((rows, d), BF16)],
        compiler_params=_cparams(("parallel", "arbitrary")),
        name="in_proj",
    )(x2, g.reshape(1, d), w_in_t, w_in_t, w_in_t)


def _norm_proj_body(x_ref, g_ref, w_ref, o_ref, hn_ref):
    @pl.when(pl.program_id(1) == 0)
    def _():
        hn_ref[...] = _rms(x_ref[...], g_ref[...]).astype(BF16)

    o_ref[...] = jnp.dot(hn_ref[...], w_ref[0].astype(BF16),
                         preferred_element_type=F32).astype(o_ref.dtype)


def _norm_proj(x2, g, w, layer, tm, tn):
    m, d = x2.shape
    n = w.shape[2]
    tm = min(tm, m)
    return pl.pallas_call(
        _norm_proj_body,
        grid=(m // tm, n // tn),
        in_specs=[pl.BlockSpec((tm, d), lambda i, j: (i, 0)),
                  pl.BlockSpec((1, d), lambda i, j: (0, 0)),
                  pl.BlockSpec((1, d, tn), lambda i, j: (layer, 0, j))],
        out_specs=pl.BlockSpec((tm, tn), lambda i, j: (i, j)),
        out_shape=jax.ShapeDtypeStruct((m, n), BF16),
        scratch_shapes=[pltpu.VMEM((tm, d), BF16)],
        compiler_params=_cparams(("parallel", "arbitrary")),
        name="norm_proj",
    )(x2, g.reshape(1, d), w)


def _att_bias_body(ext_ref, o_ref):
    ext = ext_ref[0]
    base = pltpu.roll(ext, ATT_EXT - (ATT_QB - 1), axis=1)
    slab = jnp.broadcast_to(base, (ATT_QB, ATT_EXT))
    tab = pltpu.roll(slab, 0, 1, stride=1, stride_axis=0)[:, :ATT_KW]
    qc = lax.broadcasted_iota(jnp.int32, (ATT_QB, ATT_KW), 0) // ATT_CHUNK
    kc = lax.broadcasted_iota(jnp.int32, (ATT_QB, ATT_KW), 1) // ATT_CHUNK
    o_ref[0] = jnp.where((kc >= qc) & (kc <= qc + ATT_LEFT), tab, MASK_NEG)


def _att_bias(rel_bias):
    h = rel_bias.shape[0]
    ext = jnp.concatenate(
        [jnp.broadcast_to(rel_bias[:, :1], (h, ATT_EXT - 2 * REL_CLIP - 1)), rel_bias], axis=1)
    return pl.pallas_call(
        _att_bias_body,
        grid=(h,),
        in_specs=[pl.BlockSpec((1, 1, ATT_EXT), lambda i: (i, 0, 0))],
        out_specs=pl.BlockSpec((1, ATT_QB, ATT_KW), lambda i: (i, 0, 0)),
        out_shape=jax.ShapeDtypeStruct((h, ATT_QB, ATT_KW), F32),
        compiler_params=_cparams(("parallel",)),
        name="att_bias",
    )(ext.reshape(h, 1, ATT_EXT))


def _attn_body(q_ref, k_ref, v_ref, gq_ref, gk_ref, bias_ref, o_ref, kpad, vpad, *, seq):
    kpad[0:ATT_PAD, :] = jnp.zeros((ATT_PAD, ATT_HEAD_DIM), BF16)
    vpad[0:ATT_PAD, :] = jnp.zeros((ATT_PAD, ATT_HEAD_DIM), BF16)
    kpad[ATT_PAD:, :] = _rms(k_ref[0].astype(F32), gk_ref[...]).astype(BF16)
    vpad[ATT_PAD:, :] = v_ref[0]
    bias = bias_ref[0]
    scale = ATT_HEAD_DIM ** -0.5
    for qb in range(seq // ATT_QB):
        r0 = qb * ATT_QB
        qn = _rms(q_ref[0, r0:r0 + ATT_QB, :].astype(F32), gq_ref[...]).astype(BF16)
        kw = kpad[r0:r0 + ATT_KW, :]
        s = lax.dot_general(qn, kw, _NT, preferred_element_type=F32) * scale + bias
        if r0 < ATT_PAD:
            col = lax.broadcasted_iota(jnp.int32, (ATT_QB, ATT_KW), 1)
            s = jnp.where(col + r0 >= ATT_PAD, s, MASK_NEG)
        m = jnp.max(s, axis=-1, keepdims=True)
        p = jnp.exp(s - m)
        l = jnp.sum(p, axis=-1, keepdims=True)
        o = jnp.dot(p.astype(BF16), vpad[r0:r0 + ATT_KW, :], preferred_element_type=F32)
        o_ref[0, r0:r0 + ATT_QB, :] = (o / l).astype(o_ref.dtype)


def _chunk_attention(proj3, gq, gk, bias):
    b, seq, _ = proj3.shape
    blk = (1, seq, ATT_HEAD_DIM)
    return pl.pallas_call(
        functools.partial(_attn_body, seq=seq),
        grid=(b, ATT_HEADS),
        in_specs=[pl.BlockSpec(blk, lambda i, h: (i, 0, h)),
                  pl.BlockSpec(blk, lambda i, h: (i, 0, ATT_HEADS + h)),
                  pl.BlockSpec(blk, lambda i, h: (i, 0, 2 * ATT_HEADS + h)),
                  pl.BlockSpec((1, ATT_HEAD_DIM), lambda i, h: (0, 0)),
                  pl.BlockSpec((1, ATT_HEAD_DIM), lambda i, h: (0, 0)),
                  pl.BlockSpec((1, ATT_QB, ATT_KW), lambda i, h: (h, 0, 0))],
        out_specs=pl.BlockSpec(blk, lambda i, h: (i, 0, h)),
        out_shape=jax.ShapeDtypeStruct((b, seq, ATT_WIDTH), BF16),
        scratch_shapes=[pltpu.VMEM((seq + ATT_PAD, ATT_HEAD_DIM), BF16),
                        pltpu.VMEM((seq + ATT_PAD, ATT_HEAD_DIM), BF16)],
        compiler_params=_cparams(("parallel", "parallel")),
        name="chunk_attention",
    )(proj3, proj3, proj3, gq.reshape(1, -1), gk.reshape(1, -1), bias)


def _cmul(xr, xi, yr, yi):
    return xr * yr - xi * yi, xr * yi + xi * yr


def _s5_body(u_ref, lr2_ref, li2_ref, ldt_ref, brt_ref, bit_ref, cr_ref, ci_ref, crt_ref, cit_ref,
             yt_ref, mfull, fmat, emat, kflat, *, cpb):
    t_len, n_ch, n_st = SSM_T, SSM_GROUP, SSM_STATE
    nch = u_ref.shape[1]
    dt = jnp.exp(ldt_ref[0])
    lr2, li2 = lr2_ref[0], li2_ref[0]
    first_half = lax.broadcasted_iota(jnp.int32, (1, 2 * n_st), 1) < n_st

    mag = jnp.exp(lr2 * dt)
    ar, ai = mag * jnp.cos(li2 * dt), mag * jnp.sin(li2 * dt)
    den = lr2 * lr2 + li2 * li2
    nr, ni = ar - 1.0, ai
    zr, zi = (nr * lr2 + ni * li2) / den, (ni * lr2 - nr * li2) / den
    bbr = zr * brt_ref[0] - zi * bit_ref[0]
    bbi = zr * bit_ref[0] + zi * brt_ref[0]

    n_bits = t_len.bit_length() - 1
    sq = [(ar, ai)]
    while (1 << (len(sq) - 1)) * 2 < t_len * cpb:
        sq.append(_cmul(*sq[-1], *sq[-1]))

    def power_rows(e):
        pr = pi = None
        for k in range(n_bits):
            bit = ((e >> k) & 1) == 1
            fr, fi = jnp.where(bit, sq[k][0], 1.0), jnp.where(bit, sq[k][1], 0.0)
            pr, pi = (fr, fi) if pr is None else _cmul(pr, pi, fr, fi)
        return pr, pi

    e_row = lax.broadcasted_iota(jnp.int32, (t_len, 2 * n_st), 0)
    r0r, r0i = power_rows(e_row)
    qr, qi = power_rows(t_len - 1 - e_row)
    r1r, r1i = _cmul(r0r, r0i, ar, ai)
    p0r, p0i, p1r, p1i = r0r.T, r0i.T, r1r.T, r1i.T

    cr, ci = cr_ref[0], ci_ref[0]
    coef_r, coef_i = [], []
    for mch in range(n_ch):
        br_m, bi_m = bbr[mch:mch + 1, :n_st], bbi[mch:mch + 1, :n_st]
        coef_r.append(cr * br_m - ci * bi_m)
        coef_i.append(cr * bi_m + ci * br_m)
    coef_r = jnp.concatenate(coef_r, axis=0)
    coef_i = jnp.concatenate(coef_i, axis=0)
    kflat[...] = (jnp.dot(coef_r, p0r[:n_st], preferred_element_type=F32,
                          precision=lax.Precision.HIGHEST)
                  - jnp.dot(coef_i, p0i[:n_st], preferred_element_type=F32,
                            precision=lax.Precision.HIGHEST))

    causal = (lax.broadcasted_iota(jnp.int32, (t_len, t_len), 1)
              >= lax.broadcasted_iota(jnp.int32, (t_len, t_len), 0))

    def build(mch, carry):
        for nn in range(n_ch):
            row = kflat[pl.ds(mch * n_ch + nn, 1), :]
            tz = pltpu.roll(jnp.broadcast_to(row, (t_len, t_len)), 0, 1, stride=1, stride_axis=0)
            mfull[pl.ds(pl.multiple_of(mch * t_len, t_len), t_len), nn * t_len:(nn + 1) * t_len] = (
                jnp.where(causal, tz, 0.0).astype(BF16))
        return carry

    lax.fori_loop(0, n_ch, build, 0)

    f_r = jnp.where(first_half, bbr, bbi)
    f_i = jnp.where(first_half, -bbi, bbr)
    for mch in range(n_ch):
        fmat[mch * t_len:(mch + 1) * t_len, :] = (
            qr * f_r[mch:mch + 1, :] + qi * f_i[mch:mch + 1, :]).astype(BF16)

    crt, cit = crt_ref[0], cit_ref[0]
    top = lax.broadcasted_iota(jnp.int32, (2 * n_st, 1), 0) < n_st
    e_r = jnp.where(top, p1r, -p1i)
    e_i = jnp.where(top, -p1i, -p1r)
    for nn in range(n_ch):
        emat[:, nn * t_len:(nn + 1) * t_len] = (
            crt[:, nn:nn + 1] * e_r + cit[:, nn:nn + 1] * e_i).astype(BF16)

    xcat = jnp.concatenate([u_ref[mch].astype(BF16) for mch in range(n_ch)], axis=1)
    s_loc = jnp.dot(xcat, fmat[...], preferred_element_type=F32)

    cidx = lax.broadcasted_iota(jnp.int32, (nch, 1), 0) % cpb
    xin = jnp.where(cidx >= 1, pltpu.roll(s_loc, 1, axis=0), 0.0)
    d = 1
    while d < cpb:
        a_r, a_i = sq[n_bits + d.bit_length() - 1]
        a_is = jnp.where(first_half, -a_i, a_i)
        sh = pltpu.roll(xin, d, axis=0)
        contrib = sh * a_r + pltpu.roll(sh, n_st, axis=1) * a_is
        xin = xin + jnp.where(cidx >= d, contrib, 0.0)
        d *= 2

    y = (jnp.dot(xcat, mfull[...], preferred_element_type=F32)
         + jnp.dot(xin.astype(BF16), emat[...], preferred_element_type=F32))
    for nn in range(n_ch):
        yt_ref[nn] = y[:, nn * t_len:(nn + 1) * t_len]


def _s5_scan(u3, lam_re, lam_im, log_dt, b_re, b_im, c_re, c_im, *, cpb):
    g, p, n = SSM_GROUPS, SSM_STATE, SSM_GROUP
    nch = u3.shape[1]
    dup = lambda a, axis: jnp.concatenate([a, a], axis=axis)
    lr2 = dup(lam_re, 1).reshape(g, 1, 2 * p)
    li2 = dup(lam_im, 1).reshape(g, 1, 2 * p)
    ldt = log_dt.reshape(g, 1, 1)
    brt = dup(jnp.swapaxes(b_re, 1, 2), 2)
    bit = dup(jnp.swapaxes(b_im, 1, 2), 2)
    crt = dup(jnp.swapaxes(c_re, 1, 2), 1)
    cit = dup(jnp.swapaxes(c_im, 1, 2), 1)
    spec = lambda *s: pl.BlockSpec((1,) + s, lambda i: (i, 0, 0))
    return pl.pallas_call(
        functools.partial(_s5_body, cpb=cpb),
        grid=(g,),
        in_specs=[pl.BlockSpec((n, nch, SSM_T), lambda i: (i, 0, 0)),
                  spec(1, 2 * p), spec(1, 2 * p), spec(1, 1),
                  spec(n, 2 * p), spec(n, 2 * p), spec(n, p), spec(n, p),
                  spec(2 * p, n), spec(2 * p, n)],
        out_specs=pl.BlockSpec((n, nch, SSM_T), lambda i: (i, 0, 0)),
        out_shape=jax.ShapeDtypeStruct((SSM_WIDTH, nch, SSM_T), F32),
        scratch_shapes=[pltpu.VMEM((n * SSM_T, n * SSM_T), BF16),
                        pltpu.VMEM((n * SSM_T, 2 * p), BF16),
                        pltpu.VMEM((2 * p, n * SSM_T), BF16),
                        pltpu.VMEM((n * n, SSM_T), F32)],
        compiler_params=_cparams(("parallel",)),
        name="s5_scan",
    )(u3, lr2, li2, ldt, brt, bit, c_re, c_im, crt, cit)


def _s5_post_body(yt_ref, u_ref, dsk_ref, wgt_ref, bg_ref, o_ref):
    nc = u_ref.shape[1]
    yt = jnp.concatenate([yt_ref[:, c, :] for c in range(nc)], axis=1)
    ut = jnp.concatenate([u_ref[:, c, :] for c in range(nc)], axis=1)
    y = jax.nn.gelu(yt + dsk_ref[...] * ut)
    z = jnp.dot(wgt_ref[0], y.astype(BF16), preferred_element_type=F32) + bg_ref[...]
    o_ref[...] = (y * jax.nn.sigmoid(z)).T.astype(o_ref.dtype)


def _s5_post(yt3, u3, d_skip, w_glu_t, layer, b_glu, chunks):
    w, nch, t_len = yt3.shape
    chunks = min(chunks, nch)
    return pl.pallas_call(
        _s5_post_body,
        grid=(nch // chunks,),
        in_specs=[pl.BlockSpec((w, chunks, t_len), lambda i: (0, i, 0)),
                  pl.BlockSpec((w, chunks, t_len), lambda i: (0, i, 0)),
                  pl.BlockSpec((w, 1), lambda i: (0, 0)),
                  pl.BlockSpec((1, w, w), lambda i: (layer, 0, 0)),
                  pl.BlockSpec((w, 1), lambda i: (0, 0))],
        out_specs=pl.BlockSpec((chunks * t_len, w), lambda i: (i, 0)),
        out_shape=jax.ShapeDtypeStruct((nch * t_len, w), BF16),
        compiler_params=_cparams(("parallel",)),
        name="s5_post",
    )(yt3, u3, d_skip.reshape(w, 1), w_glu_t, b_glu.reshape(w, 1))


def _mlstm_body(xm_ref, vm_ref, om_ref, ifc_ref, ifr_ref, bif_ref, cw_ref, cb_ref, wq_ref, wk_ref,
                gh_ref, sk_ref, o_ref, xc_s, q_s, k_s, cmat, nvec, mrun, *, seq):
    t_len, dh = ML_T, ML_HEAD_DIM
    xm = xm_ref[0].astype(F32)
    rowi = lax.broadcasted_iota(jnp.int32, (seq, 1), 0)
    cw = cw_ref[...]
    acc = cw[ML_CONV - 1:ML_CONV, :] * xm + cb_ref[...]
    for j in range(1, ML_CONV):
        shifted = jnp.where(rowi >= j, pltpu.roll(xm, j, axis=0), 0.0)
        acc = acc + cw[ML_CONV - 1 - j:ML_CONV - j, :] * shifted
    xc = acc * jax.nn.sigmoid(acc)
    xc_s[...] = xc
    xcb = xc.astype(BF16)
    q_s[...] = jnp.dot(xcb, wq_ref[0, 0], preferred_element_type=F32).astype(BF16)
    k_s[...] = (jnp.dot(xcb, wk_ref[0, 0], preferred_element_type=F32) * (dh ** -0.5)).astype(BF16)

    cmat[...] = jnp.zeros_like(cmat)
    nvec[...] = jnp.zeros_like(nvec)
    mrun[...] = jnp.full_like(mrun, ML_NEG)
    b_i, b_f = bif_ref[0][:, 0:1], bif_ref[0][:, 1:2]
    iota_r = lax.broadcasted_iota(jnp.int32, (t_len, t_len), 0)
    iota_c = lax.broadcasted_iota(jnp.int32, (t_len, t_len), 1)
    tri = iota_c <= iota_r
    tri_t = iota_r <= iota_c

    def step(c, carry):
        r0 = pl.multiple_of(c * t_len, t_len)
        rows = pl.ds(r0, t_len)
        ifc = ifc_ref[0, 0, rows, :]
        ifr = ifr_ref[0, 0, :, rows]
        i_col, i_row = ifc[:, 0:1] + b_i, ifr[0:1, :] + b_i
        lf_col = jax.nn.log_sigmoid(ifc[:, 1:2] + b_f)
        lf_row = jax.nn.log_sigmoid(ifr[1:2, :] + b_f)
        bcum_col = jnp.sum(jnp.where(tri, lf_row, 0.0), axis=1, keepdims=True)
        bcum_row = jnp.sum(jnp.where(tri_t, lf_col, 0.0), axis=0, keepdims=True)
        b_last = jnp.sum(lf_row, axis=1, keepdims=True)
        m_prev = mrun[...]
        dmat = jnp.where(tri, bcum_col - bcum_row + i_row, -jnp.inf)
        inter = bcum_col + m_prev
        m_row = jnp.maximum(jnp.max(dmat, axis=1, keepdims=True), inter)
        w_intra = jnp.exp(dmat - m_row)
        w_inter = jnp.exp(inter - m_row)
        qq, kk, vv = q_s[rows, :], k_s[rows, :], vm_ref[0, rows, :]
        s = lax.dot_general(qq, kk, _NT, preferred_element_type=F32) * w_intra
        cm = cmat[...]
        num = (jnp.dot(s.astype(BF16), vv, preferred_element_type=F32)
               + w_inter * lax.dot_general(qq, cm.astype(BF16), _NT, preferred_element_type=F32))
        den = (jnp.sum(s, axis=1, keepdims=True)
               + w_inter * jnp.sum(qq.astype(F32) * nvec[...], axis=1, keepdims=True))
        h = num / jnp.maximum(jnp.abs(den), jnp.exp(-m_row))
        g_col = b_last - bcum_col + i_col
        m_new = jnp.maximum(b_last + m_prev, jnp.max(g_col, axis=0, keepdims=True))
        wg = jnp.exp(g_col - m_new)
        decay = jnp.exp(b_last + m_prev - m_new)
        vw = (vv.astype(F32) * wg).astype(BF16)
        cmat[...] = decay * cm + lax.dot_general(vw, kk, _TN, preferred_element_type=F32)
        nvec[...] = decay * nvec[...] + jnp.sum(wg * kk.astype(F32), axis=0, keepdims=True)
        mrun[...] = m_new
        hn = _rms(h, gh_ref[...]) + sk_ref[...] * xc_s[rows, :]
        o_ref[0, rows, :] = (jax.nn.sigmoid(om_ref[0, rows, :].astype(F32)) * hn).astype(o_ref.dtype)
        return carry

    lax.fori_loop(0, seq // t_len, step, 0)


def _mlstm(proj3, ifc, ifr, bif, conv_w, conv_b, wq, wk, layer, g_h, skip, col0):
    b, seq, _ = proj3.shape
    dh = ML_HEAD_DIM
    c0 = col0 // dh
    blk = (1, seq, dh)
    vec = pl.BlockSpec((1, dh), lambda i, h: (0, h))
    wspec = pl.BlockSpec((1, 1, dh, dh), lambda i, h: (layer, h, 0, 0))
    return pl.pallas_call(
        functools.partial(_mlstm_body, seq=seq),
        grid=(b, ML_HEADS),
        in_specs=[pl.BlockSpec(blk, lambda i, h: (i, 0, c0 + h)),
                  pl.BlockSpec(blk, lambda i, h: (i, 0, c0 + ML_HEADS + h)),
                  pl.BlockSpec(blk, lambda i, h: (i, 0, c0 + 2 * ML_HEADS + h)),
                  pl.BlockSpec((1, 1, seq, 2), lambda i, h: (i, h, 0, 0)),
                  pl.BlockSpec((1, 1, 2, seq), lambda i, h: (i, h, 0, 0)),
                  pl.BlockSpec((1, 1, 2), lambda i, h: (h, 0, 0)),
                  pl.BlockSpec((ML_CONV, dh), lambda i, h: (0, h)),
                  vec, wspec, wspec, vec, vec],
        out_specs=pl.BlockSpec(blk, lambda i, h: (i, 0, h)),
        out_shape=jax.ShapeDtypeStruct((b, seq, ML_WIDTH), BF16),
        scratch_shapes=[pltpu.VMEM((seq, dh), F32),
                        pltpu.VMEM((seq, dh), BF16),
                        pltpu.VMEM((seq, dh), BF16),
                        pltpu.VMEM((dh, dh), F32),
                        pltpu.VMEM((1, dh), F32),
                        pltpu.VMEM((1, 1), F32)],
        compiler_params=_cparams(("parallel", "parallel")),
        name="mlstm",
    )(proj3, proj3, proj3, ifc, ifr, bif, conv_w, conv_b.reshape(1, -1), wq, wk,
      g_h.reshape(1, -1), skip.reshape(1, -1))


def _merge_body(x_ref, ya_ref, ys_ref, ym_ref, ga_ref, gs_ref, gm_ref, bg_ref,
                wa_ref, ws_ref, wm_ref, wo_ref, o_ref):
    d = x_ref.shape[1]
    bg = bg_ref[...]

    def branch(y_ref, w_ref, g_ref, k):
        gate = jax.nn.sigmoid(g_ref[...].astype(F32) + bg[:, k * d:(k + 1) * d])
        return gate * jnp.dot(y_ref[...], w_ref[0], preferred_element_type=F32)

    merged = branch(ya_ref, wa_ref, ga_ref, 0) + branch(ys_ref, ws_ref, gs_ref, 1) \
        + branch(ym_ref, wm_ref, gm_ref, 2)
    o_ref[...] = x_ref[...] + jnp.dot(merged.astype(BF16), wo_ref[0], preferred_element_type=F32)


def _merge(x2, ya, ys, ym, gates, b_gate, w_a, w_s, w_m, w_o, layer, tm):
    m, d = x2.shape
    tm = min(tm, m)
    row = lambda w: pl.BlockSpec((tm, w), lambda i: (i, 0))
    gate = lambda k: pl.BlockSpec((tm, d), lambda i: (i, k))
    wres = lambda w: _resident((1,) + w.shape[1:], lambda i: (layer, 0, 0))
    return pl.pallas_call(
        _merge_body,
        grid=(m // tm,),
        in_specs=[row(d), row(ya.shape[1]), row(ys.shape[1]), row(ym.shape[1]),
                  gate(0), gate(1), gate(2), _resident((1, 3 * d), lambda i: (0, 0)),
                  wres(w_a), wres(w_s), wres(w_m), wres(w_o)],
        out_specs=row(d),
        out_shape=jax.ShapeDtypeStruct((m, d), F32),
        compiler_params=_cparams(("parallel",)),
        name="merge",
    )(x2, ya, ys, ym, gates, gates, gates, b_gate.reshape(1, -1), w_a, w_s, w_m, w_o)


def _xattn_body(x_ref, g_ref, wq_ref, kv_ref, gq_ref, gk_ref, wo_ref, o_ref, q_s, att_s):
    d = x_ref.shape[1]
    dh = d // MEM_HEADS
    scale = dh ** -0.5
    hn = _rms(x_ref[...], g_ref[...]).astype(BF16)
    q_s[...] = jnp.dot(hn, wq_ref[0], preferred_element_type=F32)
    for h in range(MEM_HEADS):
        cols = slice(h * dh, (h + 1) * dh)
        qn = _rms(q_s[:, cols], gq_ref[...]).astype(BF16)
        kn = _rms(kv_ref[0, :, cols].astype(F32), gk_ref[...]).astype(BF16)
        s = lax.dot_general(qn, kn, _NT, preferred_element_type=F32) * scale
        p = jnp.exp(s - jnp.max(s, axis=-1, keepdims=True))
        l = jnp.sum(p, axis=-1, keepdims=True)
        v = kv_ref[0, :, d + h * dh:d + (h + 1) * dh]
        att_s[:, cols] = (jnp.dot(p.astype(BF16), v, preferred_element_type=F32) / l).astype(BF16)
    o_ref[...] = x_ref[...] + jnp.dot(att_s[...], wo_ref[0], preferred_element_type=F32)


def _xattn(x2, g, w_q, kv3, g_q, g_k, w_o, layer, seq, tm):
    m, d = x2.shape
    tm = min(tm, seq)
    per_seq = seq // tm
    wres = lambda w: _resident((1,) + w.shape[1:], lambda i: (layer, 0, 0))
    head = pl.BlockSpec((1, d // MEM_HEADS), lambda i: (0, 0))
    return pl.pallas_call(
        _xattn_body,
        grid=(m // tm,),
        in_specs=[pl.BlockSpec((tm, d), lambda i: (i, 0)),
                  pl.BlockSpec((1, d), lambda i: (0, 0)),
                  wres(w_q),
                  pl.BlockSpec((1,) + kv3.shape[1:], lambda i: (i // per_seq, 0, 0)),
                  head, head, wres(w_o)],
        out_specs=pl.BlockSpec((tm, d), lambda i: (i, 0)),
        out_shape=jax.ShapeDtypeStruct((m, d), F32),
        scratch_shapes=[pltpu.VMEM((tm, d), F32), pltpu.VMEM((tm, d), BF16)],
        compiler_params=_cparams(("parallel",)),
        name="mem_attention",
    )(x2, g.reshape(1, d), w_q, kv3, g_q.reshape(1, -1), g_k.reshape(1, -1), w_o)


def _ffn_up_body(x_ref, g_ref, wg_ref, wu_ref, a_ref, hn_ref, *, n_fill):
    rows = hn_ref.shape[0]
    s = pl.program_id(1)

    @pl.when(s < n_fill)
    def _():
        _fill_norm(x_ref, g_ref, hn_ref, s)

    @pl.when(s >= n_fill)
    def _():
        wg, wu = wg_ref[0].astype(BF16), wu_ref[0].astype(BF16)
        for r in range(0, rows, DOT_ROWS):
            hn = hn_ref[r:r + DOT_ROWS, :]
            gate = jnp.dot(hn, wg, preferred_element_type=F32)
            up = jnp.dot(hn, wu, preferred_element_type=F32)
            a_ref[r:r + DOT_ROWS, :] = (gate * jax.nn.sigmoid(gate) * up).astype(a_ref.dtype)


def _ffn_up(x2, g, w_gu, layer):
    m, d = x2.shape
    f = w_gu.shape[2] // 2
    rows = min(ROW_GROUP, m)
    n_fill = rows // NORM_ROWS
    n_tiles = f // COL_TILE
    tile = lambda s: jnp.clip(s - n_fill, 0, n_tiles - 1)
    return pl.pallas_call(
        functools.partial(_ffn_up_body, n_fill=n_fill),
        grid=(m // rows, n_fill + n_tiles),
        in_specs=[pl.BlockSpec((NORM_ROWS, d), lambda h, s: (h * n_fill + jnp.minimum(s, n_fill - 1), 0)),
                  pl.BlockSpec((1, d), lambda h, s: (0, 0)),
                  pl.BlockSpec((1, d, COL_TILE), lambda h, s: (layer, 0, tile(s))),
                  pl.BlockSpec((1, d, COL_TILE), lambda h, s: (layer, 0, n_tiles + tile(s)))],
        out_specs=pl.BlockSpec((rows, COL_TILE), lambda h, s: (h, tile(s))),
        out_shape=jax.ShapeDtypeStruct((m, f), BF16),
        scratch_shapes=[pltpu.VMEM((rows, d), BF16)],
        compiler_params=_cparams(("parallel", "arbitrary")),
        name="swiglu_up",
    )(x2, g.reshape(1, d), w_gu, w_gu)


def _ffn_down_body(x_ref, a_ref, w_ref, o_ref, wb_ref, *, n_load, k_tile):
    s = pl.program_id(0)

    @pl.when(s < n_load)
    def _():
        r0 = pl.multiple_of(s * k_tile, k_tile)
        wb_ref[pl.ds(r0, k_tile), :] = w_ref[0].astype(BF16)

    @pl.when(s >= n_load)
    def _():
        o_ref[...] = x_ref[...] + jnp.dot(a_ref[...], wb_ref[...], preferred_element_type=F32)


def _ffn_down(x2, act, w_down, layer, tm, k_tile):
    m, d = x2.shape
    f = act.shape[1]
    tm = min(tm, m)
    n_load = f // k_tile
    blk = lambda s: jnp.maximum(s - n_load, 0)
    return pl.pallas_call(
        functools.partial(_ffn_down_body, n_load=n_load, k_tile=k_tile),
        grid=(n_load + m // tm,),
        in_specs=[pl.BlockSpec((tm, d), lambda s: (blk(s), 0)),
                  pl.BlockSpec((tm, f), lambda s: (blk(s), 0)),
                  pl.BlockSpec((1, k_tile, d), lambda s: (layer, jnp.minimum(s, n_load - 1), 0))],
        out_specs=pl.BlockSpec((tm, d), lambda s: (blk(s), 0)),
        out_shape=jax.ShapeDtypeStruct((m, d), F32),
        scratch_shapes=[pltpu.VMEM((f, d), BF16)],
        compiler_params=_cparams(("arbitrary",)),
        name="swiglu_down",
    )(x2, act, w_down)


def kernel(x, mem, g_mem, norm_mix, w_in, b_gate, g_qa, g_ka, rel_bias, lam_re, lam_im, log_dt, b_re, b_im, c_re, c_im, d_skip, w_glu, b_glu, conv_w, conv_b, wq_m, wk_m, b_i, b_f, g_hm, skip_m, w_br_a, w_br_s, w_br_m, w_out, norm_x, w_xq, w_xkv, g_xq, g_xk, w_xo, norm_ffn, w_gu, w_down):
    b, seq, d = x.shape
    depth = w_in.shape[0]
    tokens = b * seq
    assert seq % ATT_QB == 0 and seq % ML_T == 0 and seq % SSM_T == 0
    assert tokens % min(ROW_GROUP, tokens) == 0

    w_glu_t = jnp.swapaxes(w_glu, 1, 2).astype(BF16)
    wq_b, wk_b = wq_m.astype(BF16), wk_m.astype(BF16)
    wa_b, ws_b, wm_b, wo_b = (w.astype(BF16) for w in (w_br_a, w_br_s, w_br_m, w_out))
    wxq_b, wxo_b = w_xq.astype(BF16), w_xo.astype(BF16)
    w_in_t = jnp.swapaxes(w_in, 1, 2)

    x2 = x.reshape(tokens, d)
    mem2 = mem.reshape(b * mem.shape[1], d)
    for i in range(depth):
        proj, u3, gates, ift = _in_proj(x2, norm_mix[i], w_in_t, i)
        proj3 = proj.reshape(b, seq, -1)

        ya = _chunk_attention(proj3, g_qa[i], g_ka[i], _att_bias(rel_bias[i]))

        yt3 = _s5_scan(u3, lam_re[i], lam_im[i], log_dt[i], b_re[i], b_im[i], c_re[i], c_im[i],
                       cpb=seq // SSM_T)
        ys = _s5_post(yt3, u3, d_skip[i], w_glu_t, i, b_glu[i], chunks=8)

        ifr = ift[:2 * ML_HEADS].reshape(2, ML_HEADS, b, seq).transpose(2, 1, 0, 3)
        ifc = jnp.swapaxes(ifr, 2, 3)
        bif = jnp.stack([b_i[i], b_f[i]], axis=-1).reshape(ML_HEADS, 1, 2)
        ym = _mlstm(proj3, ifc, ifr, bif, conv_w[i], conv_b[i], wq_b, wk_b, i, g_hm[i], skip_m[i],
                    3 * ATT_WIDTH)

        x2 = _merge(x2, ya.reshape(tokens, -1), ys, ym.reshape(tokens, -1), gates, b_gate[i],
                    wa_b, ws_b, wm_b, wo_b, i, tm=256)

        kv = _norm_proj(mem2, g_mem, w_xkv, i, tm=1024, tn=512)
        x2 = _xattn(x2, norm_x[i], wxq_b, kv.reshape(b, mem.shape[1], 2 * d), g_xq[i], g_xk[i],
                    wxo_b, i, seq, tm=512)

        act = _ffn_up(x2, norm_ffn[i], w_gu, i)
        x2 = _ffn_down(x2, act, w_down, i, tm=256, k_tile=256)
    return x2.reshape(b, seq, d)
```

```python
import functools

import jax
import jax.numpy as jnp
from jax import lax
from jax.experimental import pallas as pl
from jax.experimental.pallas import tpu as pltpu

F32 = jnp.float32
BF16 = jnp.bfloat16

EPS = 1e-6
LANES = 128
BF16_ROWS = 16
ATT_CHUNK = 64
ATT_HEADS = 8
ATT_HEAD_DIM = 128
ATT_WIDTH = ATT_HEADS * ATT_HEAD_DIM
ATT_LEFT = 8
REL_CLIP = 256
ATT_QB = 256
ATT_PAD = ATT_LEFT * ATT_CHUNK
ATT_KW = ATT_QB + ATT_PAD
ATT_EXT = 1024
MASK_NEG = -1e30
SSM_GROUP = 16
SSM_GROUPS = 48
SSM_WIDTH = SSM_GROUP * SSM_GROUPS
SSM_STATE = 64
SSM_T = LANES
SSM_SUB = 16
SSM_GROUPS_PER_STEP = 8
ML_HEADS = 4
ML_HEAD_DIM = 256
ML_WIDTH = ML_HEADS * ML_HEAD_DIM
ML_CONV = 4
ML_T = 256
ML_NEG = -1e30
MEM_HEADS = 4
ROW_GROUP = 4096
NORM_ROWS = 256
COL_TILE = 256
DOT_ROWS = 1024

VMEM_LIMIT = 56 * 1024 * 1024

_NT = (((1,), (1,)), ((), ()))
_TN = (((0,), (0,)), ((), ()))


def _cparams(sem):
    return pltpu.CompilerParams(dimension_semantics=sem, vmem_limit_bytes=VMEM_LIMIT)


def _rms(x, g):
    return x * lax.rsqrt(jnp.mean(x * x, axis=-1, keepdims=True) + EPS) * g


def _resident(shape, index_map):
    return pl.BlockSpec(shape, index_map, pipeline_mode=pl.Buffered(1))


def _fill_norm(x_ref, g_ref, hn_ref, step):
    r0 = pl.multiple_of(step * NORM_ROWS, NORM_ROWS)
    hn_ref[pl.ds(r0, NORM_ROWS), :] = _rms(x_ref[...], g_ref[...]).astype(BF16)


def _in_proj_body(x_ref, g_ref, wm_ref, wg_ref, we_ref, p_ref, u_ref, gt_ref, if_ref, hn_ref,
                  *, n_fill, n_att, n_ssm, n_ml, if_cols):
    rows = hn_ref.shape[0]
    s = pl.program_id(1)
    t = s - n_fill
    t_gate = t - (n_att + n_ssm + n_ml)

    @pl.when(s < n_fill)
    def _():
        _fill_norm(x_ref, g_ref, hn_ref, s)

    def project(wt, o_ref):
        for r in range(0, rows, DOT_ROWS):
            o_ref[r:r + DOT_ROWS, :] = lax.dot_general(
                hn_ref[r:r + DOT_ROWS, :], wt, _NT, preferred_element_type=F32).astype(o_ref.dtype)

    in_ssm = (t >= n_att) & (t < n_att + n_ssm)

    @pl.when((t >= 0) & (t_gate < 0) & jnp.logical_not(in_ssm))
    def _():
        project(wm_ref[0].astype(BF16), p_ref)

    @pl.when(in_ssm)
    def _():
        wt = wm_ref[0].astype(BF16)
        for r in range(0, rows, DOT_ROWS):
            ut = lax.dot_general(wt, hn_ref[r:r + DOT_ROWS, :], _NT, preferred_element_type=F32)
            for c in range(DOT_ROWS // SSM_T):
                u_ref[:, r // SSM_T + c, :] = ut[:, c * SSM_T:(c + 1) * SSM_T]

    @pl.when(t_gate == 0)
    def _():
        wif = wg_ref[0][:BF16_ROWS].astype(BF16)
        for r in range(0, rows, DOT_ROWS):
            if_ref[:, r:r + DOT_ROWS] = lax.dot_general(
                wif, hn_ref[r:r + DOT_ROWS, :], _NT, preferred_element_type=F32)

    @pl.when(t_gate >= 0)
    def _():
        wt = jnp.concatenate([wg_ref[0][if_cols:], we_ref[0]], axis=0)
        project(wt.astype(BF16), gt_ref)


def _in_proj(x2, g, w_in_t, layer):
    m, d = x2.shape
    rows = min(ROW_GROUP, m)
    n_fill = rows // NORM_ROWS
    n_att = 3 * ATT_WIDTH // COL_TILE
    n_ssm = SSM_WIDTH // COL_TILE
    n_ml = 3 * ML_WIDTH // COL_TILE
    n_main = n_att + n_ssm + n_ml
    if_cols = 2 * ML_HEADS
    gate_w = w_in_t.shape[1] - n_main * COL_TILE - if_cols
    n_gate = gate_w // COL_TILE
    assert gate_w % COL_TILE == 0 and if_cols <= BF16_ROWS and COL_TILE % if_cols == 0
    ext0 = (n_main + 1) * COL_TILE // if_cols
    ext_step = COL_TILE // if_cols

    def t_of(s):
        return s - n_fill

    def p_col(s):
        t = t_of(s)
        return jnp.where(t < n_att, jnp.clip(t, 0, n_att - 1), jnp.clip(t - n_ssm, n_att, n_att + n_ml - 1))

    def g_idx(s):
        return jnp.clip(t_of(s) - n_main, 0, n_gate - 1)

    body = functools.partial(_in_proj_body, n_fill=n_fill, n_att=n_att, n_ssm=n_ssm, n_ml=n_ml,
                             if_cols=if_cols)
    return pl.pallas_call(
        body,
        grid=(m // rows, n_fill + n_main + n_gate),
        in_specs=[pl.BlockSpec((NORM_ROWS, d), lambda h, s: (h * n_fill + jnp.minimum(s, n_fill - 1), 0)),
                  pl.BlockSpec((1, d), lambda h, s: (0, 0)),
                  pl.BlockSpec((1, COL_TILE, d), lambda h, s: (layer, jnp.clip(t_of(s), 0, n_main - 1), 0)),
                  pl.BlockSpec((1, COL_TILE, d), lambda h, s: (layer, n_main + g_idx(s), 0)),
                  pl.BlockSpec((1, if_cols, d), lambda h, s: (layer, ext0 + ext_step * g_idx(s), 0))],
        out_specs=[pl.BlockSpec((rows, COL_TILE), lambda h, s: (h, p_col(s))),
                   pl.BlockSpec((COL_TILE, rows // SSM_T, SSM_T),
                                lambda h, s: (jnp.clip(t_of(s) - n_att, 0, n_ssm - 1), h, 0)),
                   pl.BlockSpec((rows, COL_TILE), lambda h, s: (h, g_idx(s))),
                   pl.BlockSpec((BF16_ROWS, rows), lambda h, s: (0, h))],
        out_shape=[jax.ShapeDtypeStruct((m, (n_att + n_ml) * COL_TILE), BF16),
                   jax.ShapeDtypeStruct((SSM_WIDTH, m // SSM_T, SSM_T), F32),
                   jax.ShapeDtypeStruct((m, gate_w), BF16),
                   jax.ShapeDtypeStruct((BF16_ROWS, m), F32)],
        scratch_shapes=[pltpu.VMEM((rows, d), BF16)],
        compiler_params=_cparams(("parallel", "arbitrary")),
        name="in_proj",
    )(x2, g.reshape(1, d), w_in_t, w_in_t, w_in_t)


def _norm_proj_body(x_ref, g_ref, w_ref, o_ref, hn_ref):
    @pl.when(pl.program_id(1) == 0)
    def _():
        hn_ref[...] = _rms(x_ref[...], g_ref[...]).astype(BF16)

    o_ref[...] = jnp.dot(hn_ref[...], w_ref[0].astype(BF16),
                         preferred_element_type=F32).astype(o_ref.dtype)


def _norm_proj(x2, g, w, layer, tm, tn):
    m, d = x2.shape
    n = w.shape[2]
    tm = min(tm, m)
    return pl.pallas_call(
        _norm_proj_body,
        grid=(m // tm, n // tn),
        in_specs=[pl.BlockSpec((tm, d), lambda i, j: (i, 0)),
                  pl.BlockSpec((1, d), lambda i, j: (0, 0)),
                  pl.BlockSpec((1, d, tn), lambda i, j: (layer, 0, j))],
        out_specs=pl.BlockSpec((tm, tn), lambda i, j: (i, j)),
        out_shape=jax.ShapeDtypeStruct((m, n), BF16),
        scratch_shapes=[pltpu.VMEM((tm, d), BF16)],
        compiler_params=_cparams(("parallel", "arbitrary")),
        name="norm_proj",
    )(x2, g.reshape(1, d), w)


def _att_bias_body(ext_ref, o_ref):
    ext = ext_ref[0]
    base = pltpu.roll(ext, ATT_EXT - (ATT_QB - 1), axis=1)
    slab = jnp.broadcast_to(base, (ATT_QB, ATT_EXT))
    tab = pltpu.roll(slab, 0, 1, stride=1, stride_axis=0)[:, :ATT_KW]
    qc = lax.broadcasted_iota(jnp.int32, (ATT_QB, ATT_KW), 0) // ATT_CHUNK
    kc = lax.broadcasted_iota(jnp.int32, (ATT_QB, ATT_KW), 1) // ATT_CHUNK
    o_ref[0] = jnp.where((kc >= qc) & (kc <= qc + ATT_LEFT), tab, MASK_NEG)


def _att_bias(rel_bias):
    h = rel_bias.shape[0]
    ext = jnp.concatenate(
        [jnp.broadcast_to(rel_bias[:, :1], (h, ATT_EXT - 2 * REL_CLIP - 1)), rel_bias], axis=1)
    return pl.pallas_call(
        _att_bias_body,
        grid=(h,),
        in_specs=[pl.BlockSpec((1, 1, ATT_EXT), lambda i: (i, 0, 0))],
        out_specs=pl.BlockSpec((1, ATT_QB, ATT_KW), lambda i: (i, 0, 0)),
        out_shape=jax.ShapeDtypeStruct((h, ATT_QB, ATT_KW), F32),
        compiler_params=_cparams(("parallel",)),
        name="att_bias",
    )(ext.reshape(h, 1, ATT_EXT))


def _attn_body(q_ref, k_ref, v_ref, gq_ref, gk_ref, bias_ref, o_ref, kpad, vpad, *, seq):
    kpad[0:ATT_PAD, :] = jnp.zeros((ATT_PAD, ATT_HEAD_DIM), BF16)
    vpad[0:ATT_PAD, :] = jnp.zeros((ATT_PAD, ATT_HEAD_DIM), BF16)
    kpad[ATT_PAD:, :] = _rms(k_ref[0].astype(F32), gk_ref[...]).astype(BF16)
    vpad[ATT_PAD:, :] = v_ref[0]
    bias = bias_ref[0]
    scale = ATT_HEAD_DIM ** -0.5
    for qb in range(seq // ATT_QB):
        r0 = qb * ATT_QB
        qn = _rms(q_ref[0, r0:r0 + ATT_QB, :].astype(F32), gq_ref[...]).astype(BF16)
        kw = kpad[r0:r0 + ATT_KW, :]
        s = lax.dot_general(qn, kw, _NT, preferred_element_type=F32) * scale + bias
        if r0 < ATT_PAD:
            col = lax.broadcasted_iota(jnp.int32, (ATT_QB, ATT_KW), 1)
            s = jnp.where(col + r0 >= ATT_PAD, s, MASK_NEG)
        m = jnp.max(s, axis=-1, keepdims=True)
        p = jnp.exp(s - m)
        l = jnp.sum(p, axis=-1, keepdims=True)
        o = jnp.dot(p.astype(BF16), vpad[r0:r0 + ATT_KW, :], preferred_element_type=F32)
        o_ref[0, r0:r0 + ATT_QB, :] = (o / l).astype(o_ref.dtype)


def _chunk_attention(proj3, gq, gk, bias):
    b, seq, _ = proj3.shape
    blk = (1, seq, ATT_HEAD_DIM)
    return pl.pallas_call(
        functools.partial(_attn_body, seq=seq),
        grid=(b, ATT_HEADS),
        in_specs=[pl.BlockSpec(blk, lambda i, h: (i, 0, h)),
                  pl.BlockSpec(blk, lambda i, h: (i, 0, ATT_HEADS + h)),
                  pl.BlockSpec(blk, lambda i, h: (i, 0, 2 * ATT_HEADS + h)),
                  pl.BlockSpec((1, ATT_HEAD_DIM), lambda i, h: (0, 0)),
                  pl.BlockSpec((1, ATT_HEAD_DIM), lambda i, h: (0, 0)),
                  pl.BlockSpec((1, ATT_QB, ATT_KW), lambda i, h: (h, 0, 0))],
        out_specs=pl.BlockSpec(blk, lambda i, h: (i, 0, h)),
        out_shape=jax.ShapeDtypeStruct((b, seq, ATT_WIDTH), BF16),
        scratch_shapes=[pltpu.VMEM((seq + ATT_PAD, ATT_HEAD_DIM), BF16),
                        pltpu.VMEM((seq + ATT_PAD, ATT_HEAD_DIM), BF16)],
        compiler_params=_cparams(("parallel", "parallel")),
        name="chunk_attention",
    )(proj3, proj3, proj3, gq.reshape(1, -1), gk.reshape(1, -1), bias)


def _cmul(xr, xi, yr, yi):
    return xr * yr - xi * yi, xr * yi + xi * yr


def _s5_perm_tables(pmat, pmat_t):
    width = SSM_GROUP * SSM_T
    blk = SSM_GROUP * SSM_SUB
    sub_bits = SSM_SUB.bit_length() - 1
    t_bits = SSM_T.bit_length() - 1
    blk_bits = blk.bit_length() - 1
    for jb in range(width // blk):
        row = lax.broadcasted_iota(jnp.int32, (width, blk), 0)
        col = lax.broadcasted_iota(jnp.int32, (width, blk), 1) + jb * blk
        s0 = col & (SSM_SUB - 1)
        src = (((col & (blk - 1)) >> sub_bits) << t_bits) + ((col >> blk_bits) << sub_bits) + s0
        pmat[:, jb * blk:(jb + 1) * blk] = jnp.where(row == src, 1.0, 0.0).astype(BF16)
        src_t = (((col & (SSM_T - 1)) >> sub_bits) << blk_bits) + ((col >> t_bits) << sub_bits) + s0
        pmat_t[:, jb * blk:(jb + 1) * blk] = jnp.where(row == src_t, 1.0, 0.0).astype(BF16)


def _s5_group(gi, lr2_ref, li2_ref, ldt_ref, brt_ref, bit_ref, crx_ref, cix_ref,
              xp, yp, *, cpb, nch):
    t_len, sub, n_ch, n_st = SSM_T, SSM_SUB, SSM_GROUP, SSM_STATE
    n_sub = t_len // sub
    dt = jnp.exp(ldt_ref[gi])
    lr2, li2 = lr2_ref[gi], li2_ref[gi]
    first_half = lax.broadcasted_iota(jnp.int32, (1, 2 * n_st), 1) < n_st

    mag = jnp.exp(lr2 * dt)
    ar, ai = mag * jnp.cos(li2 * dt), mag * jnp.sin(li2 * dt)
    den = lr2 * lr2 + li2 * li2
    nr, ni = ar - 1.0, ai
    zr, zi = (nr * lr2 + ni * li2) / den, (ni * lr2 - nr * li2) / den
    bbr = zr * brt_ref[gi] - zi * bit_ref[gi]
    bbi = zr * bit_ref[gi] + zi * brt_ref[gi]

    sub_bits = sub.bit_length() - 1
    t_bits = t_len.bit_length() - 1
    sq = [(ar, ai)]
    while (1 << (len(sq) - 1)) * 2 < t_len * cpb:
        sq.append(_cmul(*sq[-1], *sq[-1]))

    def power(e, base):
        pr = pi = None
        for k in range(sub_bits):
            bit = ((e >> k) & 1) == 1
            fr, fi = jnp.where(bit, base[k][0], 1.0), jnp.where(bit, base[k][1], 0.0)
            pr, pi = (fr, fi) if pr is None else _cmul(pr, pi, fr, fi)
        return pr, pi

    def rotate(x, a):
        return x * a[0] + pltpu.roll(x, n_st, axis=1) * jnp.where(first_half, -a[1], a[1])

    e_row = lax.broadcasted_iota(jnp.int32, (sub, 2 * n_st), 0)
    qr, qi = power(sub - 1 - e_row, sq)
    f_r = jnp.where(first_half, bbr, bbi)
    f_i = jnp.where(first_half, -bbi, bbr)
    fmat = jnp.concatenate([(qr * f_r[mch:mch + 1, :] + qi * f_i[mch:mch + 1, :]).astype(BF16)
                            for mch in range(n_ch)], axis=0)

    sq_rows = jnp.concatenate([v for k in range(sub_bits) for v in sq[k]], axis=0)
    sq_cols = sq_rows.T
    base_c = [(sq_cols[:, 2 * k:2 * k + 1], sq_cols[:, 2 * k + 1:2 * k + 2]) for k in range(sub_bits)]
    tau = lax.broadcasted_iota(jnp.int32, (2 * n_st, n_ch * sub), 1) & (sub - 1)
    p0r, p0i = power(tau, base_c)
    p1r, p1i = _cmul(p0r, p0i, *base_c[0])
    top = lax.broadcasted_iota(jnp.int32, (2 * n_st, 1), 0) < n_st
    crx, cix = crx_ref[gi], cix_ref[gi]

    def c_times(pr, pi):
        return crx * jnp.where(top, pr, -pi) + cix * jnp.where(top, -pi, -pr)

    emat = c_times(p1r, p1i).astype(BF16)

    kflat = jnp.dot(f_r, c_times(p0r, p0i), preferred_element_type=F32, precision=lax.Precision.HIGHEST)
    lane_t = lax.broadcasted_iota(jnp.int32, (sub, n_ch * sub), 1) & (sub - 1)
    causal = lane_t >= lax.broadcasted_iota(jnp.int32, (sub, n_ch * sub), 0)
    bdiag = []
    for mch in range(n_ch):
        tz = pltpu.roll(jnp.broadcast_to(kflat[mch:mch + 1, :], (sub, n_ch * sub)), 0, 1,
                        stride=1, stride_axis=0)
        bdiag.append(jnp.where(causal, tz, 0.0).astype(BF16))
    bdiag = jnp.concatenate(bdiag, axis=0)

    rows = pl.ds(pl.multiple_of(gi * nch, nch), nch)
    xg = xp[rows, :]
    blk = n_ch * sub
    x2 = jnp.concatenate([xg[:, j * blk:(j + 1) * blk] for j in range(n_sub)], axis=0)
    s2 = jnp.dot(x2, fmat, preferred_element_type=F32)
    a_sub = sq[sub_bits]

    def run(state):
        entering = []
        for j in range(n_sub):
            entering.append(state)
            state = rotate(state, a_sub) + s2[j * nch:(j + 1) * nch]
        return entering, state

    _, s_loc = run(jnp.zeros((nch, 2 * n_st), F32))

    cidx = lax.broadcasted_iota(jnp.int32, (nch, 1), 0) % cpb
    xin = jnp.where(cidx >= 1, pltpu.roll(s_loc, 1, axis=0), 0.0)
    d = 1
    while d < cpb:
        sh = pltpu.roll(xin, d, axis=0)
        xin = xin + jnp.where(cidx >= d, rotate(sh, sq[t_bits + d.bit_length() - 1]), 0.0)
        d *= 2

    entering, _ = run(xin)
    xin2 = jnp.concatenate(entering, axis=0).astype(BF16)
    y2 = (jnp.dot(x2, bdiag, preferred_element_type=F32)
          + jnp.dot(xin2, emat, preferred_element_type=F32))
    yp[rows, :] = jnp.concatenate([y2[j * nch:(j + 1) * nch] for j in range(n_sub)], axis=1).astype(BF16)


def _s5_body(u_ref, lr2_ref, li2_ref, ldt_ref, brt_ref, bit_ref, crx_ref, cix_ref, yt_ref,
             pmat, pmat_t, xp, yp, *, cpb):
    n_ch, t_len = SSM_GROUP, SSM_T
    nch = u_ref.shape[1]
    gp = u_ref.shape[0] // n_ch

    @pl.when(pl.program_id(0) == 0)
    def _():
        _s5_perm_tables(pmat, pmat_t)

    xall = jnp.concatenate(
        [jnp.concatenate([u_ref[g * n_ch + mch].astype(BF16) for mch in range(n_ch)], axis=1)
         for g in range(gp)], axis=0)
    xp[...] = jnp.dot(xall, pmat[...], preferred_element_type=F32).astype(BF16)

    def group(gi, carry):
        _s5_group(gi, lr2_ref, li2_ref, ldt_ref, brt_ref, bit_ref, crx_ref, cix_ref,
                  xp, yp, cpb=cpb, nch=nch)
        return carry

    lax.fori_loop(0, gp, group, 0, unroll=4)

    yall = jnp.dot(yp[...], pmat_t[...], preferred_element_type=F32)
    for g in range(gp):
        for nn in range(n_ch):
            yt_ref[g * n_ch + nn] = yall[g * nch:(g + 1) * nch, nn * t_len:(nn + 1) * t_len]


def _s5_scan(u3, lam_re, lam_im, log_dt, b_re, b_im, c_re, c_im, *, cpb):
    g, p, n, gp = SSM_GROUPS, SSM_STATE, SSM_GROUP, SSM_GROUPS_PER_STEP
    nch = u3.shape[1]
    width = n * SSM_T
    blk = n * SSM_SUB
    dup = lambda a, axis: jnp.concatenate([a, a], axis=axis)
    lr2 = dup(lam_re, 1).reshape(g, 1, 2 * p)
    li2 = dup(lam_im, 1).reshape(g, 1, 2 * p)
    ldt = log_dt.reshape(g, 1, 1)
    brt = dup(jnp.swapaxes(b_re, 1, 2), 2)
    bit = dup(jnp.swapaxes(b_im, 1, 2), 2)
    crx = jnp.repeat(dup(jnp.swapaxes(c_re, 1, 2), 1), SSM_SUB, axis=2)
    cix = jnp.repeat(dup(jnp.swapaxes(c_im, 1, 2), 1), SSM_SUB, axis=2)
    spec = lambda *s: pl.BlockSpec((gp,) + s, lambda i: (i, 0, 0))
    return pl.pallas_call(
        functools.partial(_s5_body, cpb=cpb),
        grid=(g // gp,),
        in_specs=[pl.BlockSpec((gp * n, nch, SSM_T), lambda i: (i, 0, 0)),
                  spec(1, 2 * p), spec(1, 2 * p), spec(1, 1),
                  spec(n, 2 * p), spec(n, 2 * p), spec(2 * p, blk), spec(2 * p, blk)],
        out_specs=pl.BlockSpec((gp * n, nch, SSM_T), lambda i: (i, 0, 0)),
        out_shape=jax.ShapeDtypeStruct((SSM_WIDTH, nch, SSM_T), F32),
        scratch_shapes=[pltpu.VMEM((width, width), BF16),
                        pltpu.VMEM((width, width), BF16),
                        pltpu.VMEM((gp * nch, width), BF16),
                        pltpu.VMEM((gp * nch, width), BF16)],
        compiler_params=_cparams(("arbitrary",)),
        name="s5_scan",
    )(u3, lr2, li2, ldt, brt, bit, crx, cix)


def _s5_post_body(yt_ref, u_ref, dsk_ref, wgt_ref, bg_ref, o_ref):
    nc = u_ref.shape[1]
    yt = jnp.concatenate([yt_ref[:, c, :] for c in range(nc)], axis=1)
    ut = jnp.concatenate([u_ref[:, c, :] for c in range(nc)], axis=1)
    y = jax.nn.gelu(yt + dsk_ref[...] * ut)
    z = jnp.dot(wgt_ref[0], y.astype(BF16), preferred_element_type=F32) + bg_ref[...]
    o_ref[...] = (y * jax.nn.sigmoid(z)).T.astype(o_ref.dtype)


def _s5_post(yt3, u3, d_skip, w_glu_t, layer, b_glu, chunks):
    w, nch, t_len = yt3.shape
    chunks = min(chunks, nch)
    return pl.pallas_call(
        _s5_post_body,
        grid=(nch // chunks,),
        in_specs=[pl.BlockSpec((w, chunks, t_len), lambda i: (0, i, 0)),
                  pl.BlockSpec((w, chunks, t_len), lambda i: (0, i, 0)),
                  pl.BlockSpec((w, 1), lambda i: (0, 0)),
                  pl.BlockSpec((1, w, w), lambda i: (layer, 0, 0)),
                  pl.BlockSpec((w, 1), lambda i: (0, 0))],
        out_specs=pl.BlockSpec((chunks * t_len, w), lambda i: (i, 0)),
        out_shape=jax.ShapeDtypeStruct((nch * t_len, w), BF16),
        compiler_params=_cparams(("parallel",)),
        name="s5_post",
    )(yt3, u3, d_skip.reshape(w, 1), w_glu_t, b_glu.reshape(w, 1))


def _mlstm_body(xm_ref, vm_ref, om_ref, ifc_ref, ifr_ref, bif_ref, cw_ref, cb_ref, wq_ref, wk_ref,
                gh_ref, sk_ref, o_ref, xc_s, q_s, k_s, cmat, nvec, mrun, *, seq):
    t_len, dh = ML_T, ML_HEAD_DIM
    xm = xm_ref[0].astype(F32)
    rowi = lax.broadcasted_iota(jnp.int32, (seq, 1), 0)
    cw = cw_ref[...]
    acc = cw[ML_CONV - 1:ML_CONV, :] * xm + cb_ref[...]
    for j in range(1, ML_CONV):
        shifted = jnp.where(rowi >= j, pltpu.roll(xm, j, axis=0), 0.0)
        acc = acc + cw[ML_CONV - 1 - j:ML_CONV - j, :] * shifted
    xc = acc * jax.nn.sigmoid(acc)
    xc_s[...] = xc
    xcb = xc.astype(BF16)
    q_s[...] = jnp.dot(xcb, wq_ref[0, 0], preferred_element_type=F32).astype(BF16)
    k_s[...] = (jnp.dot(xcb, wk_ref[0, 0], preferred_element_type=F32) * (dh ** -0.5)).astype(BF16)

    cmat[...] = jnp.zeros_like(cmat)
    nvec[...] = jnp.zeros_like(nvec)
    mrun[...] = jnp.full_like(mrun, ML_NEG)
    b_i, b_f = bif_ref[0][:, 0:1], bif_ref[0][:, 1:2]
    iota_r = lax.broadcasted_iota(jnp.int32, (t_len, t_len), 0)
    iota_c = lax.broadcasted_iota(jnp.int32, (t_len, t_len), 1)
    tri = iota_c <= iota_r
    tri_t = iota_r <= iota_c

    def step(c, carry):
        r0 = pl.multiple_of(c * t_len, t_len)
        rows = pl.ds(r0, t_len)
        ifc = ifc_ref[0, 0, rows, :]
        ifr = ifr_ref[0, 0, :, rows]
        i_col, i_row = ifc[:, 0:1] + b_i, ifr[0:1, :] + b_i
        lf_col = jax.nn.log_sigmoid(ifc[:, 1:2] + b_f)
        lf_row = jax.nn.log_sigmoid(ifr[1:2, :] + b_f)
        bcum_col = jnp.sum(jnp.where(tri, lf_row, 0.0), axis=1, keepdims=True)
        bcum_row = jnp.sum(jnp.where(tri_t, lf_col, 0.0), axis=0, keepdims=True)
        b_last = jnp.sum(lf_row, axis=1, keepdims=True)
        m_prev = mrun[...]
        dmat = jnp.where(tri, bcum_col - bcum_row + i_row, -jnp.inf)
        inter = bcum_col + m_prev
        m_row = jnp.maximum(jnp.max(dmat, axis=1, keepdims=True), inter)
        w_intra = jnp.exp(dmat - m_row)
        w_inter = jnp.exp(inter - m_row)
        qq, kk, vv = q_s[rows, :], k_s[rows, :], vm_ref[0, rows, :]
        s = lax.dot_general(qq, kk, _NT, preferred_element_type=F32) * w_intra
        cm = cmat[...]
        num = (jnp.dot(s.astype(BF16), vv, preferred_element_type=F32)
               + w_inter * lax.dot_general(qq, cm.astype(BF16), _NT, preferred_element_type=F32))
        den = (jnp.sum(s, axis=1, keepdims=True)
               + w_inter * jnp.sum(qq.astype(F32) * nvec[...], axis=1, keepdims=True))
        h = num / jnp.maximum(jnp.abs(den), jnp.exp(-m_row))
        g_col = b_last - bcum_col + i_col
        m_new = jnp.maximum(b_last + m_prev, jnp.max(g_col, axis=0, keepdims=True))
        wg = jnp.exp(g_col - m_new)
        decay = jnp.exp(b_last + m_prev - m_new)
        vw = (vv.astype(F32) * wg).astype(BF16)
        cmat[...] = decay * cm + lax.dot_general(vw, kk, _TN, preferred_element_type=F32)
        nvec[...] = decay * nvec[...] + jnp.sum(wg * kk.astype(F32), axis=0, keepdims=True)
        mrun[...] = m_new
        hn = _rms(h, gh_ref[...]) + sk_ref[...] * xc_s[rows, :]
        o_ref[0, rows, :] = (jax.nn.sigmoid(om_ref[0, rows, :].astype(F32)) * hn).astype(o_ref.dtype)
        return carry

    lax.fori_loop(0, seq // t_len, step, 0)


def _mlstm(proj3, ifc, ifr, bif, conv_w, conv_b, wq, wk, layer, g_h, skip, col0):
    b, seq, _ = proj3.shape
    dh = ML_HEAD_DIM
    c0 = col0 // dh
    blk = (1, seq, dh)
    vec = pl.BlockSpec((1, dh), lambda i, h: (0, h))
    wspec = pl.BlockSpec((1, 1, dh, dh), lambda i, h: (layer, h, 0, 0))
    return pl.pallas_call(
        functools.partial(_mlstm_body, seq=seq),
        grid=(b, ML_HEADS),
        in_specs=[pl.BlockSpec(blk, lambda i, h: (i, 0, c0 + h)),
                  pl.BlockSpec(blk, lambda i, h: (i, 0, c0 + ML_HEADS + h)),
                  pl.BlockSpec(blk, lambda i, h: (i, 0, c0 + 2 * ML_HEADS + h)),
                  pl.BlockSpec((1, 1, seq, 2), lambda i, h: (i, h, 0, 0)),
                  pl.BlockSpec((1, 1, 2, seq), lambda i, h: (i, h, 0, 0)),
                  pl.BlockSpec((1, 1, 2), lambda i, h: (h, 0, 0)),
                  pl.BlockSpec((ML_CONV, dh), lambda i, h: (0, h)),
                  vec, wspec, wspec, vec, vec],
        out_specs=pl.BlockSpec(blk, lambda i, h: (i, 0, h)),
        out_shape=jax.ShapeDtypeStruct((b, seq, ML_WIDTH), BF16),
        scratch_shapes=[pltpu.VMEM((seq, dh), F32),
                        pltpu.VMEM((seq, dh), BF16),
                        pltpu.VMEM((seq, dh), BF16),
                        pltpu.VMEM((dh, dh), F32),
                        pltpu.VMEM((1, dh), F32),
                        pltpu.VMEM((1, 1), F32)],
        compiler_params=_cparams(("parallel", "parallel")),
        name="mlstm",
    )(proj3, proj3, proj3, ifc, ifr, bif, conv_w, conv_b.reshape(1, -1), wq, wk,
      g_h.reshape(1, -1), skip.reshape(1, -1))


def _merge_body(x_ref, ya_ref, ys_ref, ym_ref, ga_ref, gs_ref, gm_ref, bg_ref,
                wa_ref, ws_ref, wm_ref, wo_ref, o_ref):
    d = x_ref.shape[1]
    bg = bg_ref[...]

    def branch(y_ref, w_ref, g_ref, k):
        gate = jax.nn.sigmoid(g_ref[...].astype(F32) + bg[:, k * d:(k + 1) * d])
        return gate * jnp.dot(y_ref[...], w_ref[0], preferred_element_type=F32)

    merged = branch(ya_ref, wa_ref, ga_ref, 0) + branch(ys_ref, ws_ref, gs_ref, 1) \
        + branch(ym_ref, wm_ref, gm_ref, 2)
    o_ref[...] = x_ref[...] + jnp.dot(merged.astype(BF16), wo_ref[0], preferred_element_type=F32)


def _merge(x2, ya, ys, ym, gates, b_gate, w_a, w_s, w_m, w_o, layer, tm):
    m, d = x2.shape
    tm = min(tm, m)
    row = lambda w: pl.BlockSpec((tm, w), lambda i: (i, 0))
    gate = lambda k: pl.BlockSpec((tm, d), lambda i: (i, k))
    wres = lambda w: _resident((1,) + w.shape[1:], lambda i: (layer, 0, 0))
    return pl.pallas_call(
        _merge_body,
        grid=(m // tm,),
        in_specs=[row(d), row(ya.shape[1]), row(ys.shape[1]), row(ym.shape[1]),
                  gate(0), gate(1), gate(2), _resident((1, 3 * d), lambda i: (0, 0)),
                  wres(w_a), wres(w_s), wres(w_m), wres(w_o)],
        out_specs=row(d),
        out_shape=jax.ShapeDtypeStruct((m, d), F32),
        compiler_params=_cparams(("parallel",)),
        name="merge",
    )(x2, ya, ys, ym, gates, gates, gates, b_gate.reshape(1, -1), w_a, w_s, w_m, w_o)


def _xattn_body(x_ref, g_ref, wq_ref, kv_ref, gq_ref, gk_ref, wo_ref, o_ref, q_s, att_s):
    d = x_ref.shape[1]
    dh = d // MEM_HEADS
    scale = dh ** -0.5
    hn = _rms(x_ref[...], g_ref[...]).astype(BF16)
    q_s[...] = jnp.dot(hn, wq_ref[0], preferred_element_type=F32)
    for h in range(MEM_HEADS):
        cols = slice(h * dh, (h + 1) * dh)
        qn = _rms(q_s[:, cols], gq_ref[...]).astype(BF16)
        kn = _rms(kv_ref[0, :, cols].astype(F32), gk_ref[...]).astype(BF16)
        s = lax.dot_general(qn, kn, _NT, preferred_element_type=F32) * scale
        p = jnp.exp(s - jnp.max(s, axis=-1, keepdims=True))
        l = jnp.sum(p, axis=-1, keepdims=True)
        v = kv_ref[0, :, d + h * dh:d + (h + 1) * dh]
        att_s[:, cols] = (jnp.dot(p.astype(BF16), v, preferred_element_type=F32) / l).astype(BF16)
    o_ref[...] = x_ref[...] + jnp.dot(att_s[...], wo_ref[0], preferred_element_type=F32)


def _xattn(x2, g, w_q, kv3, g_q, g_k, w_o, layer, seq, tm):
    m, d = x2.shape
    tm = min(tm, seq)
    per_seq = seq // tm
    wres = lambda w: _resident((1,) + w.shape[1:], lambda i: (layer, 0, 0))
    head = pl.BlockSpec((1, d // MEM_HEADS), lambda i: (0, 0))
    return pl.pallas_call(
        _xattn_body,
        grid=(m // tm,),
        in_specs=[pl.BlockSpec((tm, d), lambda i: (i, 0)),
                  pl.BlockSpec((1, d), lambda i: (0, 0)),
                  wres(w_q),
                  pl.BlockSpec((1,) + kv3.shape[1:], lambda i: (i // per_seq, 0, 0)),
                  head, head, wres(w_o)],
        out_specs=pl.BlockSpec((tm, d), lambda i: (i, 0)),
        out_shape=jax.ShapeDtypeStruct((m, d), F32),
        scratch_shapes=[pltpu.VMEM((tm, d), F32), pltpu.VMEM((tm, d), BF16)],
        compiler_params=_cparams(("parallel",)),
        name="mem_attention",
    )(x2, g.reshape(1, d), w_q, kv3, g_q.reshape(1, -1), g_k.reshape(1, -1), w_o)


def _ffn_up_body(x_ref, g_ref, wg_ref, wu_ref, a_ref, hn_ref, *, n_fill):
    rows = hn_ref.shape[0]
    s = pl.program_id(1)

    @pl.when(s < n_fill)
    def _():
        _fill_norm(x_ref, g_ref, hn_ref, s)

    @pl.when(s >= n_fill)
    def _():
        wg, wu = wg_ref[0].astype(BF16), wu_ref[0].astype(BF16)
        for r in range(0, rows, DOT_ROWS):
            hn = hn_ref[r:r + DOT_ROWS, :]
            gate = jnp.dot(hn, wg, preferred_element_type=F32)
            up = jnp.dot(hn, wu, preferred_element_type=F32)
            a_ref[r:r + DOT_ROWS, :] = (gate * jax.nn.sigmoid(gate) * up).astype(a_ref.dtype)


def _ffn_up(x2, g, w_gu, layer):
    m, d = x2.shape
    f = w_gu.shape[2] // 2
    rows = min(ROW_GROUP, m)
    n_fill = rows // NORM_ROWS
    n_tiles = f // COL_TILE
    tile = lambda s: jnp.clip(s - n_fill, 0, n_tiles - 1)
    return pl.pallas_call(
        functools.partial(_ffn_up_body, n_fill=n_fill),
        grid=(m // rows, n_fill + n_tiles),
        in_specs=[pl.BlockSpec((NORM_ROWS, d), lambda h, s: (h * n_fill + jnp.minimum(s, n_fill - 1), 0)),
                  pl.BlockSpec((1, d), lambda h, s: (0, 0)),
                  pl.BlockSpec((1, d, COL_TILE), lambda h, s: (layer, 0, tile(s))),
                  pl.BlockSpec((1, d, COL_TILE), lambda h, s: (layer, 0, n_tiles + tile(s)))],
        out_specs=pl.BlockSpec((rows, COL_TILE), lambda h, s: (h, tile(s))),
        out_shape=jax.ShapeDtypeStruct((m, f), BF16),
        scratch_shapes=[pltpu.VMEM((rows, d), BF16)],
        compiler_params=_cparams(("parallel", "arbitrary")),
        name="swiglu_up",
    )(x2, g.reshape(1, d), w_gu, w_gu)


def _ffn_down_body(x_ref, a_ref, w_ref, o_ref, wb_ref, *, n_load, k_tile):
    s = pl.program_id(0)

    @pl.when(s < n_load)
    def _():
        r0 = pl.multiple_of(s * k_tile, k_tile)
        wb_ref[pl.ds(r0, k_tile), :] = w_ref[0].astype(BF16)

    @pl.when(s >= n_load)
    def _():
        o_ref[...] = x_ref[...] + jnp.dot(a_ref[...], wb_ref[...], preferred_element_type=F32)


def _ffn_down(x2, act, w_down, layer, tm, k_tile):
    m, d = x2.shape
    f = act.shape[1]
    tm = min(tm, m)
    n_load = f // k_tile
    blk = lambda s: jnp.maximum(s - n_load, 0)
    return pl.pallas_call(
        functools.partial(_ffn_down_body, n_load=n_load, k_tile=k_tile),
        grid=(n_load + m // tm,),
        in_specs=[pl.BlockSpec((tm, d), lambda s: (blk(s), 0)),
                  pl.BlockSpec((tm, f), lambda s: (blk(s), 0)),
                  pl.BlockSpec((1, k_tile, d), lambda s: (layer, jnp.minimum(s, n_load - 1), 0))],
        out_specs=pl.BlockSpec((tm, d), lambda s: (blk(s), 0)),
        out_shape=jax.ShapeDtypeStruct((m, d), F32),
        scratch_shapes=[pltpu.VMEM((f, d), BF16)],
        compiler_params=_cparams(("arbitrary",)),
        name="swiglu_down",
    )(x2, act, w_down)


def kernel(x, mem, g_mem, norm_mix, w_in, b_gate, g_qa, g_ka, rel_bias, lam_re, lam_im, log_dt, b_re, b_im, c_re, c_im, d_skip, w_glu, b_glu, conv_w, conv_b, wq_m, wk_m, b_i, b_f, g_hm, skip_m, w_br_a, w_br_s, w_br_m, w_out, norm_x, w_xq, w_xkv, g_xq, g_xk, w_xo, norm_ffn, w_gu, w_down):
    b, seq, d = x.shape
    depth = w_in.shape[0]
    tokens = b * seq
    assert seq % ATT_QB == 0 and seq % ML_T == 0 and seq % SSM_T == 0
    assert tokens % min(ROW_GROUP, tokens) == 0

    w_glu_t = jnp.swapaxes(w_glu, 1, 2).astype(BF16)
    wq_b, wk_b = wq_m.astype(BF16), wk_m.astype(BF16)
    wa_b, ws_b, wm_b, wo_b = (w.astype(BF16) for w in (w_br_a, w_br_s, w_br_m, w_out))
    wxq_b, wxo_b = w_xq.astype(BF16), w_xo.astype(BF16)
    w_in_t = jnp.swapaxes(w_in, 1, 2)

    x2 = x.reshape(tokens, d)
    mem2 = mem.reshape(b * mem.shape[1], d)
    for i in range(depth):
        proj, u3, gates, ift = _in_proj(x2, norm_mix[i], w_in_t, i)
        proj3 = proj.reshape(b, seq, -1)

        ya = _chunk_attention(proj3, g_qa[i], g_ka[i], _att_bias(rel_bias[i]))

        yt3 = _s5_scan(u3, lam_re[i], lam_im[i], log_dt[i], b_re[i], b_im[i], c_re[i], c_im[i],
                       cpb=seq // SSM_T)
        ys = _s5_post(yt3, u3, d_skip[i], w_glu_t, i, b_glu[i], chunks=8)

        ifr = ift[:2 * ML_HEADS].reshape(2, ML_HEADS, b, seq).transpose(2, 1, 0, 3)
        ifc = jnp.swapaxes(ifr, 2, 3)
        bif = jnp.stack([b_i[i], b_f[i]], axis=-1).reshape(ML_HEADS, 1, 2)
        ym = _mlstm(proj3, ifc, ifr, bif, conv_w[i], conv_b[i], wq_b, wk_b, i, g_hm[i], skip_m[i],
                    3 * ATT_WIDTH)

        x2 = _merge(x2, ya.reshape(tokens, -1), ys, ym.reshape(tokens, -1), gates, b_gate[i],
                    wa_b, ws_b, wm_b, wo_b, i, tm=256)

        kv = _norm_proj(mem2, g_mem, w_xkv, i, tm=1024, tn=512)
        x2 = _xattn(x2, norm_x[i], wxq_b, kv.reshape(b, mem.shape[1], 2 * d), g_xq[i], g_xk[i],
                    wxo_b, i, seq, tm=512)

        act = _ffn_up(x2, norm_ffn[i], w_gu, i)
        x2 = _ffn_down(x2, act, w_down, i, tm=256, k_tile=256)
    return x2.reshape(b, seq, d)
```

```python
import functools

import jax
import jax.numpy as jnp
from jax import lax
from jax.experimental import pallas as pl
from jax.experimental.pallas import tpu as pltpu

F32 = jnp.float32
BF16 = jnp.bfloat16

EPS = 1e-6
LANES = 128
BF16_ROWS = 16
ATT_CHUNK = 64
ATT_HEADS = 8
ATT_HEAD_DIM = 128
ATT_WIDTH = ATT_HEADS * ATT_HEAD_DIM
ATT_LEFT = 8
REL_CLIP = 256
ATT_QB = 256
ATT_PAD = ATT_LEFT * ATT_CHUNK
ATT_KW = ATT_QB + ATT_PAD
ATT_EXT = 1024
MASK_NEG = -1e30
SSM_GROUP = 16
SSM_GROUPS = 48
SSM_WIDTH = SSM_GROUP * SSM_GROUPS
SSM_STATE = 64
SSM_T = LANES
SSM_SUB = 16
SSM_GROUPS_PER_STEP = 8
ML_HEADS = 4
ML_HEAD_DIM = 256
ML_WIDTH = ML_HEADS * ML_HEAD_DIM
ML_CONV = 4
ML_T = 256
ML_NEG = -1e30
MEM_HEADS = 4
ROW_GROUP = 4096
NORM_ROWS = 256
COL_TILE = 256
DOT_ROWS = 1024

VMEM_LIMIT = 56 * 1024 * 1024

_NT = (((1,), (1,)), ((), ()))
_TN = (((0,), (0,)), ((), ()))


def _cparams(sem):
    return pltpu.CompilerParams(dimension_semantics=sem, vmem_limit_bytes=VMEM_LIMIT)


def _rms(x, g):
    return x * lax.rsqrt(jnp.mean(x * x, axis=-1, keepdims=True) + EPS) * g


def _resident(shape, index_map):
    return pl.BlockSpec(shape, index_map, pipeline_mode=pl.Buffered(1))


def _fill_specs(d, n_fill):
    row_blk = lambda h, s: h * n_fill + jnp.minimum(s, n_fill - 1)
    return [pl.BlockSpec((NORM_ROWS, d // 2), lambda h, s, c=c: (row_blk(h, s), c)) for c in range(2)]


def _fill_norm(xa_ref, xb_ref, g_ref, hn_ref, step):
    r0 = pl.multiple_of(step * NORM_ROWS, NORM_ROWS)
    x = jnp.concatenate([xa_ref[...], xb_ref[...]], axis=1)
    hn_ref[pl.ds(r0, NORM_ROWS), :] = _rms(x, g_ref[...]).astype(BF16)


def _in_proj_body(xa_ref, xb_ref, g_ref, wm_ref, wg_ref, we_ref, p_ref, u_ref, gt_ref, if_ref, hn_ref,
                  *, n_fill, n_att, n_ssm, n_ml, if_cols):
    rows = hn_ref.shape[0]
    s = pl.program_id(1)
    t = s - n_fill
    t_gate = t - (n_att + n_ssm + n_ml)

    @pl.when(s < n_fill)
    def _():
        _fill_norm(xa_ref, xb_ref, g_ref, hn_ref, s)

    def project(wt, o_ref):
        for r in range(0, rows, DOT_ROWS):
            o_ref[r:r + DOT_ROWS, :] = lax.dot_general(
                hn_ref[r:r + DOT_ROWS, :], wt, _NT, preferred_element_type=F32).astype(o_ref.dtype)

    in_ssm = (t >= n_att) & (t < n_att + n_ssm)

    @pl.when((t >= 0) & (t_gate < 0) & jnp.logical_not(in_ssm))
    def _():
        project(wm_ref[0].astype(BF16), p_ref)

    @pl.when(in_ssm)
    def _():
        wt = wm_ref[0].astype(BF16)
        for r in range(0, rows, DOT_ROWS):
            ut = lax.dot_general(wt, hn_ref[r:r + DOT_ROWS, :], _NT, preferred_element_type=F32)
            for c in range(DOT_ROWS // SSM_T):
                u_ref[:, r // SSM_T + c, :] = ut[:, c * SSM_T:(c + 1) * SSM_T]

    @pl.when(t_gate == 0)
    def _():
        wif = wg_ref[0][:BF16_ROWS].astype(BF16)
        for r in range(0, rows, DOT_ROWS):
            if_ref[:, r:r + DOT_ROWS] = lax.dot_general(
                wif, hn_ref[r:r + DOT_ROWS, :], _NT, preferred_element_type=F32)

    @pl.when(t_gate >= 0)
    def _():
        wt = jnp.concatenate([wg_ref[0][if_cols:], we_ref[0]], axis=0)
        project(wt.astype(BF16), gt_ref)


def _in_proj(x2, g, w_in_t, layer):
    m, d = x2.shape
    rows = min(ROW_GROUP, m)
    n_fill = rows // NORM_ROWS
    n_att = 3 * ATT_WIDTH // COL_TILE
    n_ssm = SSM_WIDTH // COL_TILE
    n_ml = 3 * ML_WIDTH // COL_TILE
    n_main = n_att + n_ssm + n_ml
    if_cols = 2 * ML_HEADS
    gate_w = w_in_t.shape[1] - n_main * COL_TILE - if_cols
    n_gate = gate_w // COL_TILE
    assert gate_w % COL_TILE == 0 and if_cols <= BF16_ROWS and COL_TILE % if_cols == 0
    ext0 = (n_main + 1) * COL_TILE // if_cols
    ext_step = COL_TILE // if_cols

    def t_of(s):
        return s - n_fill

    def p_col(s):
        t = t_of(s)
        return jnp.where(t < n_att, jnp.clip(t, 0, n_att - 1), jnp.clip(t - n_ssm, n_att, n_att + n_ml - 1))

    def g_idx(s):
        return jnp.clip(t_of(s) - n_main, 0, n_gate - 1)

    body = functools.partial(_in_proj_body, n_fill=n_fill, n_att=n_att, n_ssm=n_ssm, n_ml=n_ml,
                             if_cols=if_cols)
    return pl.pallas_call(
        body,
        grid=(m // rows, n_fill + n_main + n_gate),
        in_specs=_fill_specs(d, n_fill) + [
                  pl.BlockSpec((1, d), lambda h, s: (0, 0)),
                  pl.BlockSpec((1, COL_TILE, d), lambda h, s: (layer, jnp.clip(t_of(s), 0, n_main - 1), 0)),
                  pl.BlockSpec((1, COL_TILE, d), lambda h, s: (layer, n_main + g_idx(s), 0)),
                  pl.BlockSpec((1, if_cols, d), lambda h, s: (layer, ext0 + ext_step * g_idx(s), 0))],
        out_specs=[pl.BlockSpec((rows, COL_TILE), lambda h, s: (h, p_col(s))),
                   pl.BlockSpec((COL_TILE, rows // SSM_T, SSM_T),
                                lambda h, s: (jnp.clip(t_of(s) - n_att, 0, n_ssm - 1), h, 0)),
                   pl.BlockSpec((rows, COL_TILE), lambda h, s: (h, g_idx(s))),
                   pl.BlockSpec((BF16_ROWS, rows), lambda h, s: (0, h))],
        out_shape=[jax.ShapeDtypeStruct((m, (n_att + n_ml) * COL_TILE), BF16),
                   jax.ShapeDtypeStruct((SSM_WIDTH, m // SSM_T, SSM_T), F32),
                   jax.ShapeDtypeStruct((m, gate_w), BF16),
                   jax.ShapeDtypeStruct((BF16_ROWS, m), F32)],
        scratch_shapes=[pltpu.VMEM((rows, d), BF16)],
        compiler_params=_cparams(("parallel", "arbitrary")),
        name="in_proj",
    )(x2, x2, g.reshape(1, d), w_in_t, w_in_t, w_in_t)


def _norm_proj_body(x_ref, g_ref, w_ref, o_ref, hn_ref):
    @pl.when(pl.program_id(1) == 0)
    def _():
        hn_ref[...] = _rms(x_ref[...], g_ref[...]).astype(BF16)

    o_ref[...] = jnp.dot(hn_ref[...], w_ref[0].astype(BF16),
                         preferred_element_type=F32).astype(o_ref.dtype)


def _norm_proj(x2, g, w, layer, tm, tn):
    m, d = x2.shape
    n = w.shape[2]
    tm = min(tm, m)
    return pl.pallas_call(
        _norm_proj_body,
        grid=(m // tm, n // tn),
        in_specs=[pl.BlockSpec((tm, d), lambda i, j: (i, 0)),
                  pl.BlockSpec((1, d), lambda i, j: (0, 0)),
                  pl.BlockSpec((1, d, tn), lambda i, j: (layer, 0, j))],
        out_specs=pl.BlockSpec((tm, tn), lambda i, j: (i, j)),
        out_shape=jax.ShapeDtypeStruct((m, n), BF16),
        scratch_shapes=[pltpu.VMEM((tm, d), BF16)],
        compiler_params=_cparams(("parallel", "arbitrary")),
        name="norm_proj",
    )(x2, g.reshape(1, d), w)


def _att_bias_body(ext_ref, o_ref):
    ext = ext_ref[0]
    base = pltpu.roll(ext, ATT_EXT - (ATT_QB - 1), axis=1)
    slab = jnp.broadcast_to(base, (ATT_QB, ATT_EXT))
    tab = pltpu.roll(slab, 0, 1, stride=1, stride_axis=0)[:, :ATT_KW]
    qc = lax.broadcasted_iota(jnp.int32, (ATT_QB, ATT_KW), 0) // ATT_CHUNK
    kc = lax.broadcasted_iota(jnp.int32, (ATT_QB, ATT_KW), 1) // ATT_CHUNK
    o_ref[0] = jnp.where((kc >= qc) & (kc <= qc + ATT_LEFT), tab, MASK_NEG)


def _att_bias(rel_bias):
    h = rel_bias.shape[0]
    lead = ATT_QB - 1 + ATT_PAD - REL_CLIP
    tail = ATT_EXT - lead - rel_bias.shape[1]
    ext = jnp.concatenate([jnp.broadcast_to(rel_bias[:, :1], (h, lead)), rel_bias,
                           jnp.broadcast_to(rel_bias[:, -1:], (h, tail))], axis=1)
    return pl.pallas_call(
        _att_bias_body,
        grid=(h,),
        in_specs=[pl.BlockSpec((1, 1, ATT_EXT), lambda i: (i, 0, 0))],
        out_specs=pl.BlockSpec((1, ATT_QB, ATT_KW), lambda i: (i, 0, 0)),
        out_shape=jax.ShapeDtypeStruct((h, ATT_QB, ATT_KW), F32),
        compiler_params=_cparams(("parallel",)),
        name="att_bias",
    )(ext.reshape(h, 1, ATT_EXT))


def _attn_body(q_ref, k_ref, v_ref, gq_ref, gk_ref, bias_ref, o_ref, kpad, vpad, *, seq):
    kpad[0:ATT_PAD, :] = jnp.zeros((ATT_PAD, ATT_HEAD_DIM), BF16)
    vpad[0:ATT_PAD, :] = jnp.zeros((ATT_PAD, ATT_HEAD_DIM), BF16)
    kpad[ATT_PAD:, :] = _rms(k_ref[0].astype(F32), gk_ref[...]).astype(BF16)
    vpad[ATT_PAD:, :] = v_ref[0]
    bias = bias_ref[0]
    scale = ATT_HEAD_DIM ** -0.5
    for qb in range(seq // ATT_QB):
        r0 = qb * ATT_QB
        qn = _rms(q_ref[0, r0:r0 + ATT_QB, :].astype(F32), gq_ref[...]).astype(BF16)
        kw = kpad[r0:r0 + ATT_KW, :]
        s = lax.dot_general(qn, kw, _NT, preferred_element_type=F32) * scale + bias
        if r0 < ATT_PAD:
            col = lax.broadcasted_iota(jnp.int32, (ATT_QB, ATT_KW), 1)
            s = jnp.where(col + r0 >= ATT_PAD, s, MASK_NEG)
        m = jnp.max(s, axis=-1, keepdims=True)
        p = jnp.exp(s - m)
        l = jnp.sum(p, axis=-1, keepdims=True)
        o = jnp.dot(p.astype(BF16), vpad[r0:r0 + ATT_KW, :], preferred_element_type=F32)
        o_ref[0, r0:r0 + ATT_QB, :] = (o / l).astype(o_ref.dtype)


def _chunk_attention(proj3, gq, gk, bias, layer):
    b, seq, _ = proj3.shape
    bias0 = layer * ATT_HEADS
    blk = (1, seq, ATT_HEAD_DIM)
    return pl.pallas_call(
        functools.partial(_attn_body, seq=seq),
        grid=(b, ATT_HEADS),
        in_specs=[pl.BlockSpec(blk, lambda i, h: (i, 0, h)),
                  pl.BlockSpec(blk, lambda i, h: (i, 0, ATT_HEADS + h)),
                  pl.BlockSpec(blk, lambda i, h: (i, 0, 2 * ATT_HEADS + h)),
                  pl.BlockSpec((1, ATT_HEAD_DIM), lambda i, h: (0, 0)),
                  pl.BlockSpec((1, ATT_HEAD_DIM), lambda i, h: (0, 0)),
                  pl.BlockSpec((1, ATT_QB, ATT_KW), lambda i, h: (bias0 + h, 0, 0))],
        out_specs=pl.BlockSpec(blk, lambda i, h: (i, 0, h)),
        out_shape=jax.ShapeDtypeStruct((b, seq, ATT_WIDTH), BF16),
        scratch_shapes=[pltpu.VMEM((seq + ATT_PAD, ATT_HEAD_DIM), BF16),
                        pltpu.VMEM((seq + ATT_PAD, ATT_HEAD_DIM), BF16)],
        compiler_params=_cparams(("parallel", "parallel")),
        name="chunk_attention",
    )(proj3, proj3, proj3, gq.reshape(1, -1), gk.reshape(1, -1), bias)


def _cmul(xr, xi, yr, yi):
    return xr * yr - xi * yi, xr * yi + xi * yr


def _s5_perm_tables(pmat, pmat_t):
    width = SSM_GROUP * SSM_T
    blk = SSM_GROUP * SSM_SUB
    sub_bits = SSM_SUB.bit_length() - 1
    t_bits = SSM_T.bit_length() - 1
    blk_bits = blk.bit_length() - 1
    for jb in range(width // blk):
        row = lax.broadcasted_iota(jnp.int32, (width, blk), 0)
        col = lax.broadcasted_iota(jnp.int32, (width, blk), 1) + jb * blk
        s0 = col & (SSM_SUB - 1)
        src = (((col & (blk - 1)) >> sub_bits) << t_bits) + ((col >> blk_bits) << sub_bits) + s0
        pmat[:, jb * blk:(jb + 1) * blk] = jnp.where(row == src, 1.0, 0.0).astype(BF16)
        src_t = (((col & (SSM_T - 1)) >> sub_bits) << blk_bits) + ((col >> t_bits) << sub_bits) + s0
        pmat_t[:, jb * blk:(jb + 1) * blk] = jnp.where(row == src_t, 1.0, 0.0).astype(BF16)


def _s5_group(gi, lr2_ref, li2_ref, ldt_ref, brt_ref, bit_ref, crx_ref, cix_ref,
              xp, yp, *, cpb, nch):
    t_len, sub, n_ch, n_st = SSM_T, SSM_SUB, SSM_GROUP, SSM_STATE
    n_sub = t_len // sub
    dt = jnp.exp(ldt_ref[gi])
    lr2, li2 = lr2_ref[gi], li2_ref[gi]
    first_half = lax.broadcasted_iota(jnp.int32, (1, 2 * n_st), 1) < n_st

    mag = jnp.exp(lr2 * dt)
    ar, ai = mag * jnp.cos(li2 * dt), mag * jnp.sin(li2 * dt)
    den = lr2 * lr2 + li2 * li2
    nr, ni = ar - 1.0, ai
    zr, zi = (nr * lr2 + ni * li2) / den, (ni * lr2 - nr * li2) / den
    bbr = zr * brt_ref[gi] - zi * bit_ref[gi]
    bbi = zr * bit_ref[gi] + zi * brt_ref[gi]

    sub_bits = sub.bit_length() - 1
    t_bits = t_len.bit_length() - 1
    sq = [(ar, ai)]
    while (1 << (len(sq) - 1)) * 2 < t_len * cpb:
        sq.append(_cmul(*sq[-1], *sq[-1]))

    def power(e, base):
        pr = pi = None
        for k in range(sub_bits):
            bit = ((e >> k) & 1) == 1
            fr, fi = jnp.where(bit, base[k][0], 1.0), jnp.where(bit, base[k][1], 0.0)
            pr, pi = (fr, fi) if pr is None else _cmul(pr, pi, fr, fi)
        return pr, pi

    def rotate(x, a):
        return x * a[0] + pltpu.roll(x, n_st, axis=1) * jnp.where(first_half, -a[1], a[1])

    e_row = lax.broadcasted_iota(jnp.int32, (sub, 2 * n_st), 0)
    qr, qi = power(sub - 1 - e_row, sq)
    f_r = jnp.where(first_half, bbr, bbi)
    f_i = jnp.where(first_half, -bbi, bbr)
    fmat = jnp.concatenate([(qr * f_r[mch:mch + 1, :] + qi * f_i[mch:mch + 1, :]).astype(BF16)
                            for mch in range(n_ch)], axis=0)

    sq_rows = jnp.concatenate([v for k in range(sub_bits) for v in sq[k]], axis=0)
    sq_cols = sq_rows.T
    base_c = [(sq_cols[:, 2 * k:2 * k + 1], sq_cols[:, 2 * k + 1:2 * k + 2]) for k in range(sub_bits)]
    tau = lax.broadcasted_iota(jnp.int32, (2 * n_st, n_ch * sub), 1) & (sub - 1)
    p0r, p0i = power(tau, base_c)
    p1r, p1i = _cmul(p0r, p0i, *base_c[0])
    top = lax.broadcasted_iota(jnp.int32, (2 * n_st, 1), 0) < n_st
    crx, cix = crx_ref[gi], cix_ref[gi]

    def c_times(pr, pi):
        return crx * jnp.where(top, pr, -pi) + cix * jnp.where(top, -pi, -pr)

    emat = c_times(p1r, p1i).astype(BF16)

    kflat = jnp.dot(f_r, c_times(p0r, p0i), preferred_element_type=F32, precision=lax.Precision.HIGHEST)
    lane_t = lax.broadcasted_iota(jnp.int32, (sub, n_ch * sub), 1) & (sub - 1)
    causal = lane_t >= lax.broadcasted_iota(jnp.int32, (sub, n_ch * sub), 0)
    bdiag = []
    for mch in range(n_ch):
        tz = pltpu.roll(jnp.broadcast_to(kflat[mch:mch + 1, :], (sub, n_ch * sub)), 0, 1,
                        stride=1, stride_axis=0)
        bdiag.append(jnp.where(causal, tz, 0.0).astype(BF16))
    bdiag = jnp.concatenate(bdiag, axis=0)

    rows = pl.ds(pl.multiple_of(gi * nch, nch), nch)
    xg = xp[rows, :]
    blk = n_ch * sub
    x2 = jnp.concatenate([xg[:, j * blk:(j + 1) * blk] for j in range(n_sub)], axis=0)
    s2 = jnp.dot(x2, fmat, preferred_element_type=F32)
    a_sub = sq[sub_bits]

    def run(state):
        entering = []
        for j in range(n_sub):
            entering.append(state)
            state = rotate(state, a_sub) + s2[j * nch:(j + 1) * nch]
        return entering, state

    _, s_loc = run(jnp.zeros((nch, 2 * n_st), F32))

    cidx = lax.broadcasted_iota(jnp.int32, (nch, 1), 0) % cpb
    xin = jnp.where(cidx >= 1, pltpu.roll(s_loc, 1, axis=0), 0.0)
    d = 1
    while d < cpb:
        sh = pltpu.roll(xin, d, axis=0)
        xin = xin + jnp.where(cidx >= d, rotate(sh, sq[t_bits + d.bit_length() - 1]), 0.0)
        d *= 2

    entering, _ = run(xin)
    xin2 = jnp.concatenate(entering, axis=0).astype(BF16)
    y2 = (jnp.dot(x2, bdiag, preferred_element_type=F32)
          + jnp.dot(xin2, emat, preferred_element_type=F32))
    yp[rows, :] = jnp.concatenate([y2[j * nch:(j + 1) * nch] for j in range(n_sub)], axis=1).astype(BF16)


def _s5_body(u_ref, lr2_ref, li2_ref, ldt_ref, brt_ref, bit_ref, crx_ref, cix_ref, yt_ref,
             pmat, pmat_t, xp, yp, *, cpb):
    n_ch, t_len = SSM_GROUP, SSM_T
    nch = u_ref.shape[1]
    gp = u_ref.shape[0] // n_ch

    @pl.when(pl.program_id(0) == 0)
    def _():
        _s5_perm_tables(pmat, pmat_t)

    xall = jnp.concatenate(
        [jnp.concatenate([u_ref[g * n_ch + mch].astype(BF16) for mch in range(n_ch)], axis=1)
         for g in range(gp)], axis=0)
    xp[...] = jnp.dot(xall, pmat[...], preferred_element_type=F32).astype(BF16)

    def group(gi, carry):
        _s5_group(gi, lr2_ref, li2_ref, ldt_ref, brt_ref, bit_ref, crx_ref, cix_ref,
                  xp, yp, cpb=cpb, nch=nch)
        return carry

    lax.fori_loop(0, gp, group, 0, unroll=4)

    yall = jnp.dot(yp[...], pmat_t[...], preferred_element_type=F32)
    for g in range(gp):
        for nn in range(n_ch):
            yt_ref[g * n_ch + nn] = yall[g * nch:(g + 1) * nch, nn * t_len:(nn + 1) * t_len]


def _s5_params(lam_re, lam_im, log_dt, b_re, b_im, c_re, c_im):
    lg, p = lam_re.shape[0] * lam_re.shape[1], SSM_STATE
    flat = lambda a: a.reshape((lg,) + a.shape[2:])
    dup = lambda a, axis: jnp.concatenate([a, a], axis=axis)
    lr2 = dup(flat(lam_re), 1).reshape(lg, 1, 2 * p)
    li2 = dup(flat(lam_im), 1).reshape(lg, 1, 2 * p)
    ldt = log_dt.reshape(lg, 1, 1)
    brt = dup(jnp.swapaxes(flat(b_re), 1, 2), 2)
    bit = dup(jnp.swapaxes(flat(b_im), 1, 2), 2)
    crx = jnp.repeat(dup(jnp.swapaxes(flat(c_re), 1, 2), 1), SSM_SUB, axis=2)
    cix = jnp.repeat(dup(jnp.swapaxes(flat(c_im), 1, 2), 1), SSM_SUB, axis=2)
    return lr2, li2, ldt, brt, bit, crx, cix


def _s5_scan(u3, params, layer, *, cpb):
    g, p, n, gp = SSM_GROUPS, SSM_STATE, SSM_GROUP, SSM_GROUPS_PER_STEP
    nch = u3.shape[1]
    width = n * SSM_T
    blk = n * SSM_SUB
    step0 = layer * (g // gp)
    spec = lambda *s: pl.BlockSpec((gp,) + s, lambda i: (step0 + i, 0, 0))
    return pl.pallas_call(
        functools.partial(_s5_body, cpb=cpb),
        grid=(g // gp,),
        in_specs=[pl.BlockSpec((gp * n, nch, SSM_T), lambda i: (i, 0, 0)),
                  spec(1, 2 * p), spec(1, 2 * p), spec(1, 1),
                  spec(n, 2 * p), spec(n, 2 * p), spec(2 * p, blk), spec(2 * p, blk)],
        out_specs=pl.BlockSpec((gp * n, nch, SSM_T), lambda i: (i, 0, 0)),
        out_shape=jax.ShapeDtypeStruct((SSM_WIDTH, nch, SSM_T), F32),
        scratch_shapes=[pltpu.VMEM((width, width), BF16),
                        pltpu.VMEM((width, width), BF16),
                        pltpu.VMEM((gp * nch, width), BF16),
                        pltpu.VMEM((gp * nch, width), BF16)],
        compiler_params=_cparams(("arbitrary",)),
        name="s5_scan",
    )(u3, *params)


def _s5_post_body(yt_ref, u_ref, dsk_ref, wgt_ref, bg_ref, o_ref):
    nc = u_ref.shape[1]
    yt = jnp.concatenate([yt_ref[:, c, :] for c in range(nc)], axis=1)
    ut = jnp.concatenate([u_ref[:, c, :] for c in range(nc)], axis=1)
    y = jax.nn.gelu(yt + dsk_ref[...] * ut)
    z = jnp.dot(wgt_ref[0], y.astype(BF16), preferred_element_type=F32) + bg_ref[...]
    o_ref[...] = (y * jax.nn.sigmoid(z)).T.astype(o_ref.dtype)


def _s5_post(yt3, u3, d_skip, w_glu_t, layer, b_glu, chunks):
    w, nch, t_len = yt3.shape
    chunks = min(chunks, nch)
    return pl.pallas_call(
        _s5_post_body,
        grid=(nch // chunks,),
        in_specs=[pl.BlockSpec((w, chunks, t_len), lambda i: (0, i, 0)),
                  pl.BlockSpec((w, chunks, t_len), lambda i: (0, i, 0)),
                  pl.BlockSpec((w, 1), lambda i: (0, 0)),
                  pl.BlockSpec((1, w, w), lambda i: (layer, 0, 0)),
                  pl.BlockSpec((w, 1), lambda i: (0, 0))],
        out_specs=pl.BlockSpec((chunks * t_len, w), lambda i: (i, 0)),
        out_shape=jax.ShapeDtypeStruct((nch * t_len, w), BF16),
        compiler_params=_cparams(("parallel",)),
        name="s5_post",
    )(yt3, u3, d_skip.reshape(w, 1), w_glu_t, b_glu.reshape(w, 1))


def _mlstm_body(xm_ref, vm_ref, om_ref, if_ref, bif_ref, cw_ref, cb_ref, wq_ref, wk_ref,
                gh_ref, sk_ref, o_ref, xc_s, q_s, k_s, cmat, nvec, mrun, ifr_s, ifc_s, *, seq):
    t_len, dh = ML_T, ML_HEAD_DIM
    head = pl.program_id(1)
    xm = xm_ref[0].astype(F32)
    rowi = lax.broadcasted_iota(jnp.int32, (seq, 1), 0)
    cw = cw_ref[...]
    acc = cw[ML_CONV - 1:ML_CONV, :] * xm + cb_ref[...]
    for j in range(1, ML_CONV):
        shifted = jnp.where(rowi >= j, pltpu.roll(xm, j, axis=0), 0.0)
        acc = acc + cw[ML_CONV - 1 - j:ML_CONV - j, :] * shifted
    xc = acc * jax.nn.sigmoid(acc)
    xc_s[...] = xc
    xcb = xc.astype(BF16)
    q_s[...] = jnp.dot(xcb, wq_ref[0, 0], preferred_element_type=F32).astype(BF16)
    k_s[...] = (jnp.dot(xcb, wk_ref[0, 0], preferred_element_type=F32) * (dh ** -0.5)).astype(BF16)

    cmat[...] = jnp.zeros_like(cmat)
    nvec[...] = jnp.zeros_like(nvec)
    mrun[...] = jnp.full_like(mrun, ML_NEG)
    b_i, b_f = bif_ref[0][:, 0:1], bif_ref[0][:, 1:2]
    ifr_s[...] = jnp.concatenate([if_ref[pl.ds(head, 1), :], if_ref[pl.ds(ML_HEADS + head, 1), :],
                                  jnp.zeros((6, seq), F32)], axis=0)
    ifc_s[...] = ifr_s[...].T
    iota_r = lax.broadcasted_iota(jnp.int32, (t_len, t_len), 0)
    iota_c = lax.broadcasted_iota(jnp.int32, (t_len, t_len), 1)
    tri = iota_c <= iota_r
    tri_t = iota_r <= iota_c

    def step(c, carry):
        r0 = pl.multiple_of(c * t_len, t_len)
        rows = pl.ds(r0, t_len)
        ifr = ifr_s[:, rows]
        ifc = ifc_s[rows, :]
        i_col, i_row = ifc[:, 0:1] + b_i, ifr[0:1, :] + b_i
        lf_col = jax.nn.log_sigmoid(ifc[:, 1:2] + b_f)
        lf_row = jax.nn.log_sigmoid(ifr[1:2, :] + b_f)
        bcum_col = jnp.sum(jnp.where(tri, lf_row, 0.0), axis=1, keepdims=True)
        bcum_row = jnp.sum(jnp.where(tri_t, lf_col, 0.0), axis=0, keepdims=True)
        b_last = jnp.sum(lf_row, axis=1, keepdims=True)
        m_prev = mrun[...]
        dmat = jnp.where(tri, bcum_col - bcum_row + i_row, -jnp.inf)
        inter = bcum_col + m_prev
        m_row = jnp.maximum(jnp.max(dmat, axis=1, keepdims=True), inter)
        w_intra = jnp.exp(dmat - m_row)
        w_inter = jnp.exp(inter - m_row)
        qq, kk, vv = q_s[rows, :], k_s[rows, :], vm_ref[0, rows, :]
        s = lax.dot_general(qq, kk, _NT, preferred_element_type=F32) * w_intra
        cm = cmat[...]
        num = (jnp.dot(s.astype(BF16), vv, preferred_element_type=F32)
               + w_inter * lax.dot_general(qq, cm.astype(BF16), _NT, preferred_element_type=F32))
        den = (jnp.sum(s, axis=1, keepdims=True)
               + w_inter * jnp.sum(qq.astype(F32) * nvec[...], axis=1, keepdims=True))
        h = num / jnp.maximum(jnp.abs(den), jnp.exp(-m_row))
        g_col = b_last - bcum_col + i_col
        m_new = jnp.maximum(b_last + m_prev, jnp.max(g_col, axis=0, keepdims=True))
        wg = jnp.exp(g_col - m_new)
        decay = jnp.exp(b_last + m_prev - m_new)
        vw = (vv.astype(F32) * wg).astype(BF16)
        cmat[...] = decay * cm + lax.dot_general(vw, kk, _TN, preferred_element_type=F32)
        nvec[...] = decay * nvec[...] + jnp.sum(wg * kk.astype(F32), axis=0, keepdims=True)
        mrun[...] = m_new
        hn = _rms(h, gh_ref[...]) + sk_ref[...] * xc_s[rows, :]
        o_ref[0, rows, :] = (jax.nn.sigmoid(om_ref[0, rows, :].astype(F32)) * hn).astype(o_ref.dtype)
        return carry

    lax.fori_loop(0, seq // t_len, step, 0)


def _mlstm(proj3, ift, bif, conv_w, conv_b, wq, wk, layer, g_h, skip, col0):
    b, seq, _ = proj3.shape
    dh = ML_HEAD_DIM
    c0 = col0 // dh
    blk = (1, seq, dh)
    vec = pl.BlockSpec((1, dh), lambda i, h: (0, h))
    wspec = pl.BlockSpec((1, 1, dh, dh), lambda i, h: (layer, h, 0, 0))
    return pl.pallas_call(
        functools.partial(_mlstm_body, seq=seq),
        grid=(b, ML_HEADS),
        in_specs=[pl.BlockSpec(blk, lambda i, h: (i, 0, c0 + h)),
                  pl.BlockSpec(blk, lambda i, h: (i, 0, c0 + ML_HEADS + h)),
                  pl.BlockSpec(blk, lambda i, h: (i, 0, c0 + 2 * ML_HEADS + h)),
                  pl.BlockSpec((ift.shape[0], seq), lambda i, h: (0, i)),
                  pl.BlockSpec((1, 1, 2), lambda i, h: (h, 0, 0)),
                  pl.BlockSpec((ML_CONV, dh), lambda i, h: (0, h)),
                  vec, wspec, wspec, vec, vec],
        out_specs=pl.BlockSpec(blk, lambda i, h: (i, 0, h)),
        out_shape=jax.ShapeDtypeStruct((b, seq, ML_WIDTH), BF16),
        scratch_shapes=[pltpu.VMEM((seq, dh), F32),
                        pltpu.VMEM((seq, dh), BF16),
                        pltpu.VMEM((seq, dh), BF16),
                        pltpu.VMEM((dh, dh), F32),
                        pltpu.VMEM((1, dh), F32),
                        pltpu.VMEM((1, 1), F32),
                        pltpu.VMEM((8, seq), F32),
                        pltpu.VMEM((seq, 8), F32)],
        compiler_params=_cparams(("parallel", "parallel")),
        name="mlstm",
    )(proj3, proj3, proj3, ift, bif, conv_w, conv_b.reshape(1, -1), wq, wk,
      g_h.reshape(1, -1), skip.reshape(1, -1))


def _merge_body(x_ref, ya_ref, ys_ref, ym_ref, ga_ref, gs_ref, gm_ref, bg_ref,
                wa_ref, ws_ref, wm_ref, wo_ref, o_ref):
    d = x_ref.shape[1]
    bg = bg_ref[...]

    def branch(y_ref, w_ref, g_ref, k):
        gate = jax.nn.sigmoid(g_ref[...].astype(F32) + bg[:, k * d:(k + 1) * d])
        return gate * jnp.dot(y_ref[...], w_ref[0], preferred_element_type=F32)

    merged = branch(ya_ref, wa_ref, ga_ref, 0) + branch(ys_ref, ws_ref, gs_ref, 1) \
        + branch(ym_ref, wm_ref, gm_ref, 2)
    o_ref[...] = x_ref[...] + jnp.dot(merged.astype(BF16), wo_ref[0], preferred_element_type=F32)


def _merge(x2, ya, ys, ym, gates, b_gate, w_a, w_s, w_m, w_o, layer, tm):
    m, d = x2.shape
    tm = min(tm, m)
    row = lambda w: pl.BlockSpec((tm, w), lambda i: (i, 0))
    gate = lambda k: pl.BlockSpec((tm, d), lambda i: (i, k))
    wres = lambda w: _resident((1,) + w.shape[1:], lambda i: (layer, 0, 0))
    return pl.pallas_call(
        _merge_body,
        grid=(m // tm,),
        in_specs=[row(d), row(ya.shape[1]), row(ys.shape[1]), row(ym.shape[1]),
                  gate(0), gate(1), gate(2), _resident((1, 3 * d), lambda i: (0, 0)),
                  wres(w_a), wres(w_s), wres(w_m), wres(w_o)],
        out_specs=row(d),
        out_shape=jax.ShapeDtypeStruct((m, d), F32),
        compiler_params=_cparams(("parallel",)),
        name="merge",
    )(x2, ya, ys, ym, gates, gates, gates, b_gate.reshape(1, -1), w_a, w_s, w_m, w_o)


def _xattn_body(x_ref, g_ref, wq_ref, kv_ref, gq_ref, gk_ref, wo_ref, o_ref, q_s, att_s):
    d = x_ref.shape[1]
    dh = d // MEM_HEADS
    scale = dh ** -0.5
    hn = _rms(x_ref[...], g_ref[...]).astype(BF16)
    q_s[...] = jnp.dot(hn, wq_ref[0], preferred_element_type=F32)
    for h in range(MEM_HEADS):
        cols = slice(h * dh, (h + 1) * dh)
        qn = _rms(q_s[:, cols], gq_ref[...]).astype(BF16)
        kn = _rms(kv_ref[0, :, cols].astype(F32), gk_ref[...]).astype(BF16)
        s = lax.dot_general(qn, kn, _NT, preferred_element_type=F32) * scale
        p = jnp.exp(s - jnp.max(s, axis=-1, keepdims=True))
        l = jnp.sum(p, axis=-1, keepdims=True)
        v = kv_ref[0, :, d + h * dh:d + (h + 1) * dh]
        att_s[:, cols] = (jnp.dot(p.astype(BF16), v, preferred_element_type=F32) / l).astype(BF16)
    o_ref[...] = x_ref[...] + jnp.dot(att_s[...], wo_ref[0], preferred_element_type=F32)


def _xattn(x2, g, w_q, kv3, g_q, g_k, w_o, layer, seq, tm):
    m, d = x2.shape
    tm = min(tm, seq)
    per_seq = seq // tm
    wres = lambda w: _resident((1,) + w.shape[1:], lambda i: (layer, 0, 0))
    head = pl.BlockSpec((1, d // MEM_HEADS), lambda i: (0, 0))
    return pl.pallas_call(
        _xattn_body,
        grid=(m // tm,),
        in_specs=[pl.BlockSpec((tm, d), lambda i: (i, 0)),
                  pl.BlockSpec((1, d), lambda i: (0, 0)),
                  wres(w_q),
                  pl.BlockSpec((1,) + kv3.shape[1:], lambda i: (i // per_seq, 0, 0)),
                  head, head, wres(w_o)],
        out_specs=pl.BlockSpec((tm, d), lambda i: (i, 0)),
        out_shape=jax.ShapeDtypeStruct((m, d), F32),
        scratch_shapes=[pltpu.VMEM((tm, d), F32), pltpu.VMEM((tm, d), BF16)],
        compiler_params=_cparams(("parallel",)),
        name="mem_attention",
    )(x2, g.reshape(1, d), w_q, kv3, g_q.reshape(1, -1), g_k.reshape(1, -1), w_o)


def _ffn_up_body(xa_ref, xb_ref, g_ref, wg_ref, wu_ref, a_ref, hn_ref, *, n_fill):
    rows = hn_ref.shape[0]
    s = pl.program_id(1)

    @pl.when(s < n_fill)
    def _():
        _fill_norm(xa_ref, xb_ref, g_ref, hn_ref, s)

    @pl.when(s >= n_fill)
    def _():
        wg, wu = wg_ref[0].astype(BF16), wu_ref[0].astype(BF16)
        for r in range(0, rows, DOT_ROWS):
            hn = hn_ref[r:r + DOT_ROWS, :]
            gate = jnp.dot(hn, wg, preferred_element_type=F32)
            up = jnp.dot(hn, wu, preferred_element_type=F32)
            a_ref[r:r + DOT_ROWS, :] = (gate * jax.nn.sigmoid(gate) * up).astype(a_ref.dtype)


def _ffn_up(x2, g, w_gu, layer):
    m, d = x2.shape
    f = w_gu.shape[2] // 2
    rows = min(ROW_GROUP, m)
    n_fill = rows // NORM_ROWS
    n_tiles = f // COL_TILE
    tile = lambda s: jnp.clip(s - n_fill, 0, n_tiles - 1)
    return pl.pallas_call(
        functools.partial(_ffn_up_body, n_fill=n_fill),
        grid=(m // rows, n_fill + n_tiles),
        in_specs=_fill_specs(d, n_fill) + [
                  pl.BlockSpec((1, d), lambda h, s: (0, 0)),
                  pl.BlockSpec((1, d, COL_TILE), lambda h, s: (layer, 0, tile(s))),
                  pl.BlockSpec((1, d, COL_TILE), lambda h, s: (layer, 0, n_tiles + tile(s)))],
        out_specs=pl.BlockSpec((rows, COL_TILE), lambda h, s: (h, tile(s))),
        out_shape=jax.ShapeDtypeStruct((m, f), BF16),
        scratch_shapes=[pltpu.VMEM((rows, d), BF16)],
        compiler_params=_cparams(("parallel", "arbitrary")),
        name="swiglu_up",
    )(x2, x2, g.reshape(1, d), w_gu, w_gu)


def _ffn_down_body(x_ref, a_ref, wa_ref, wb2_ref, o_ref, wb_ref, *, n_load, k_tile):
    s = pl.program_id(0)
    half = wa_ref.shape[2]

    @pl.when(s < n_load)
    def _():
        r0 = pl.multiple_of(s * k_tile, k_tile)
        wb_ref[pl.ds(r0, k_tile), :half] = wa_ref[0].astype(BF16)
        wb_ref[pl.ds(r0, k_tile), half:] = wb2_ref[0].astype(BF16)

    @pl.when(s >= n_load)
    def _():
        o_ref[...] = x_ref[...] + jnp.dot(a_ref[...], wb_ref[...], preferred_element_type=F32)


def _ffn_down(x2, act, w_down, layer, tm, k_tile):
    m, d = x2.shape
    f = act.shape[1]
    tm = min(tm, m)
    n_load = f // k_tile
    blk = lambda s: jnp.maximum(s - n_load, 0)
    return pl.pallas_call(
        functools.partial(_ffn_down_body, n_load=n_load, k_tile=k_tile),
        grid=(n_load + m // tm,),
        in_specs=[pl.BlockSpec((tm, d), lambda s: (blk(s), 0)),
                  pl.BlockSpec((tm, f), lambda s: (blk(s), 0)),
                  pl.BlockSpec((1, k_tile, d // 2), lambda s: (layer, jnp.minimum(s, n_load - 1), 0)),
                  pl.BlockSpec((1, k_tile, d // 2), lambda s: (layer, jnp.minimum(s, n_load - 1), 1))],
        out_specs=pl.BlockSpec((tm, d), lambda s: (blk(s), 0)),
        out_shape=jax.ShapeDtypeStruct((m, d), F32),
        scratch_shapes=[pltpu.VMEM((f, d), BF16)],
        compiler_params=_cparams(("arbitrary",)),
        name="swiglu_down",
    )(x2, act, w_down, w_down)


def kernel(x, mem, g_mem, norm_mix, w_in, b_gate, g_qa, g_ka, rel_bias, lam_re, lam_im, log_dt, b_re, b_im, c_re, c_im, d_skip, w_glu, b_glu, conv_w, conv_b, wq_m, wk_m, b_i, b_f, g_hm, skip_m, w_br_a, w_br_s, w_br_m, w_out, norm_x, w_xq, w_xkv, g_xq, g_xk, w_xo, norm_ffn, w_gu, w_down):
    b, seq, d = x.shape
    depth = w_in.shape[0]
    tokens = b * seq
    assert seq % ATT_QB == 0 and seq % ML_T == 0 and seq % SSM_T == 0
    assert tokens % min(ROW_GROUP, tokens) == 0

    w_glu_t = jnp.swapaxes(w_glu, 1, 2).astype(BF16)
    wq_b, wk_b = wq_m.astype(BF16), wk_m.astype(BF16)
    wa_b, ws_b, wm_b, wo_b = (w.astype(BF16) for w in (w_br_a, w_br_s, w_br_m, w_out))
    wxq_b, wxo_b = w_xq.astype(BF16), w_xo.astype(BF16)
    w_in_t = jnp.swapaxes(w_in, 1, 2)
    s5_params = _s5_params(lam_re, lam_im, log_dt, b_re, b_im, c_re, c_im)
    att_bias = _att_bias(rel_bias.reshape(depth * ATT_HEADS, -1))
    bif_all = jnp.stack([b_i, b_f], axis=-1).reshape(depth, ML_HEADS, 1, 2)

    x2 = x.reshape(tokens, d)
    mem2 = mem.reshape(b * mem.shape[1], d)
    for i in range(depth):
        proj, u3, gates, ift = _in_proj(x2, norm_mix[i], w_in_t, i)
        proj3 = proj.reshape(b, seq, -1)

        ya = _chunk_attention(proj3, g_qa[i], g_ka[i], att_bias, i)

        yt3 = _s5_scan(u3, s5_params, i, cpb=seq // SSM_T)
        ys = _s5_post(yt3, u3, d_skip[i], w_glu_t, i, b_glu[i], chunks=8)

        ym = _mlstm(proj3, ift, bif_all[i], conv_w[i], conv_b[i], wq_b, wk_b, i, g_hm[i], skip_m[i],
                    3 * ATT_WIDTH)

        x2 = _merge(x2, ya.reshape(tokens, -1), ys, ym.reshape(tokens, -1), gates, b_gate[i],
                    wa_b, ws_b, wm_b, wo_b, i, tm=256)

        kv = _norm_proj(mem2, g_mem, w_xkv, i, tm=1024, tn=512)
        x2 = _xattn(x2, norm_x[i], wxq_b, kv.reshape(b, mem.shape[1], 2 * d), g_xq[i], g_xk[i],
                    wxo_b, i, seq, tm=512)

        act = _ffn_up(x2, norm_ffn[i], w_gu, i)
        x2 = _ffn_down(x2, act, w_down, i, tm=256, k_tile=256)
    return x2.reshape(b, seq, d)
```

```python
import functools

import jax
import jax.numpy as jnp
from jax import lax
from jax.experimental import pallas as pl
from jax.experimental.pallas import tpu as pltpu

F32 = jnp.float32
BF16 = jnp.bfloat16

EPS = 1e-6
LANES = 128
BF16_ROWS = 16
ATT_CHUNK = 64
ATT_HEADS = 8
ATT_HEAD_DIM = 128
ATT_WIDTH = ATT_HEADS * ATT_HEAD_DIM
ATT_LEFT = 8
REL_CLIP = 256
ATT_QB = 256
ATT_PAD = ATT_LEFT * ATT_CHUNK
ATT_KW = ATT_QB + ATT_PAD
ATT_EXT = 1024
MASK_NEG = -1e30
SSM_GROUP = 16
SSM_GROUPS = 48
SSM_WIDTH = SSM_GROUP * SSM_GROUPS
SSM_STATE = 64
SSM_T = LANES
SSM_SUB = 16
SSM_GROUPS_PER_STEP = 8
ML_HEADS = 4
ML_HEAD_DIM = 256
ML_WIDTH = ML_HEADS * ML_HEAD_DIM
ML_CONV = 4
ML_T = 256
ML_NEG = -1e30
MEM_HEADS = 4
ROW_GROUP = 4096
NORM_ROWS = 256
COL_TILE = 256
DOT_ROWS = 1024

VMEM_LIMIT = 56 * 1024 * 1024

_NT = (((1,), (1,)), ((), ()))
_TN = (((0,), (0,)), ((), ()))


def _cparams(sem):
    return pltpu.CompilerParams(dimension_semantics=sem, vmem_limit_bytes=VMEM_LIMIT)


def _rms(x, g):
    return x * lax.rsqrt(jnp.mean(x * x, axis=-1, keepdims=True) + EPS) * g


def _resident(shape, index_map):
    return pl.BlockSpec(shape, index_map, pipeline_mode=pl.Buffered(1))


def _fill_specs(d, n_fill):
    row_blk = lambda h, s: h * n_fill + jnp.minimum(s, n_fill - 1)
    return [pl.BlockSpec((NORM_ROWS, d // 2), lambda h, s, c=c: (row_blk(h, s), c)) for c in range(2)]


def _fill_norm(xa_ref, xb_ref, g_ref, hn_ref, step):
    r0 = pl.multiple_of(step * NORM_ROWS, NORM_ROWS)
    x = jnp.concatenate([xa_ref[...], xb_ref[...]], axis=1)
    hn_ref[pl.ds(r0, NORM_ROWS), :] = _rms(x, g_ref[...]).astype(BF16)


def _in_proj_body(xa_ref, xb_ref, g_ref, wm_ref, wg_ref, we_ref, p_ref, u_ref, gt_ref, if_ref, hn_ref,
                  *, n_fill, n_att, n_ssm, n_ml, if_cols):
    rows = hn_ref.shape[0]
    s = pl.program_id(1)
    t = s - n_fill
    t_gate = t - (n_att + n_ssm + n_ml)

    @pl.when(s < n_fill)
    def _():
        _fill_norm(xa_ref, xb_ref, g_ref, hn_ref, s)

    def project(wt, o_ref):
        for r in range(0, rows, DOT_ROWS):
            o_ref[r:r + DOT_ROWS, :] = lax.dot_general(
                hn_ref[r:r + DOT_ROWS, :], wt, _NT, preferred_element_type=F32).astype(o_ref.dtype)

    in_ssm = (t >= n_att) & (t < n_att + n_ssm)

    @pl.when((t >= 0) & (t_gate < 0) & jnp.logical_not(in_ssm))
    def _():
        project(wm_ref[0].astype(BF16), p_ref)

    @pl.when(in_ssm)
    def _():
        wt = jnp.concatenate([wm_ref[0], wg_ref[0][:BF16_ROWS]], axis=0).astype(BF16)
        for r in range(0, rows, DOT_ROWS):
            ut = lax.dot_general(wt, hn_ref[r:r + DOT_ROWS, :], _NT, preferred_element_type=F32)
            for c in range(DOT_ROWS // SSM_T):
                u_ref[:, r // SSM_T + c, :] = ut[:COL_TILE, c * SSM_T:(c + 1) * SSM_T]
            if_ref[:, r:r + DOT_ROWS] = ut[COL_TILE:, :]

    @pl.when(t_gate >= 0)
    def _():
        wt = jnp.concatenate([wg_ref[0][if_cols:], we_ref[0]], axis=0)
        project(wt.astype(BF16), gt_ref)


def _in_proj(x2, g, w_in_t, layer):
    m, d = x2.shape
    rows = min(ROW_GROUP, m)
    n_fill = rows // NORM_ROWS
    n_att = 3 * ATT_WIDTH // COL_TILE
    n_ssm = SSM_WIDTH // COL_TILE
    n_ml = 3 * ML_WIDTH // COL_TILE
    n_main = n_att + n_ssm + n_ml
    if_cols = 2 * ML_HEADS
    gate_w = w_in_t.shape[1] - n_main * COL_TILE - if_cols
    n_gate = gate_w // COL_TILE
    assert gate_w % COL_TILE == 0 and if_cols <= BF16_ROWS and COL_TILE % if_cols == 0
    ext0 = (n_main + 1) * COL_TILE // if_cols
    ext_step = COL_TILE // if_cols

    def t_of(s):
        return s - n_fill

    def p_col(s):
        t = t_of(s)
        return jnp.where(t < n_att, jnp.clip(t, 0, n_att - 1), jnp.clip(t - n_ssm, n_att, n_att + n_ml - 1))

    def g_idx(s):
        return jnp.clip(t_of(s) - n_main, 0, n_gate - 1)

    body = functools.partial(_in_proj_body, n_fill=n_fill, n_att=n_att, n_ssm=n_ssm, n_ml=n_ml,
                             if_cols=if_cols)
    return pl.pallas_call(
        body,
        grid=(m // rows, n_fill + n_main + n_gate),
        in_specs=_fill_specs(d, n_fill) + [
                  pl.BlockSpec((1, d), lambda h, s: (0, 0)),
                  pl.BlockSpec((1, COL_TILE, d), lambda h, s: (layer, jnp.clip(t_of(s), 0, n_main - 1), 0)),
                  pl.BlockSpec((1, COL_TILE, d), lambda h, s: (layer, n_main + g_idx(s), 0)),
                  pl.BlockSpec((1, if_cols, d), lambda h, s: (layer, ext0 + ext_step * g_idx(s), 0))],
        out_specs=[pl.BlockSpec((rows, COL_TILE), lambda h, s: (h, p_col(s))),
                   pl.BlockSpec((COL_TILE, rows // SSM_T, SSM_T),
                                lambda h, s: (jnp.clip(t_of(s) - n_att, 0, n_ssm - 1), h, 0)),
                   pl.BlockSpec((rows, COL_TILE), lambda h, s: (h, g_idx(s))),
                   pl.BlockSpec((BF16_ROWS, rows), lambda h, s: (0, h))],
        out_shape=[jax.ShapeDtypeStruct((m, (n_att + n_ml) * COL_TILE), BF16),
                   jax.ShapeDtypeStruct((SSM_WIDTH, m // SSM_T, SSM_T), F32),
                   jax.ShapeDtypeStruct((m, gate_w), BF16),
                   jax.ShapeDtypeStruct((BF16_ROWS, m), F32)],
        scratch_shapes=[pltpu.VMEM((rows, d), BF16)],
        compiler_params=_cparams(("parallel", "arbitrary")),
        name="in_proj",
    )(x2, x2, g.reshape(1, d), w_in_t, w_in_t, w_in_t)


def _norm_proj_body(x_ref, g_ref, w_ref, o_ref, hn_ref):
    @pl.when(pl.program_id(1) == 0)
    def _():
        hn_ref[...] = _rms(x_ref[...], g_ref[...]).astype(BF16)

    o_ref[...] = jnp.dot(hn_ref[...], w_ref[0].astype(BF16),
                         preferred_element_type=F32).astype(o_ref.dtype)


def _norm_proj(x2, g, w, layer, tm, tn):
    m, d = x2.shape
    n = w.shape[2]
    tm = min(tm, m)
    return pl.pallas_call(
        _norm_proj_body,
        grid=(m // tm, n // tn),
        in_specs=[pl.BlockSpec((tm, d), lambda i, j: (i, 0)),
                  pl.BlockSpec((1, d), lambda i, j: (0, 0)),
                  pl.BlockSpec((1, d, tn), lambda i, j: (layer, 0, j))],
        out_specs=pl.BlockSpec((tm, tn), lambda i, j: (i, j)),
        out_shape=jax.ShapeDtypeStruct((m, n), BF16),
        scratch_shapes=[pltpu.VMEM((tm, d), BF16)],
        compiler_params=_cparams(("parallel", "arbitrary")),
        name="norm_proj",
    )(x2, g.reshape(1, d), w)


def _att_bias_body(ext_ref, o_ref):
    ext = ext_ref[0]
    base = pltpu.roll(ext, ATT_EXT - (ATT_QB - 1), axis=1)
    slab = jnp.broadcast_to(base, (ATT_QB, ATT_EXT))
    tab = pltpu.roll(slab, 0, 1, stride=1, stride_axis=0)[:, :ATT_KW]
    qc = lax.broadcasted_iota(jnp.int32, (ATT_QB, ATT_KW), 0) // ATT_CHUNK
    kc = lax.broadcasted_iota(jnp.int32, (ATT_QB, ATT_KW), 1) // ATT_CHUNK
    o_ref[0] = jnp.where((kc >= qc) & (kc <= qc + ATT_LEFT), tab, MASK_NEG)


def _att_bias(rel_bias):
    h = rel_bias.shape[0]
    lead = ATT_QB - 1 + ATT_PAD - REL_CLIP
    tail = ATT_EXT - lead - rel_bias.shape[1]
    ext = jnp.concatenate([jnp.broadcast_to(rel_bias[:, :1], (h, lead)), rel_bias,
                           jnp.broadcast_to(rel_bias[:, -1:], (h, tail))], axis=1)
    return pl.pallas_call(
        _att_bias_body,
        grid=(h,),
        in_specs=[pl.BlockSpec((1, 1, ATT_EXT), lambda i: (i, 0, 0))],
        out_specs=pl.BlockSpec((1, ATT_QB, ATT_KW), lambda i: (i, 0, 0)),
        out_shape=jax.ShapeDtypeStruct((h, ATT_QB, ATT_KW), F32),
        compiler_params=_cparams(("parallel",)),
        name="att_bias",
    )(ext.reshape(h, 1, ATT_EXT))


def _attn_body(q_ref, k_ref, v_ref, gq_ref, gk_ref, bias_ref, o_ref, kpad, vpad, *, seq):
    kpad[0:ATT_PAD, :] = jnp.zeros((ATT_PAD, ATT_HEAD_DIM), BF16)
    vpad[0:ATT_PAD, :] = jnp.zeros((ATT_PAD, ATT_HEAD_DIM), BF16)
    kpad[ATT_PAD:, :] = _rms(k_ref[0].astype(F32), gk_ref[...]).astype(BF16)
    vpad[ATT_PAD:, :] = v_ref[0]
    bias = bias_ref[0]
    scale = ATT_HEAD_DIM ** -0.5
    for qb in range(seq // ATT_QB):
        r0 = qb * ATT_QB
        qn = _rms(q_ref[0, r0:r0 + ATT_QB, :].astype(F32), gq_ref[...]).astype(BF16)
        kw = kpad[r0:r0 + ATT_KW, :]
        s = lax.dot_general(qn, kw, _NT, preferred_element_type=F32) * scale + bias
        if r0 < ATT_PAD:
            col = lax.broadcasted_iota(jnp.int32, (ATT_QB, ATT_KW), 1)
            s = jnp.where(col + r0 >= ATT_PAD, s, MASK_NEG)
        m = jnp.max(s, axis=-1, keepdims=True)
        p = jnp.exp(s - m)
        l = jnp.sum(p, axis=-1, keepdims=True)
        o = jnp.dot(p.astype(BF16), vpad[r0:r0 + ATT_KW, :], preferred_element_type=F32)
        o_ref[0, r0:r0 + ATT_QB, :] = (o / l).astype(o_ref.dtype)


def _chunk_attention(proj3, gq, gk, bias, layer):
    b, seq, _ = proj3.shape
    bias0 = layer * ATT_HEADS
    blk = (1, seq, ATT_HEAD_DIM)
    return pl.pallas_call(
        functools.partial(_attn_body, seq=seq),
        grid=(b, ATT_HEADS),
        in_specs=[pl.BlockSpec(blk, lambda i, h: (i, 0, h)),
                  pl.BlockSpec(blk, lambda i, h: (i, 0, ATT_HEADS + h)),
                  pl.BlockSpec(blk, lambda i, h: (i, 0, 2 * ATT_HEADS + h)),
                  pl.BlockSpec((1, ATT_HEAD_DIM), lambda i, h: (0, 0)),
                  pl.BlockSpec((1, ATT_HEAD_DIM), lambda i, h: (0, 0)),
                  pl.BlockSpec((1, ATT_QB, ATT_KW), lambda i, h: (bias0 + h, 0, 0))],
        out_specs=pl.BlockSpec(blk, lambda i, h: (i, 0, h)),
        out_shape=jax.ShapeDtypeStruct((b, seq, ATT_WIDTH), BF16),
        scratch_shapes=[pltpu.VMEM((seq + ATT_PAD, ATT_HEAD_DIM), BF16),
                        pltpu.VMEM((seq + ATT_PAD, ATT_HEAD_DIM), BF16)],
        compiler_params=_cparams(("parallel", "parallel")),
        name="chunk_attention",
    )(proj3, proj3, proj3, gq.reshape(1, -1), gk.reshape(1, -1), bias)


def _cmul(xr, xi, yr, yi):
    return xr * yr - xi * yi, xr * yi + xi * yr


def _s5_perm_tables(pmat, pmat_t):
    width = SSM_GROUP * SSM_T
    blk = SSM_GROUP * SSM_SUB
    sub_bits = SSM_SUB.bit_length() - 1
    t_bits = SSM_T.bit_length() - 1
    blk_bits = blk.bit_length() - 1
    for jb in range(width // blk):
        row = lax.broadcasted_iota(jnp.int32, (width, blk), 0)
        col = lax.broadcasted_iota(jnp.int32, (width, blk), 1) + jb * blk
        s0 = col & (SSM_SUB - 1)
        src = (((col & (blk - 1)) >> sub_bits) << t_bits) + ((col >> blk_bits) << sub_bits) + s0
        pmat[:, jb * blk:(jb + 1) * blk] = jnp.where(row == src, 1.0, 0.0).astype(BF16)
        src_t = (((col & (SSM_T - 1)) >> sub_bits) << blk_bits) + ((col >> t_bits) << sub_bits) + s0
        pmat_t[:, jb * blk:(jb + 1) * blk] = jnp.where(row == src_t, 1.0, 0.0).astype(BF16)


def _s5_group(gi, lr2_ref, li2_ref, ldt_ref, brt_ref, bit_ref, crx_ref, cix_ref,
              xp, yp, *, cpb, nch):
    t_len, sub, n_ch, n_st = SSM_T, SSM_SUB, SSM_GROUP, SSM_STATE
    n_sub = t_len // sub
    dt = jnp.exp(ldt_ref[gi])
    lr2, li2 = lr2_ref[gi], li2_ref[gi]
    first_half = lax.broadcasted_iota(jnp.int32, (1, 2 * n_st), 1) < n_st

    mag = jnp.exp(lr2 * dt)
    ar, ai = mag * jnp.cos(li2 * dt), mag * jnp.sin(li2 * dt)
    den = lr2 * lr2 + li2 * li2
    nr, ni = ar - 1.0, ai
    zr, zi = (nr * lr2 + ni * li2) / den, (ni * lr2 - nr * li2) / den
    bbr = zr * brt_ref[gi] - zi * bit_ref[gi]
    bbi = zr * bit_ref[gi] + zi * brt_ref[gi]

    sub_bits = sub.bit_length() - 1
    t_bits = t_len.bit_length() - 1
    sq = [(ar, ai)]
    while (1 << (len(sq) - 1)) * 2 < t_len * cpb:
        sq.append(_cmul(*sq[-1], *sq[-1]))

    def power(e, base):
        pr = pi = None
        for k in range(sub_bits):
            bit = ((e >> k) & 1) == 1
            fr, fi = jnp.where(bit, base[k][0], 1.0), jnp.where(bit, base[k][1], 0.0)
            pr, pi = (fr, fi) if pr is None else _cmul(pr, pi, fr, fi)
        return pr, pi

    def rotate(x, a):
        return x * a[0] + pltpu.roll(x, n_st, axis=1) * jnp.where(first_half, -a[1], a[1])

    e_row = lax.broadcasted_iota(jnp.int32, (sub, 2 * n_st), 0)
    qr, qi = power(sub - 1 - e_row, sq)
    f_r = jnp.where(first_half, bbr, bbi)
    f_i = jnp.where(first_half, -bbi, bbr)
    fmat = jnp.concatenate([(qr * f_r[mch:mch + 1, :] + qi * f_i[mch:mch + 1, :]).astype(BF16)
                            for mch in range(n_ch)], axis=0)

    sq_rows = jnp.concatenate([v for k in range(sub_bits) for v in sq[k]], axis=0)
    sq_cols = sq_rows.T
    base_c = [(sq_cols[:, 2 * k:2 * k + 1], sq_cols[:, 2 * k + 1:2 * k + 2]) for k in range(sub_bits)]
    tau = lax.broadcasted_iota(jnp.int32, (2 * n_st, n_ch * sub), 1) & (sub - 1)
    p0r, p0i = power(tau, base_c)
    p1r, p1i = _cmul(p0r, p0i, *base_c[0])
    top = lax.broadcasted_iota(jnp.int32, (2 * n_st, 1), 0) < n_st
    crx, cix = crx_ref[gi], cix_ref[gi]

    def c_times(pr, pi):
        return crx * jnp.where(top, pr, -pi) + cix * jnp.where(top, -pi, -pr)

    emat = c_times(p1r, p1i).astype(BF16)

    kflat = jnp.dot(f_r, c_times(p0r, p0i), preferred_element_type=F32, precision=lax.Precision.HIGHEST)
    lane_t = lax.broadcasted_iota(jnp.int32, (sub, n_ch * sub), 1) & (sub - 1)
    causal = lane_t >= lax.broadcasted_iota(jnp.int32, (sub, n_ch * sub), 0)
    bdiag = []
    for mch in range(n_ch):
        tz = pltpu.roll(jnp.broadcast_to(kflat[mch:mch + 1, :], (sub, n_ch * sub)), 0, 1,
                        stride=1, stride_axis=0)
        bdiag.append(jnp.where(causal, tz, 0.0).astype(BF16))
    bdiag = jnp.concatenate(bdiag, axis=0)

    rows = pl.ds(pl.multiple_of(gi * nch, nch), nch)
    xg = xp[rows, :]
    blk = n_ch * sub
    x2 = jnp.concatenate([xg[:, j * blk:(j + 1) * blk] for j in range(n_sub)], axis=0)
    s2 = jnp.dot(x2, fmat, preferred_element_type=F32)
    a_sub = sq[sub_bits]

    def run(state):
        entering = []
        for j in range(n_sub):
            entering.append(state)
            state = rotate(state, a_sub) + s2[j * nch:(j + 1) * nch]
        return entering, state

    _, s_loc = run(jnp.zeros((nch, 2 * n_st), F32))

    cidx = lax.broadcasted_iota(jnp.int32, (nch, 1), 0) % cpb
    xin = jnp.where(cidx >= 1, pltpu.roll(s_loc, 1, axis=0), 0.0)
    d = 1
    while d < cpb:
        sh = pltpu.roll(xin, d, axis=0)
        xin = xin + jnp.where(cidx >= d, rotate(sh, sq[t_bits + d.bit_length() - 1]), 0.0)
        d *= 2

    entering, _ = run(xin)
    xin2 = jnp.concatenate(entering, axis=0).astype(BF16)
    y2 = (jnp.dot(x2, bdiag, preferred_element_type=F32)
          + jnp.dot(xin2, emat, preferred_element_type=F32))
    yp[rows, :] = jnp.concatenate([y2[j * nch:(j + 1) * nch] for j in range(n_sub)], axis=1).astype(BF16)


def _s5_body(u_ref, lr2_ref, li2_ref, ldt_ref, brt_ref, bit_ref, crx_ref, cix_ref, yt_ref,
             pmat, pmat_t, xp, yp, *, cpb):
    n_ch, t_len = SSM_GROUP, SSM_T
    nch = u_ref.shape[1]
    gp = u_ref.shape[0] // n_ch

    @pl.when(pl.program_id(0) == 0)
    def _():
        _s5_perm_tables(pmat, pmat_t)

    xall = jnp.concatenate(
        [jnp.concatenate([u_ref[g * n_ch + mch].astype(BF16) for mch in range(n_ch)], axis=1)
         for g in range(gp)], axis=0)
    xp[...] = jnp.dot(xall, pmat[...], preferred_element_type=F32).astype(BF16)

    def group(gi, carry):
        _s5_group(gi, lr2_ref, li2_ref, ldt_ref, brt_ref, bit_ref, crx_ref, cix_ref,
                  xp, yp, cpb=cpb, nch=nch)
        return carry

    lax.fori_loop(0, gp, group, 0, unroll=4)

    yall = jnp.dot(yp[...], pmat_t[...], preferred_element_type=F32)
    for g in range(gp):
        for nn in range(n_ch):
            yt_ref[g * n_ch + nn] = yall[g * nch:(g + 1) * nch, nn * t_len:(nn + 1) * t_len]


def _s5_params(lam_re, lam_im, log_dt, b_re, b_im, c_re, c_im):
    lg, p = lam_re.shape[0] * lam_re.shape[1], SSM_STATE
    flat = lambda a: a.reshape((lg,) + a.shape[2:])
    dup = lambda a, axis: jnp.concatenate([a, a], axis=axis)
    lr2 = dup(flat(lam_re), 1).reshape(lg, 1, 2 * p)
    li2 = dup(flat(lam_im), 1).reshape(lg, 1, 2 * p)
    ldt = log_dt.reshape(lg, 1, 1)
    brt = dup(jnp.swapaxes(flat(b_re), 1, 2), 2)
    bit = dup(jnp.swapaxes(flat(b_im), 1, 2), 2)
    crx = jnp.repeat(dup(jnp.swapaxes(flat(c_re), 1, 2), 1), SSM_SUB, axis=2)
    cix = jnp.repeat(dup(jnp.swapaxes(flat(c_im), 1, 2), 1), SSM_SUB, axis=2)
    return lr2, li2, ldt, brt, bit, crx, cix


def _s5_scan(u3, params, layer, *, cpb):
    g, p, n, gp = SSM_GROUPS, SSM_STATE, SSM_GROUP, SSM_GROUPS_PER_STEP
    nch = u3.shape[1]
    width = n * SSM_T
    blk = n * SSM_SUB
    step0 = layer * (g // gp)
    spec = lambda *s: pl.BlockSpec((gp,) + s, lambda i: (step0 + i, 0, 0))
    return pl.pallas_call(
        functools.partial(_s5_body, cpb=cpb),
        grid=(g // gp,),
        in_specs=[pl.BlockSpec((gp * n, nch, SSM_T), lambda i: (i, 0, 0)),
                  spec(1, 2 * p), spec(1, 2 * p), spec(1, 1),
                  spec(n, 2 * p), spec(n, 2 * p), spec(2 * p, blk), spec(2 * p, blk)],
        out_specs=pl.BlockSpec((gp * n, nch, SSM_T), lambda i: (i, 0, 0)),
        out_shape=jax.ShapeDtypeStruct((SSM_WIDTH, nch, SSM_T), F32),
        scratch_shapes=[pltpu.VMEM((width, width), BF16),
                        pltpu.VMEM((width, width), BF16),
                        pltpu.VMEM((gp * nch, width), BF16),
                        pltpu.VMEM((gp * nch, width), BF16)],
        compiler_params=_cparams(("arbitrary",)),
        name="s5_scan",
    )(u3, *params)


def _s5_post_body(yt_ref, u_ref, dsk_ref, wgt_ref, bg_ref, o_ref):
    nc = u_ref.shape[1]
    yt = jnp.concatenate([yt_ref[:, c, :] for c in range(nc)], axis=1)
    ut = jnp.concatenate([u_ref[:, c, :] for c in range(nc)], axis=1)
    y = jax.nn.gelu(yt + dsk_ref[...] * ut)
    z = jnp.dot(wgt_ref[0], y.astype(BF16), preferred_element_type=F32) + bg_ref[...]
    o_ref[...] = (y * jax.nn.sigmoid(z)).T.astype(o_ref.dtype)


def _s5_post(yt3, u3, d_skip, w_glu_t, layer, b_glu, chunks):
    w, nch, t_len = yt3.shape
    chunks = min(chunks, nch)
    return pl.pallas_call(
        _s5_post_body,
        grid=(nch // chunks,),
        in_specs=[pl.BlockSpec((w, chunks, t_len), lambda i: (0, i, 0)),
                  pl.BlockSpec((w, chunks, t_len), lambda i: (0, i, 0)),
                  pl.BlockSpec((w, 1), lambda i: (0, 0)),
                  pl.BlockSpec((1, w, w), lambda i: (layer, 0, 0)),
                  pl.BlockSpec((w, 1), lambda i: (0, 0))],
        out_specs=pl.BlockSpec((chunks * t_len, w), lambda i: (i, 0)),
        out_shape=jax.ShapeDtypeStruct((nch * t_len, w), BF16),
        compiler_params=_cparams(("parallel",)),
        name="s5_post",
    )(yt3, u3, d_skip.reshape(w, 1), w_glu_t, b_glu.reshape(w, 1))


def _mlstm_body(xm_ref, vm_ref, om_ref, if_ref, bif_ref, cw_ref, cb_ref, wq_ref, wk_ref,
                gh_ref, sk_ref, o_ref, xc_s, q_s, k_s, cmat, nvec, mrun, ifr_s, ifc_s, *, seq):
    t_len, dh = ML_T, ML_HEAD_DIM
    head = pl.program_id(1)
    xm = xm_ref[0].astype(F32)
    rowi = lax.broadcasted_iota(jnp.int32, (seq, 1), 0)
    cw = cw_ref[...]
    acc = cw[ML_CONV - 1:ML_CONV, :] * xm + cb_ref[...]
    for j in range(1, ML_CONV):
        shifted = jnp.where(rowi >= j, pltpu.roll(xm, j, axis=0), 0.0)
        acc = acc + cw[ML_CONV - 1 - j:ML_CONV - j, :] * shifted
    xc = acc * jax.nn.sigmoid(acc)
    xc_s[...] = xc
    xcb = xc.astype(BF16)
    q_s[...] = jnp.dot(xcb, wq_ref[0, 0], preferred_element_type=F32).astype(BF16)
    k_s[...] = (jnp.dot(xcb, wk_ref[0, 0], preferred_element_type=F32) * (dh ** -0.5)).astype(BF16)

    cmat[...] = jnp.zeros_like(cmat)
    nvec[...] = jnp.zeros_like(nvec)
    mrun[...] = jnp.full_like(mrun, ML_NEG)
    b_i, b_f = bif_ref[0][:, 0:1], bif_ref[0][:, 1:2]
    ifr_s[...] = jnp.concatenate([if_ref[pl.ds(head, 1), :], if_ref[pl.ds(ML_HEADS + head, 1), :],
                                  jnp.zeros((6, seq), F32)], axis=0)
    ifc_s[...] = ifr_s[...].T
    iota_r = lax.broadcasted_iota(jnp.int32, (t_len, t_len), 0)
    iota_c = lax.broadcasted_iota(jnp.int32, (t_len, t_len), 1)
    tri = iota_c <= iota_r
    tri_t = iota_r <= iota_c

    def step(c, carry):
        r0 = pl.multiple_of(c * t_len, t_len)
        rows = pl.ds(r0, t_len)
        ifr = ifr_s[:, rows]
        ifc = ifc_s[rows, :]
        i_col, i_row = ifc[:, 0:1] + b_i, ifr[0:1, :] + b_i
        lf_col = jax.nn.log_sigmoid(ifc[:, 1:2] + b_f)
        lf_row = jax.nn.log_sigmoid(ifr[1:2, :] + b_f)
        bcum_col = jnp.sum(jnp.where(tri, lf_row, 0.0), axis=1, keepdims=True)
        bcum_row = jnp.sum(jnp.where(tri_t, lf_col, 0.0), axis=0, keepdims=True)
        b_last = jnp.sum(lf_row, axis=1, keepdims=True)
        m_prev = mrun[...]
        dmat = jnp.where(tri, bcum_col - bcum_row + i_row, -jnp.inf)
        inter = bcum_col + m_prev
        m_row = jnp.maximum(jnp.max(dmat, axis=1, keepdims=True), inter)
        w_intra = jnp.exp(dmat - m_row)
        w_inter = jnp.exp(inter - m_row)
        qq, kk, vv = q_s[rows, :], k_s[rows, :], vm_ref[0, rows, :]
        s = lax.dot_general(qq, kk, _NT, preferred_element_type=F32) * w_intra
        cm = cmat[...]
        num = (jnp.dot(s.astype(BF16), vv, preferred_element_type=F32)
               + w_inter * lax.dot_general(qq, cm.astype(BF16), _NT, preferred_element_type=F32))
        den = (jnp.sum(s, axis=1, keepdims=True)
               + w_inter * jnp.sum(qq.astype(F32) * nvec[...], axis=1, keepdims=True))
        h = num / jnp.maximum(jnp.abs(den), jnp.exp(-m_row))
        g_col = b_last - bcum_col + i_col
        m_new = jnp.maximum(b_last + m_prev, jnp.max(g_col, axis=0, keepdims=True))
        wg = jnp.exp(g_col - m_new)
        decay = jnp.exp(b_last + m_prev - m_new)
        vw = (vv.astype(F32) * wg).astype(BF16)
        cmat[...] = decay * cm + lax.dot_general(vw, kk, _TN, preferred_element_type=F32)
        nvec[...] = decay * nvec[...] + jnp.sum(wg * kk.astype(F32), axis=0, keepdims=True)
        mrun[...] = m_new
        hn = _rms(h, gh_ref[...]) + sk_ref[...] * xc_s[rows, :]
        o_ref[0, rows, :] = (jax.nn.sigmoid(om_ref[0, rows, :].astype(F32)) * hn).astype(o_ref.dtype)
        return carry

    lax.fori_loop(0, seq // t_len, step, 0, unroll=2)


def _mlstm(proj3, ift, bif, conv_w, conv_b, wq, wk, layer, g_h, skip, col0):
    b, seq, _ = proj3.shape
    dh = ML_HEAD_DIM
    c0 = col0 // dh
    blk = (1, seq, dh)
    vec = pl.BlockSpec((1, dh), lambda i, h: (0, h))
    wspec = pl.BlockSpec((1, 1, dh, dh), lambda i, h: (layer, h, 0, 0))
    return pl.pallas_call(
        functools.partial(_mlstm_body, seq=seq),
        grid=(b, ML_HEADS),
        in_specs=[pl.BlockSpec(blk, lambda i, h: (i, 0, c0 + h)),
                  pl.BlockSpec(blk, lambda i, h: (i, 0, c0 + ML_HEADS + h)),
                  pl.BlockSpec(blk, lambda i, h: (i, 0, c0 + 2 * ML_HEADS + h)),
                  pl.BlockSpec((ift.shape[0], seq), lambda i, h: (0, i)),
                  pl.BlockSpec((1, 1, 2), lambda i, h: (h, 0, 0)),
                  pl.BlockSpec((ML_CONV, dh), lambda i, h: (0, h)),
                  vec, wspec, wspec, vec, vec],
        out_specs=pl.BlockSpec(blk, lambda i, h: (i, 0, h)),
        out_shape=jax.ShapeDtypeStruct((b, seq, ML_WIDTH), BF16),
        scratch_shapes=[pltpu.VMEM((seq, dh), F32),
                        pltpu.VMEM((seq, dh), BF16),
                        pltpu.VMEM((seq, dh), BF16),
                        pltpu.VMEM((dh, dh), F32),
                        pltpu.VMEM((1, dh), F32),
                        pltpu.VMEM((1, 1), F32),
                        pltpu.VMEM((8, seq), F32),
                        pltpu.VMEM((seq, 8), F32)],
        compiler_params=_cparams(("parallel", "parallel")),
        name="mlstm",
    )(proj3, proj3, proj3, ift, bif, conv_w, conv_b.reshape(1, -1), wq, wk,
      g_h.reshape(1, -1), skip.reshape(1, -1))


def _merge_body(x_ref, ya_ref, ys_ref, ym_ref, ga_ref, gs_ref, gm_ref, bg_ref,
                wa_ref, ws_ref, wm_ref, wo_ref, o_ref):
    d = x_ref.shape[1]
    bg = bg_ref[...]

    def branch(y_ref, w_ref, g_ref, k):
        gate = jax.nn.sigmoid(g_ref[...].astype(F32) + bg[:, k * d:(k + 1) * d])
        return gate * jnp.dot(y_ref[...], w_ref[0], preferred_element_type=F32)

    merged = branch(ya_ref, wa_ref, ga_ref, 0) + branch(ys_ref, ws_ref, gs_ref, 1) \
        + branch(ym_ref, wm_ref, gm_ref, 2)
    o_ref[...] = x_ref[...] + jnp.dot(merged.astype(BF16), wo_ref[0], preferred_element_type=F32)


def _merge(x2, ya, ys, ym, gates, b_gate, w_a, w_s, w_m, w_o, layer, tm):
    m, d = x2.shape
    tm = min(tm, m)
    row = lambda w: pl.BlockSpec((tm, w), lambda i: (i, 0))
    gate = lambda k: pl.BlockSpec((tm, d), lambda i: (i, k))
    wres = lambda w: _resident((1,) + w.shape[1:], lambda i: (layer, 0, 0))
    return pl.pallas_call(
        _merge_body,
        grid=(m // tm,),
        in_specs=[row(d), row(ya.shape[1]), row(ys.shape[1]), row(ym.shape[1]),
                  gate(0), gate(1), gate(2), _resident((1, 3 * d), lambda i: (0, 0)),
                  wres(w_a), wres(w_s), wres(w_m), wres(w_o)],
        out_specs=row(d),
        out_shape=jax.ShapeDtypeStruct((m, d), F32),
        compiler_params=_cparams(("parallel",)),
        name="merge",
    )(x2, ya, ys, ym, gates, gates, gates, b_gate.reshape(1, -1), w_a, w_s, w_m, w_o)


def _xattn_body(x_ref, g_ref, wq_ref, kv_ref, gq_ref, gk_ref, wo_ref, o_ref, q_s, att_s):
    d = x_ref.shape[1]
    dh = d // MEM_HEADS
    scale = dh ** -0.5
    hn = _rms(x_ref[...], g_ref[...]).astype(BF16)
    q_s[...] = jnp.dot(hn, wq_ref[0], preferred_element_type=F32)
    for h in range(MEM_HEADS):
        cols = slice(h * dh, (h + 1) * dh)
        qn = _rms(q_s[:, cols], gq_ref[...]).astype(BF16)
        kn = _rms(kv_ref[0, :, cols].astype(F32), gk_ref[...]).astype(BF16)
        s = lax.dot_general(qn, kn, _NT, preferred_element_type=F32) * scale
        p = jnp.exp(s - jnp.max(s, axis=-1, keepdims=True))
        l = jnp.sum(p, axis=-1, keepdims=True)
        v = kv_ref[0, :, d + h * dh:d + (h + 1) * dh]
        att_s[:, cols] = (jnp.dot(p.astype(BF16), v, preferred_element_type=F32) / l).astype(BF16)
    o_ref[...] = x_ref[...] + jnp.dot(att_s[...], wo_ref[0], preferred_element_type=F32)


def _xattn(x2, g, w_q, kv3, g_q, g_k, w_o, layer, seq, tm):
    m, d = x2.shape
    tm = min(tm, seq)
    per_seq = seq // tm
    wres = lambda w: _resident((1,) + w.shape[1:], lambda i: (layer, 0, 0))
    head = pl.BlockSpec((1, d // MEM_HEADS), lambda i: (0, 0))
    return pl.pallas_call(
        _xattn_body,
        grid=(m // tm,),
        in_specs=[pl.BlockSpec((tm, d), lambda i: (i, 0)),
                  pl.BlockSpec((1, d), lambda i: (0, 0)),
                  wres(w_q),
                  pl.BlockSpec((1,) + kv3.shape[1:], lambda i: (i // per_seq, 0, 0)),
                  head, head, wres(w_o)],
        out_specs=pl.BlockSpec((tm, d), lambda i: (i, 0)),
        out_shape=jax.ShapeDtypeStruct((m, d), F32),
        scratch_shapes=[pltpu.VMEM((tm, d), F32), pltpu.VMEM((tm, d), BF16)],
        compiler_params=_cparams(("parallel",)),
        name="mem_attention",
    )(x2, g.reshape(1, d), w_q, kv3, g_q.reshape(1, -1), g_k.reshape(1, -1), w_o)


def _ffn_up_body(xa_ref, xb_ref, g_ref, wg_ref, wu_ref, a_ref, hn_ref, *, n_fill):
    rows = hn_ref.shape[0]
    s = pl.program_id(1)

    @pl.when(s < n_fill)
    def _():
        _fill_norm(xa_ref, xb_ref, g_ref, hn_ref, s)

    @pl.when(s >= n_fill)
    def _():
        wg, wu = wg_ref[0].astype(BF16), wu_ref[0].astype(BF16)
        for r in range(0, rows, DOT_ROWS):
            hn = hn_ref[r:r + DOT_ROWS, :]
            gate = jnp.dot(hn, wg, preferred_element_type=F32)
            up = jnp.dot(hn, wu, preferred_element_type=F32)
            a_ref[r:r + DOT_ROWS, :] = (gate * jax.nn.sigmoid(gate) * up).astype(a_ref.dtype)


def _ffn_up(x2, g, w_gu, layer):
    m, d = x2.shape
    f = w_gu.shape[2] // 2
    rows = min(ROW_GROUP, m)
    n_fill = rows // NORM_ROWS
    n_tiles = f // COL_TILE
    tile = lambda s: jnp.clip(s - n_fill, 0, n_tiles - 1)
    return pl.pallas_call(
        functools.partial(_ffn_up_body, n_fill=n_fill),
        grid=(m // rows, n_fill + n_tiles),
        in_specs=_fill_specs(d, n_fill) + [
                  pl.BlockSpec((1, d), lambda h, s: (0, 0)),
                  pl.BlockSpec((1, d, COL_TILE), lambda h, s: (layer, 0, tile(s))),
                  pl.BlockSpec((1, d, COL_TILE), lambda h, s: (layer, 0, n_tiles + tile(s)))],
        out_specs=pl.BlockSpec((rows, COL_TILE), lambda h, s: (h, tile(s))),
        out_shape=jax.ShapeDtypeStruct((m, f), BF16),
        scratch_shapes=[pltpu.VMEM((rows, d), BF16)],
        compiler_params=_cparams(("parallel", "arbitrary")),
        name="swiglu_up",
    )(x2, x2, g.reshape(1, d), w_gu, w_gu)


def _ffn_down_body(x_ref, a_ref, wa_ref, wb2_ref, o_ref, wb_ref, *, n_load, k_tile):
    s = pl.program_id(0)
    half = wa_ref.shape[2]

    @pl.when(s < n_load)
    def _():
        r0 = pl.multiple_of(s * k_tile, k_tile)
        wb_ref[pl.ds(r0, k_tile), :half] = wa_ref[0].astype(BF16)
        wb_ref[pl.ds(r0, k_tile), half:] = wb2_ref[0].astype(BF16)

    @pl.when(s >= n_load)
    def _():
        o_ref[...] = x_ref[...] + jnp.dot(a_ref[...], wb_ref[...], preferred_element_type=F32)


def _ffn_down(x2, act, w_down, layer, tm, k_tile):
    m, d = x2.shape
    f = act.shape[1]
    tm = min(tm, m)
    n_load = f // k_tile
    blk = lambda s: jnp.maximum(s - n_load, 0)
    return pl.pallas_call(
        functools.partial(_ffn_down_body, n_load=n_load, k_tile=k_tile),
        grid=(n_load + m // tm,),
        in_specs=[pl.BlockSpec((tm, d), lambda s: (blk(s), 0)),
                  pl.BlockSpec((tm, f), lambda s: (blk(s), 0)),
                  pl.BlockSpec((1, k_tile, d // 2), lambda s: (layer, jnp.minimum(s, n_load - 1), 0)),
                  pl.BlockSpec((1, k_tile, d // 2), lambda s: (layer, jnp.minimum(s, n_load - 1), 1))],
        out_specs=pl.BlockSpec((tm, d), lambda s: (blk(s), 0)),
        out_shape=jax.ShapeDtypeStruct((m, d), F32),
        scratch_shapes=[pltpu.VMEM((f, d), BF16)],
        compiler_params=_cparams(("arbitrary",)),
        name="swiglu_down",
    )(x2, act, w_down, w_down)


def kernel(x, mem, g_mem, norm_mix, w_in, b_gate, g_qa, g_ka, rel_bias, lam_re, lam_im, log_dt, b_re, b_im, c_re, c_im, d_skip, w_glu, b_glu, conv_w, conv_b, wq_m, wk_m, b_i, b_f, g_hm, skip_m, w_br_a, w_br_s, w_br_m, w_out, norm_x, w_xq, w_xkv, g_xq, g_xk, w_xo, norm_ffn, w_gu, w_down):
    b, seq, d = x.shape
    depth = w_in.shape[0]
    tokens = b * seq
    assert seq % ATT_QB == 0 and seq % ML_T == 0 and seq % SSM_T == 0
    assert tokens % min(ROW_GROUP, tokens) == 0

    w_glu_t = jnp.swapaxes(w_glu, 1, 2).astype(BF16)
    wq_b, wk_b = wq_m.astype(BF16), wk_m.astype(BF16)
    wa_b, ws_b, wm_b, wo_b = (w.astype(BF16) for w in (w_br_a, w_br_s, w_br_m, w_out))
    wxq_b, wxo_b = w_xq.astype(BF16), w_xo.astype(BF16)
    w_in_t = jnp.swapaxes(w_in, 1, 2)
    s5_params = _s5_params(lam_re, lam_im, log_dt, b_re, b_im, c_re, c_im)
    att_bias = _att_bias(rel_bias.reshape(depth * ATT_HEADS, -1))
    bif_all = jnp.stack([b_i, b_f], axis=-1).reshape(depth, ML_HEADS, 1, 2)

    x2 = x.reshape(tokens, d)
    mem2 = mem.reshape(b * mem.shape[1], d)
    for i in range(depth):
        proj, u3, gates, ift = _in_proj(x2, norm_mix[i], w_in_t, i)
        proj3 = proj.reshape(b, seq, -1)

        ya = _chunk_attention(proj3, g_qa[i], g_ka[i], att_bias, i)

        yt3 = _s5_scan(u3, s5_params, i, cpb=seq // SSM_T)
        ys = _s5_post(yt3, u3, d_skip[i], w_glu_t, i, b_glu[i], chunks=8)

        ym = _mlstm(proj3, ift, bif_all[i], conv_w[i], conv_b[i], wq_b, wk_b, i, g_hm[i], skip_m[i],
                    3 * ATT_WIDTH)

        x2 = _merge(x2, ya.reshape(tokens, -1), ys, ym.reshape(tokens, -1), gates, b_gate[i],
                    wa_b, ws_b, wm_b, wo_b, i, tm=256)

        kv = _norm_proj(mem2, g_mem, w_xkv, i, tm=1024, tn=512)
        x2 = _xattn(x2, norm_x[i], wxq_b, kv.reshape(b, mem.shape[1], 2 * d), g_xq[i], g_xk[i],
                    wxo_b, i, seq, tm=512)

        act = _ffn_up(x2, norm_ffn[i], w_gu, i)
        x2 = _ffn_down(x2, act, w_down, i, tm=256, k_tile=256)
    return x2.reshape(b, seq, d)
```

```python
import functools

import jax
import jax.numpy as jnp
from jax import lax
from jax.experimental import pallas as pl
from jax.experimental.pallas import tpu as pltpu

F32 = jnp.float32
BF16 = jnp.bfloat16

EPS = 1e-6
LANES = 128
BF16_ROWS = 16
ATT_CHUNK = 64
ATT_HEADS = 8
ATT_HEAD_DIM = 128
ATT_WIDTH = ATT_HEADS * ATT_HEAD_DIM
ATT_LEFT = 8
REL_CLIP = 256
ATT_QB = 256
ATT_PAD = ATT_LEFT * ATT_CHUNK
ATT_KW = ATT_QB + ATT_PAD
ATT_EXT = 1024
MASK_NEG = -1e30
SSM_GROUP = 16
SSM_GROUPS = 48
SSM_WIDTH = SSM_GROUP * SSM_GROUPS
SSM_STATE = 64
SSM_T = LANES
SSM_SUB = 16
SSM_GROUPS_PER_STEP = 8
ML_HEADS = 4
ML_HEAD_DIM = 256
ML_WIDTH = ML_HEADS * ML_HEAD_DIM
ML_CONV = 4
ML_T = 256
ML_NEG = -1e30
MEM_HEADS = 4
ROW_GROUP = 4096
NORM_ROWS = 256
COL_TILE = 256
DOT_ROWS = 1024

VMEM_LIMIT = 56 * 1024 * 1024

_NT = (((1,), (1,)), ((), ()))
_TN = (((0,), (0,)), ((), ()))


def _cparams(sem):
    return pltpu.CompilerParams(dimension_semantics=sem, vmem_limit_bytes=VMEM_LIMIT)


def _rms(x, g):
    return x * lax.rsqrt(jnp.mean(x * x, axis=-1, keepdims=True) + EPS) * g


def _resident(shape, index_map):
    return pl.BlockSpec(shape, index_map, pipeline_mode=pl.Buffered(1))


def _fill_specs(d, n_fill):
    row_blk = lambda h, s: h * n_fill + jnp.minimum(s, n_fill - 1)
    return [pl.BlockSpec((NORM_ROWS, d // 2), lambda h, s, c=c: (row_blk(h, s), c)) for c in range(2)]


def _fill_norm(xa_ref, xb_ref, g_ref, hn_ref, step):
    r0 = pl.multiple_of(step * NORM_ROWS, NORM_ROWS)
    x = jnp.concatenate([xa_ref[...], xb_ref[...]], axis=1)
    hn_ref[pl.ds(r0, NORM_ROWS), :] = _rms(x, g_ref[...]).astype(BF16)


def _in_proj_body(xa_ref, xb_ref, g_ref, wm_ref, wg_ref, we_ref, p_ref, u_ref, gt_ref, if_ref, hn_ref,
                  *, n_fill, n_att, n_ssm, n_ml, if_cols):
    rows = hn_ref.shape[0]
    s = pl.program_id(1)
    t = s - n_fill
    t_gate = t - (n_att + n_ssm + n_ml)

    @pl.when(s < n_fill)
    def _():
        _fill_norm(xa_ref, xb_ref, g_ref, hn_ref, s)

    def project(wt, o_ref):
        for r in range(0, rows, DOT_ROWS):
            o_ref[r:r + DOT_ROWS, :] = lax.dot_general(
                hn_ref[r:r + DOT_ROWS, :], wt, _NT, preferred_element_type=F32).astype(o_ref.dtype)

    in_ssm = (t >= n_att) & (t < n_att + n_ssm)

    @pl.when((t >= 0) & (t_gate < 0) & jnp.logical_not(in_ssm))
    def _():
        project(wm_ref[0].astype(BF16), p_ref)

    @pl.when(in_ssm)
    def _():
        wt = jnp.concatenate([wm_ref[0], wg_ref[0][:BF16_ROWS]], axis=0).astype(BF16)
        for r in range(0, rows, DOT_ROWS):
            ut = lax.dot_general(wt, hn_ref[r:r + DOT_ROWS, :], _NT, preferred_element_type=F32)
            u_ref[:, r // SSM_T:(r + DOT_ROWS) // SSM_T, :] = pltpu.einshape(
                "h(cl)->hcl", ut[:COL_TILE, :], l=SSM_T)
            if_ref[:, r:r + DOT_ROWS] = ut[COL_TILE:, :]

    @pl.when(t_gate >= 0)
    def _():
        wt = jnp.concatenate([wg_ref[0][if_cols:], we_ref[0]], axis=0)
        project(wt.astype(BF16), gt_ref)


def _in_proj(x2, g, w_in_t, layer):
    m, d = x2.shape
    rows = min(ROW_GROUP, m)
    n_fill = rows // NORM_ROWS
    n_att = 3 * ATT_WIDTH // COL_TILE
    n_ssm = SSM_WIDTH // COL_TILE
    n_ml = 3 * ML_WIDTH // COL_TILE
    n_main = n_att + n_ssm + n_ml
    if_cols = 2 * ML_HEADS
    gate_w = w_in_t.shape[1] - n_main * COL_TILE - if_cols
    n_gate = gate_w // COL_TILE
    assert gate_w % COL_TILE == 0 and if_cols <= BF16_ROWS and COL_TILE % if_cols == 0
    ext0 = (n_main + 1) * COL_TILE // if_cols
    ext_step = COL_TILE // if_cols

    def t_of(s):
        return s - n_fill

    def p_col(s):
        t = t_of(s)
        return jnp.where(t < n_att, jnp.clip(t, 0, n_att - 1), jnp.clip(t - n_ssm, n_att, n_att + n_ml - 1))

    def g_idx(s):
        return jnp.clip(t_of(s) - n_main, 0, n_gate - 1)

    body = functools.partial(_in_proj_body, n_fill=n_fill, n_att=n_att, n_ssm=n_ssm, n_ml=n_ml,
                             if_cols=if_cols)
    return pl.pallas_call(
        body,
        grid=(m // rows, n_fill + n_main + n_gate),
        in_specs=_fill_specs(d, n_fill) + [
                  pl.BlockSpec((1, d), lambda h, s: (0, 0)),
                  pl.BlockSpec((1, COL_TILE, d), lambda h, s: (layer, jnp.clip(t_of(s), 0, n_main - 1), 0)),
                  pl.BlockSpec((1, COL_TILE, d), lambda h, s: (layer, n_main + g_idx(s), 0)),
                  pl.BlockSpec((1, if_cols, d), lambda h, s: (layer, ext0 + ext_step * g_idx(s), 0))],
        out_specs=[pl.BlockSpec((rows, COL_TILE), lambda h, s: (h, p_col(s))),
                   pl.BlockSpec((COL_TILE, rows // SSM_T, SSM_T),
                                lambda h, s: (jnp.clip(t_of(s) - n_att, 0, n_ssm - 1), h, 0)),
                   pl.BlockSpec((rows, COL_TILE), lambda h, s: (h, g_idx(s))),
                   pl.BlockSpec((BF16_ROWS, rows), lambda h, s: (0, h))],
        out_shape=[jax.ShapeDtypeStruct((m, (n_att + n_ml) * COL_TILE), BF16),
                   jax.ShapeDtypeStruct((SSM_WIDTH, m // SSM_T, SSM_T), F32),
                   jax.ShapeDtypeStruct((m, gate_w), BF16),
                   jax.ShapeDtypeStruct((BF16_ROWS, m), F32)],
        scratch_shapes=[pltpu.VMEM((rows, d), BF16)],
        compiler_params=_cparams(("parallel", "arbitrary")),
        name="in_proj",
    )(x2, x2, g.reshape(1, d), w_in_t, w_in_t, w_in_t)


def _norm_proj_body(x_ref, g_ref, w_ref, o_ref, hn_ref):
    @pl.when(pl.program_id(1) == 0)
    def _():
        hn_ref[...] = _rms(x_ref[...], g_ref[...]).astype(BF16)

    o_ref[...] = jnp.dot(hn_ref[...], w_ref[0].astype(BF16),
                         preferred_element_type=F32).astype(o_ref.dtype)


def _norm_proj(x2, g, w, layer, tm, tn):
    m, d = x2.shape
    n = w.shape[2]
    tm = min(tm, m)
    return pl.pallas_call(
        _norm_proj_body,
        grid=(m // tm, n // tn),
        in_specs=[pl.BlockSpec((tm, d), lambda i, j: (i, 0)),
                  pl.BlockSpec((1, d), lambda i, j: (0, 0)),
                  pl.BlockSpec((1, d, tn), lambda i, j: (layer, 0, j))],
        out_specs=pl.BlockSpec((tm, tn), lambda i, j: (i, j)),
        out_shape=jax.ShapeDtypeStruct((m, n), BF16),
        scratch_shapes=[pltpu.VMEM((tm, d), BF16)],
        compiler_params=_cparams(("parallel", "arbitrary")),
        name="norm_proj",
    )(x2, g.reshape(1, d), w)


def _att_bias_body(ext_ref, o_ref):
    ext = ext_ref[0]
    base = pltpu.roll(ext, ATT_EXT - (ATT_QB - 1), axis=1)
    slab = jnp.broadcast_to(base, (ATT_QB, ATT_EXT))
    tab = pltpu.roll(slab, 0, 1, stride=1, stride_axis=0)[:, :ATT_KW]
    qc = lax.broadcasted_iota(jnp.int32, (ATT_QB, ATT_KW), 0) // ATT_CHUNK
    kc = lax.broadcasted_iota(jnp.int32, (ATT_QB, ATT_KW), 1) // ATT_CHUNK
    o_ref[0] = jnp.where((kc >= qc) & (kc <= qc + ATT_LEFT), tab, MASK_NEG)


def _att_bias(rel_bias):
    h = rel_bias.shape[0]
    lead = ATT_QB - 1 + ATT_PAD - REL_CLIP
    tail = ATT_EXT - lead - rel_bias.shape[1]
    ext = jnp.concatenate([jnp.broadcast_to(rel_bias[:, :1], (h, lead)), rel_bias,
                           jnp.broadcast_to(rel_bias[:, -1:], (h, tail))], axis=1)
    return pl.pallas_call(
        _att_bias_body,
        grid=(h,),
        in_specs=[pl.BlockSpec((1, 1, ATT_EXT), lambda i: (i, 0, 0))],
        out_specs=pl.BlockSpec((1, ATT_QB, ATT_KW), lambda i: (i, 0, 0)),
        out_shape=jax.ShapeDtypeStruct((h, ATT_QB, ATT_KW), F32),
        compiler_params=_cparams(("parallel",)),
        name="att_bias",
    )(ext.reshape(h, 1, ATT_EXT))


def _attn_body(q_ref, k_ref, v_ref, gq_ref, gk_ref, bias_ref, o_ref, kpad, vpad, *, seq):
    kpad[0:ATT_PAD, :] = jnp.zeros((ATT_PAD, ATT_HEAD_DIM), BF16)
    vpad[0:ATT_PAD, :] = jnp.zeros((ATT_PAD, ATT_HEAD_DIM), BF16)
    kpad[ATT_PAD:, :] = _rms(k_ref[0].astype(F32), gk_ref[...]).astype(BF16)
    vpad[ATT_PAD:, :] = v_ref[0]
    bias = bias_ref[0]
    scale = ATT_HEAD_DIM ** -0.5
    for qb in range(seq // ATT_QB):
        r0 = qb * ATT_QB
        qn = _rms(q_ref[0, r0:r0 + ATT_QB, :].astype(F32), gq_ref[...]).astype(BF16)
        kw = kpad[r0:r0 + ATT_KW, :]
        s = lax.dot_general(qn, kw, _NT, preferred_element_type=F32) * scale + bias
        if r0 < ATT_PAD:
            col = lax.broadcasted_iota(jnp.int32, (ATT_QB, ATT_KW), 1)
            s = jnp.where(col + r0 >= ATT_PAD, s, MASK_NEG)
        m = jnp.max(s, axis=-1, keepdims=True)
        p = jnp.exp(s - m)
        l = jnp.sum(p, axis=-1, keepdims=True)
        o = jnp.dot(p.astype(BF16), vpad[r0:r0 + ATT_KW, :], preferred_element_type=F32)
        o_ref[0, r0:r0 + ATT_QB, :] = (o / l).astype(o_ref.dtype)


def _chunk_attention(proj3, gq, gk, bias, layer):
    b, seq, _ = proj3.shape
    bias0 = layer * ATT_HEADS
    blk = (1, seq, ATT_HEAD_DIM)
    return pl.pallas_call(
        functools.partial(_attn_body, seq=seq),
        grid=(b, ATT_HEADS),
        in_specs=[pl.BlockSpec(blk, lambda i, h: (i, 0, h)),
                  pl.BlockSpec(blk, lambda i, h: (i, 0, ATT_HEADS + h)),
                  pl.BlockSpec(blk, lambda i, h: (i, 0, 2 * ATT_HEADS + h)),
                  pl.BlockSpec((1, ATT_HEAD_DIM), lambda i, h: (0, 0)),
                  pl.BlockSpec((1, ATT_HEAD_DIM), lambda i, h: (0, 0)),
                  pl.BlockSpec((1, ATT_QB, ATT_KW), lambda i, h: (bias0 + h, 0, 0))],
        out_specs=pl.BlockSpec(blk, lambda i, h: (i, 0, h)),
        out_shape=jax.ShapeDtypeStruct((b, seq, ATT_WIDTH), BF16),
        scratch_shapes=[pltpu.VMEM((seq + ATT_PAD, ATT_HEAD_DIM), BF16),
                        pltpu.VMEM((seq + ATT_PAD, ATT_HEAD_DIM), BF16)],
        compiler_params=_cparams(("parallel", "parallel")),
        name="chunk_attention",
    )(proj3, proj3, proj3, gq.reshape(1, -1), gk.reshape(1, -1), bias)


def _cmul(xr, xi, yr, yi):
    return xr * yr - xi * yi, xr * yi + xi * yr


def _s5_perm_tables(pmat, pmat_t):
    width = SSM_GROUP * SSM_T
    blk = SSM_GROUP * SSM_SUB
    sub_bits = SSM_SUB.bit_length() - 1
    t_bits = SSM_T.bit_length() - 1
    blk_bits = blk.bit_length() - 1
    for jb in range(width // blk):
        row = lax.broadcasted_iota(jnp.int32, (width, blk), 0)
        col = lax.broadcasted_iota(jnp.int32, (width, blk), 1) + jb * blk
        s0 = col & (SSM_SUB - 1)
        src = (((col & (blk - 1)) >> sub_bits) << t_bits) + ((col >> blk_bits) << sub_bits) + s0
        pmat[:, jb * blk:(jb + 1) * blk] = jnp.where(row == src, 1.0, 0.0).astype(BF16)
        src_t = (((col & (SSM_T - 1)) >> sub_bits) << blk_bits) + ((col >> t_bits) << sub_bits) + s0
        pmat_t[:, jb * blk:(jb + 1) * blk] = jnp.where(row == src_t, 1.0, 0.0).astype(BF16)


def _s5_group(gi, lr2_ref, li2_ref, ldt_ref, brt_ref, bit_ref, crx_ref, cix_ref,
              xp, yp, *, cpb, nch):
    t_len, sub, n_ch, n_st = SSM_T, SSM_SUB, SSM_GROUP, SSM_STATE
    n_sub = t_len // sub
    dt = jnp.exp(ldt_ref[gi])
    lr2, li2 = lr2_ref[gi], li2_ref[gi]
    first_half = lax.broadcasted_iota(jnp.int32, (1, 2 * n_st), 1) < n_st

    mag = jnp.exp(lr2 * dt)
    ar, ai = mag * jnp.cos(li2 * dt), mag * jnp.sin(li2 * dt)
    den = lr2 * lr2 + li2 * li2
    nr, ni = ar - 1.0, ai
    zr, zi = (nr * lr2 + ni * li2) / den, (ni * lr2 - nr * li2) / den
    bbr = zr * brt_ref[gi] - zi * bit_ref[gi]
    bbi = zr * bit_ref[gi] + zi * brt_ref[gi]

    sub_bits = sub.bit_length() - 1
    t_bits = t_len.bit_length() - 1
    sq = [(ar, ai)]
    while (1 << (len(sq) - 1)) * 2 < t_len * cpb:
        sq.append(_cmul(*sq[-1], *sq[-1]))

    def power(e, base):
        pr = pi = None
        for k in range(sub_bits):
            bit = ((e >> k) & 1) == 1
            fr, fi = jnp.where(bit, base[k][0], 1.0), jnp.where(bit, base[k][1], 0.0)
            pr, pi = (fr, fi) if pr is None else _cmul(pr, pi, fr, fi)
        return pr, pi

    def rotate(x, a):
        return x * a[0] + pltpu.roll(x, n_st, axis=1) * jnp.where(first_half, -a[1], a[1])

    e_row = lax.broadcasted_iota(jnp.int32, (sub, 2 * n_st), 0)
    qr, qi = power(sub - 1 - e_row, sq)
    f_r = jnp.where(first_half, bbr, bbi)
    f_i = jnp.where(first_half, -bbi, bbr)
    fmat = jnp.concatenate([(qr * f_r[mch:mch + 1, :] + qi * f_i[mch:mch + 1, :]).astype(BF16)
                            for mch in range(n_ch)], axis=0)

    sq_rows = jnp.concatenate([v for k in range(sub_bits) for v in sq[k]], axis=0)
    sq_cols = sq_rows.T
    base_c = [(sq_cols[:, 2 * k:2 * k + 1], sq_cols[:, 2 * k + 1:2 * k + 2]) for k in range(sub_bits)]
    tau = lax.broadcasted_iota(jnp.int32, (2 * n_st, n_ch * sub), 1) & (sub - 1)
    p0r, p0i = power(tau, base_c)
    p1r, p1i = _cmul(p0r, p0i, *base_c[0])
    top = lax.broadcasted_iota(jnp.int32, (2 * n_st, 1), 0) < n_st
    crx, cix = crx_ref[gi], cix_ref[gi]

    def c_times(pr, pi):
        return crx * jnp.where(top, pr, -pi) + cix * jnp.where(top, -pi, -pr)

    emat = c_times(p1r, p1i).astype(BF16)

    kflat = jnp.dot(f_r, c_times(p0r, p0i), preferred_element_type=F32, precision=lax.Precision.HIGHEST)
    lane_t = lax.broadcasted_iota(jnp.int32, (sub, n_ch * sub), 1) & (sub - 1)
    causal = lane_t >= lax.broadcasted_iota(jnp.int32, (sub, n_ch * sub), 0)
    bdiag = []
    for mch in range(n_ch):
        tz = pltpu.roll(jnp.broadcast_to(kflat[mch:mch + 1, :], (sub, n_ch * sub)), 0, 1,
                        stride=1, stride_axis=0)
        bdiag.append(jnp.where(causal, tz, 0.0).astype(BF16))
    bdiag = jnp.concatenate(bdiag, axis=0)

    rows = pl.ds(pl.multiple_of(gi * nch, nch), nch)
    xg = xp[rows, :]
    blk = n_ch * sub
    x2 = jnp.concatenate([xg[:, j * blk:(j + 1) * blk] for j in range(n_sub)], axis=0)
    s2 = jnp.dot(x2, fmat, preferred_element_type=F32)
    a_sub = sq[sub_bits]

    def run(state):
        entering = []
        for j in range(n_sub):
            entering.append(state)
            state = rotate(state, a_sub) + s2[j * nch:(j + 1) * nch]
        return entering, state

    _, s_loc = run(jnp.zeros((nch, 2 * n_st), F32))

    cidx = lax.broadcasted_iota(jnp.int32, (nch, 1), 0) % cpb
    xin = jnp.where(cidx >= 1, pltpu.roll(s_loc, 1, axis=0), 0.0)
    d = 1
    while d < cpb:
        sh = pltpu.roll(xin, d, axis=0)
        xin = xin + jnp.where(cidx >= d, rotate(sh, sq[t_bits + d.bit_length() - 1]), 0.0)
        d *= 2

    entering, _ = run(xin)
    xin2 = jnp.concatenate(entering, axis=0).astype(BF16)
    y2 = (jnp.dot(x2, bdiag, preferred_element_type=F32)
          + jnp.dot(xin2, emat, preferred_element_type=F32))
    yp[rows, :] = jnp.concatenate([y2[j * nch:(j + 1) * nch] for j in range(n_sub)], axis=1).astype(BF16)


def _s5_body(u_ref, lr2_ref, li2_ref, ldt_ref, brt_ref, bit_ref, crx_ref, cix_ref, yt_ref,
             pmat, pmat_t, xp, yp, *, cpb):
    n_ch, t_len = SSM_GROUP, SSM_T
    nch = u_ref.shape[1]
    gp = u_ref.shape[0] // n_ch

    @pl.when(pl.program_id(0) == 0)
    def _():
        _s5_perm_tables(pmat, pmat_t)

    xall = jnp.concatenate(
        [jnp.concatenate([u_ref[g * n_ch + mch].astype(BF16) for mch in range(n_ch)], axis=1)
         for g in range(gp)], axis=0)
    xp[...] = jnp.dot(xall, pmat[...], preferred_element_type=F32).astype(BF16)

    def group(gi, carry):
        _s5_group(gi, lr2_ref, li2_ref, ldt_ref, brt_ref, bit_ref, crx_ref, cix_ref,
                  xp, yp, cpb=cpb, nch=nch)
        return carry

    lax.fori_loop(0, gp, group, 0, unroll=4)

    yall = jnp.dot(yp[...], pmat_t[...], preferred_element_type=F32)
    for g in range(gp):
        for nn in range(n_ch):
            yt_ref[g * n_ch + nn] = yall[g * nch:(g + 1) * nch, nn * t_len:(nn + 1) * t_len]


def _s5_params(lam_re, lam_im, log_dt, b_re, b_im, c_re, c_im):
    lg, p = lam_re.shape[0] * lam_re.shape[1], SSM_STATE
    flat = lambda a: a.reshape((lg,) + a.shape[2:])
    dup = lambda a, axis: jnp.concatenate([a, a], axis=axis)
    lr2 = dup(flat(lam_re), 1).reshape(lg, 1, 2 * p)
    li2 = dup(flat(lam_im), 1).reshape(lg, 1, 2 * p)
    ldt = log_dt.reshape(lg, 1, 1)
    brt = dup(jnp.swapaxes(flat(b_re), 1, 2), 2)
    bit = dup(jnp.swapaxes(flat(b_im), 1, 2), 2)
    crx = jnp.repeat(dup(jnp.swapaxes(flat(c_re), 1, 2), 1), SSM_SUB, axis=2)
    cix = jnp.repeat(dup(jnp.swapaxes(flat(c_im), 1, 2), 1), SSM_SUB, axis=2)
    return lr2, li2, ldt, brt, bit, crx, cix


def _s5_scan(u3, params, layer, *, cpb):
    g, p, n, gp = SSM_GROUPS, SSM_STATE, SSM_GROUP, SSM_GROUPS_PER_STEP
    nch = u3.shape[1]
    width = n * SSM_T
    blk = n * SSM_SUB
    step0 = layer * (g // gp)
    spec = lambda *s: pl.BlockSpec((gp,) + s, lambda i: (step0 + i, 0, 0))
    return pl.pallas_call(
        functools.partial(_s5_body, cpb=cpb),
        grid=(g // gp,),
        in_specs=[pl.BlockSpec((gp * n, nch, SSM_T), lambda i: (i, 0, 0)),
                  spec(1, 2 * p), spec(1, 2 * p), spec(1, 1),
                  spec(n, 2 * p), spec(n, 2 * p), spec(2 * p, blk), spec(2 * p, blk)],
        out_specs=pl.BlockSpec((gp * n, nch, SSM_T), lambda i: (i, 0, 0)),
        out_shape=jax.ShapeDtypeStruct((SSM_WIDTH, nch, SSM_T), F32),
        scratch_shapes=[pltpu.VMEM((width, width), BF16),
                        pltpu.VMEM((width, width), BF16),
                        pltpu.VMEM((gp * nch, width), BF16),
                        pltpu.VMEM((gp * nch, width), BF16)],
        compiler_params=_cparams(("arbitrary",)),
        name="s5_scan",
    )(u3, *params)


def _s5_post_body(yt_ref, u_ref, dsk_ref, wgt_ref, bg_ref, o_ref):
    yt = pltpu.einshape("hcl->h(cl)", yt_ref[...])
    ut = pltpu.einshape("hcl->h(cl)", u_ref[...])
    y = jax.nn.gelu(yt + dsk_ref[...] * ut)
    z = jnp.dot(wgt_ref[0], y.astype(BF16), preferred_element_type=F32) + bg_ref[...]
    o_ref[...] = (y * jax.nn.sigmoid(z)).T.astype(o_ref.dtype)


def _s5_post(yt3, u3, d_skip, w_glu_t, layer, b_glu, chunks):
    w, nch, t_len = yt3.shape
    chunks = min(chunks, nch)
    return pl.pallas_call(
        _s5_post_body,
        grid=(nch // chunks,),
        in_specs=[pl.BlockSpec((w, chunks, t_len), lambda i: (0, i, 0)),
                  pl.BlockSpec((w, chunks, t_len), lambda i: (0, i, 0)),
                  pl.BlockSpec((w, 1), lambda i: (0, 0)),
                  pl.BlockSpec((1, w, w), lambda i: (layer, 0, 0)),
                  pl.BlockSpec((w, 1), lambda i: (0, 0))],
        out_specs=pl.BlockSpec((chunks * t_len, w), lambda i: (i, 0)),
        out_shape=jax.ShapeDtypeStruct((nch * t_len, w), BF16),
        compiler_params=_cparams(("parallel",)),
        name="s5_post",
    )(yt3, u3, d_skip.reshape(w, 1), w_glu_t, b_glu.reshape(w, 1))


def _mlstm_body(xm_ref, vm_ref, om_ref, if_ref, bif_ref, cw_ref, cb_ref, wq_ref, wk_ref,
                gh_ref, sk_ref, o_ref, xc_s, q_s, k_s, cmat, nvec, mrun, ifr_s, ifc_s, *, seq):
    t_len, dh = ML_T, ML_HEAD_DIM
    head = pl.program_id(1)
    xm = xm_ref[0].astype(F32)
    rowi = lax.broadcasted_iota(jnp.int32, (seq, 1), 0)
    cw = cw_ref[...]
    acc = cw[ML_CONV - 1:ML_CONV, :] * xm + cb_ref[...]
    for j in range(1, ML_CONV):
        shifted = jnp.where(rowi >= j, pltpu.roll(xm, j, axis=0), 0.0)
        acc = acc + cw[ML_CONV - 1 - j:ML_CONV - j, :] * shifted
    xc = acc * jax.nn.sigmoid(acc)
    xc_s[...] = xc
    xcb = xc.astype(BF16)
    q_s[...] = jnp.dot(xcb, wq_ref[0, 0], preferred_element_type=F32).astype(BF16)
    k_s[...] = (jnp.dot(xcb, wk_ref[0, 0], preferred_element_type=F32) * (dh ** -0.5)).astype(BF16)

    cmat[...] = jnp.zeros_like(cmat)
    nvec[...] = jnp.zeros_like(nvec)
    mrun[...] = jnp.full_like(mrun, ML_NEG)
    b_i, b_f = bif_ref[0][:, 0:1], bif_ref[0][:, 1:2]
    ifr_s[...] = jnp.concatenate([if_ref[pl.ds(head, 1), :], if_ref[pl.ds(ML_HEADS + head, 1), :],
                                  jnp.zeros((6, seq), F32)], axis=0)
    ifc_s[...] = ifr_s[...].T
    iota_r = lax.broadcasted_iota(jnp.int32, (t_len, t_len), 0)
    iota_c = lax.broadcasted_iota(jnp.int32, (t_len, t_len), 1)
    tri = iota_c <= iota_r
    tri_t = iota_r <= iota_c

    def step(c, carry):
        r0 = pl.multiple_of(c * t_len, t_len)
        rows = pl.ds(r0, t_len)
        ifr = ifr_s[:, rows]
        ifc = ifc_s[rows, :]
        i_col, i_row = ifc[:, 0:1] + b_i, ifr[0:1, :] + b_i
        lf_col = jax.nn.log_sigmoid(ifc[:, 1:2] + b_f)
        lf_row = jax.nn.log_sigmoid(ifr[1:2, :] + b_f)
        bcum_col = jnp.sum(jnp.where(tri, lf_row, 0.0), axis=1, keepdims=True)
        bcum_row = jnp.sum(jnp.where(tri_t, lf_col, 0.0), axis=0, keepdims=True)
        b_last = jnp.sum(lf_row, axis=1, keepdims=True)
        m_prev = mrun[...]
        dmat = jnp.where(tri, bcum_col - bcum_row + i_row, -jnp.inf)
        inter = bcum_col + m_prev
        m_row = jnp.maximum(jnp.max(dmat, axis=1, keepdims=True), inter)
        w_intra = jnp.exp(dmat - m_row)
        w_inter = jnp.exp(inter - m_row)
        qq, kk, vv = q_s[rows, :], k_s[rows, :], vm_ref[0, rows, :]
        s = lax.dot_general(qq, kk, _NT, preferred_element_type=F32) * w_intra
        cm = cmat[...]
        num = (jnp.dot(s.astype(BF16), vv, preferred_element_type=F32)
               + w_inter * lax.dot_general(qq, cm.astype(BF16), _NT, preferred_element_type=F32))
        den = (jnp.sum(s, axis=1, keepdims=True)
               + w_inter * jnp.sum(qq.astype(F32) * nvec[...], axis=1, keepdims=True))
        h = num / jnp.maximum(jnp.abs(den), jnp.exp(-m_row))
        g_col = b_last - bcum_col + i_col
        m_new = jnp.maximum(b_last + m_prev, jnp.max(g_col, axis=0, keepdims=True))
        wg = jnp.exp(g_col - m_new)
        decay = jnp.exp(b_last + m_prev - m_new)
        vw = (vv.astype(F32) * wg).astype(BF16)
        cmat[...] = decay * cm + lax.dot_general(vw, kk, _TN, preferred_element_type=F32)
        nvec[...] = decay * nvec[...] + jnp.sum(wg * kk.astype(F32), axis=0, keepdims=True)
        mrun[...] = m_new
        hn = _rms(h, gh_ref[...]) + sk_ref[...] * xc_s[rows, :]
        o_ref[0, rows, :] = (jax.nn.sigmoid(om_ref[0, rows, :].astype(F32)) * hn).astype(o_ref.dtype)
        return carry

    lax.fori_loop(0, seq // t_len, step, 0, unroll=2)


def _mlstm(proj3, ift, bif, conv_w, conv_b, wq, wk, layer, g_h, skip, col0):
    b, seq, _ = proj3.shape
    dh = ML_HEAD_DIM
    c0 = col0 // dh
    blk = (1, seq, dh)
    vec = pl.BlockSpec((1, dh), lambda i, h: (0, h))
    wspec = pl.BlockSpec((1, 1, dh, dh), lambda i, h: (layer, h, 0, 0))
    return pl.pallas_call(
        functools.partial(_mlstm_body, seq=seq),
        grid=(b, ML_HEADS),
        in_specs=[pl.BlockSpec(blk, lambda i, h: (i, 0, c0 + h)),
                  pl.BlockSpec(blk, lambda i, h: (i, 0, c0 + ML_HEADS + h)),
                  pl.BlockSpec(blk, lambda i, h: (i, 0, c0 + 2 * ML_HEADS + h)),
                  pl.BlockSpec((ift.shape[0], seq), lambda i, h: (0, i)),
                  pl.BlockSpec((1, 1, 2), lambda i, h: (h, 0, 0)),
                  pl.BlockSpec((ML_CONV, dh), lambda i, h: (0, h)),
                  vec, wspec, wspec, vec, vec],
        out_specs=pl.BlockSpec(blk, lambda i, h: (i, 0, h)),
        out_shape=jax.ShapeDtypeStruct((b, seq, ML_WIDTH), BF16),
        scratch_shapes=[pltpu.VMEM((seq, dh), F32),
                        pltpu.VMEM((seq, dh), BF16),
                        pltpu.VMEM((seq, dh), BF16),
                        pltpu.VMEM((dh, dh), F32),
                        pltpu.VMEM((1, dh), F32),
                        pltpu.VMEM((1, 1), F32),
                        pltpu.VMEM((8, seq), F32),
                        pltpu.VMEM((seq, 8), F32)],
        compiler_params=_cparams(("parallel", "parallel")),
        name="mlstm",
    )(proj3, proj3, proj3, ift, bif, conv_w, conv_b.reshape(1, -1), wq, wk,
      g_h.reshape(1, -1), skip.reshape(1, -1))


def _merge_body(x_ref, ya_ref, ys_ref, ym_ref, ga_ref, gs_ref, gm_ref, bg_ref,
                wa_ref, ws_ref, wm_ref, wo_ref, o_ref):
    d = x_ref.shape[1]
    bg = bg_ref[...]

    def branch(y_ref, w_ref, g_ref, k):
        gate = jax.nn.sigmoid(g_ref[...].astype(F32) + bg[:, k * d:(k + 1) * d])
        return gate * jnp.dot(y_ref[...], w_ref[0], preferred_element_type=F32)

    merged = branch(ya_ref, wa_ref, ga_ref, 0) + branch(ys_ref, ws_ref, gs_ref, 1) \
        + branch(ym_ref, wm_ref, gm_ref, 2)
    o_ref[...] = x_ref[...] + jnp.dot(merged.astype(BF16), wo_ref[0], preferred_element_type=F32)


def _merge(x2, ya, ys, ym, gates, b_gate, w_a, w_s, w_m, w_o, layer, tm):
    m, d = x2.shape
    tm = min(tm, m)
    row = lambda w: pl.BlockSpec((tm, w), lambda i: (i, 0))
    gate = lambda k: pl.BlockSpec((tm, d), lambda i: (i, k))
    wres = lambda w: _resident((1,) + w.shape[1:], lambda i: (layer, 0, 0))
    return pl.pallas_call(
        _merge_body,
        grid=(m // tm,),
        in_specs=[row(d), row(ya.shape[1]), row(ys.shape[1]), row(ym.shape[1]),
                  gate(0), gate(1), gate(2), _resident((1, 3 * d), lambda i: (0, 0)),
                  wres(w_a), wres(w_s), wres(w_m), wres(w_o)],
        out_specs=row(d),
        out_shape=jax.ShapeDtypeStruct((m, d), F32),
        compiler_params=_cparams(("parallel",)),
        name="merge",
    )(x2, ya, ys, ym, gates, gates, gates, b_gate.reshape(1, -1), w_a, w_s, w_m, w_o)


def _xattn_body(x_ref, g_ref, wq_ref, kv_ref, gq_ref, gk_ref, wo_ref, o_ref, q_s, att_s):
    d = x_ref.shape[1]
    dh = d // MEM_HEADS
    scale = dh ** -0.5
    hn = _rms(x_ref[...], g_ref[...]).astype(BF16)
    q_s[...] = jnp.dot(hn, wq_ref[0], preferred_element_type=F32)
    for h in range(MEM_HEADS):
        cols = slice(h * dh, (h + 1) * dh)
        qn = _rms(q_s[:, cols], gq_ref[...]).astype(BF16)
        kn = _rms(kv_ref[0, :, cols].astype(F32), gk_ref[...]).astype(BF16)
        s = lax.dot_general(qn, kn, _NT, preferred_element_type=F32) * scale
        p = jnp.exp(s - jnp.max(s, axis=-1, keepdims=True))
        l = jnp.sum(p, axis=-1, keepdims=True)
        v = kv_ref[0, :, d + h * dh:d + (h + 1) * dh]
        att_s[:, cols] = (jnp.dot(p.astype(BF16), v, preferred_element_type=F32) / l).astype(BF16)
    o_ref[...] = x_ref[...] + jnp.dot(att_s[...], wo_ref[0], preferred_element_type=F32)


def _xattn(x2, g, w_q, kv3, g_q, g_k, w_o, layer, seq, tm):
    m, d = x2.shape
    tm = min(tm, seq)
    per_seq = seq // tm
    wres = lambda w: _resident((1,) + w.shape[1:], lambda i: (layer, 0, 0))
    head = pl.BlockSpec((1, d // MEM_HEADS), lambda i: (0, 0))
    return pl.pallas_call(
        _xattn_body,
        grid=(m // tm,),
        in_specs=[pl.BlockSpec((tm, d), lambda i: (i, 0)),
                  pl.BlockSpec((1, d), lambda i: (0, 0)),
                  wres(w_q),
                  pl.BlockSpec((1,) + kv3.shape[1:], lambda i: (i // per_seq, 0, 0)),
                  head, head, wres(w_o)],
        out_specs=pl.BlockSpec((tm, d), lambda i: (i, 0)),
        out_shape=jax.ShapeDtypeStruct((m, d), F32),
        scratch_shapes=[pltpu.VMEM((tm, d), F32), pltpu.VMEM((tm, d), BF16)],
        compiler_params=_cparams(("parallel",)),
        name="mem_attention",
    )(x2, g.reshape(1, d), w_q, kv3, g_q.reshape(1, -1), g_k.reshape(1, -1), w_o)


def _ffn_up_body(xa_ref, xb_ref, g_ref, wg_ref, wu_ref, a_ref, hn_ref, *, n_fill):
    rows = hn_ref.shape[0]
    s = pl.program_id(1)

    @pl.when(s < n_fill)
    def _():
        _fill_norm(xa_ref, xb_ref, g_ref, hn_ref, s)

    @pl.when(s >= n_fill)
    def _():
        wg, wu = wg_ref[0].astype(BF16), wu_ref[0].astype(BF16)
        for r in range(0, rows, DOT_ROWS):
            hn = hn_ref[r:r + DOT_ROWS, :]
            gate = jnp.dot(hn, wg, preferred_element_type=F32)
            up = jnp.dot(hn, wu, preferred_element_type=F32)
            a_ref[r:r + DOT_ROWS, :] = (gate * jax.nn.sigmoid(gate) * up).astype(a_ref.dtype)


def _ffn_up(x2, g, w_gu, layer):
    m, d = x2.shape
    f = w_gu.shape[2] // 2
    rows = min(ROW_GROUP, m)
    n_fill = rows // NORM_ROWS
    n_tiles = f // COL_TILE
    tile = lambda s: jnp.clip(s - n_fill, 0, n_tiles - 1)
    return pl.pallas_call(
        functools.partial(_ffn_up_body, n_fill=n_fill),
        grid=(m // rows, n_fill + n_tiles),
        in_specs=_fill_specs(d, n_fill) + [
                  pl.BlockSpec((1, d), lambda h, s: (0, 0)),
                  pl.BlockSpec((1, d, COL_TILE), lambda h, s: (layer, 0, tile(s))),
                  pl.BlockSpec((1, d, COL_TILE), lambda h, s: (layer, 0, n_tiles + tile(s)))],
        out_specs=pl.BlockSpec((rows, COL_TILE), lambda h, s: (h, tile(s))),
        out_shape=jax.ShapeDtypeStruct((m, f), BF16),
        scratch_shapes=[pltpu.VMEM((rows, d), BF16)],
        compiler_params=_cparams(("parallel", "arbitrary")),
        name="swiglu_up",
    )(x2, x2, g.reshape(1, d), w_gu, w_gu)


def _ffn_down_body(x_ref, a_ref, wa_ref, wb2_ref, o_ref, wb_ref, *, n_load, k_tile):
    s = pl.program_id(0)
    half = wa_ref.shape[2]

    @pl.when(s < n_load)
    def _():
        r0 = pl.multiple_of(s * k_tile, k_tile)
        wb_ref[pl.ds(r0, k_tile), :half] = wa_ref[0].astype(BF16)
        wb_ref[pl.ds(r0, k_tile), half:] = wb2_ref[0].astype(BF16)

    @pl.when(s >= n_load)
    def _():
        o_ref[...] = x_ref[...] + jnp.dot(a_ref[...], wb_ref[...], preferred_element_type=F32)


def _ffn_down(x2, act, w_down, layer, tm, k_tile):
    m, d = x2.shape
    f = act.shape[1]
    tm = min(tm, m)
    n_load = f // k_tile
    blk = lambda s: jnp.maximum(s - n_load, 0)
    return pl.pallas_call(
        functools.partial(_ffn_down_body, n_load=n_load, k_tile=k_tile),
        grid=(n_load + m // tm,),
        in_specs=[pl.BlockSpec((tm, d), lambda s: (blk(s), 0)),
                  pl.BlockSpec((tm, f), lambda s: (blk(s), 0)),
                  pl.BlockSpec((1, k_tile, d // 2), lambda s: (layer, jnp.minimum(s, n_load - 1), 0)),
                  pl.BlockSpec((1, k_tile, d // 2), lambda s: (layer, jnp.minimum(s, n_load - 1), 1))],
        out_specs=pl.BlockSpec((tm, d), lambda s: (blk(s), 0)),
        out_shape=jax.ShapeDtypeStruct((m, d), F32),
        scratch_shapes=[pltpu.VMEM((f, d), BF16)],
        compiler_params=_cparams(("arbitrary",)),
        name="swiglu_down",
    )(x2, act, w_down, w_down)


def kernel(x, mem, g_mem, norm_mix, w_in, b_gate, g_qa, g_ka, rel_bias, lam_re, lam_im, log_dt, b_re, b_im, c_re, c_im, d_skip, w_glu, b_glu, conv_w, conv_b, wq_m, wk_m, b_i, b_f, g_hm, skip_m, w_br_a, w_br_s, w_br_m, w_out, norm_x, w_xq, w_xkv, g_xq, g_xk, w_xo, norm_ffn, w_gu, w_down):
    b, seq, d = x.shape
    depth = w_in.shape[0]
    tokens = b * seq
    assert seq % ATT_QB == 0 and seq % ML_T == 0 and seq % SSM_T == 0
    assert tokens % min(ROW_GROUP, tokens) == 0

    w_glu_t = jnp.swapaxes(w_glu, 1, 2).astype(BF16)
    wq_b, wk_b = wq_m.astype(BF16), wk_m.astype(BF16)
    wa_b, ws_b, wm_b, wo_b = (w.astype(BF16) for w in (w_br_a, w_br_s, w_br_m, w_out))
    wxq_b, wxo_b = w_xq.astype(BF16), w_xo.astype(BF16)
    w_in_t = jnp.swapaxes(w_in, 1, 2)
    s5_params = _s5_params(lam_re, lam_im, log_dt, b_re, b_im, c_re, c_im)
    att_bias = _att_bias(rel_bias.reshape(depth * ATT_HEADS, -1))
    bif_all = jnp.stack([b_i, b_f], axis=-1).reshape(depth, ML_HEADS, 1, 2)

    x2 = x.reshape(tokens, d)
    mem2 = mem.reshape(b * mem.shape[1], d)
    for i in range(depth):
        proj, u3, gates, ift = _in_proj(x2, norm_mix[i], w_in_t, i)
        proj3 = proj.reshape(b, seq, -1)

        ya = _chunk_attention(proj3, g_qa[i], g_ka[i], att_bias, i)

        yt3 = _s5_scan(u3, s5_params, i, cpb=seq // SSM_T)
        ys = _s5_post(yt3, u3, d_skip[i], w_glu_t, i, b_glu[i], chunks=8)

        ym = _mlstm(proj3, ift, bif_all[i], conv_w[i], conv_b[i], wq_b, wk_b, i, g_hm[i], skip_m[i],
                    3 * ATT_WIDTH)

        x2 = _merge(x2, ya.reshape(tokens, -1), ys, ym.reshape(tokens, -1), gates, b_gate[i],
                    wa_b, ws_b, wm_b, wo_b, i, tm=256)

        kv = _norm_proj(mem2, g_mem, w_xkv, i, tm=1024, tn=512)
        x2 = _xattn(x2, norm_x[i], wxq_b, kv.reshape(b, mem.shape[1], 2 * d), g_xq[i], g_xk[i],
                    wxo_b, i, seq, tm=512)

        act = _ffn_up(x2, norm_ffn[i], w_gu, i)
        x2 = _ffn_down(x2, act, w_down, i, tm=256, k_tile=256)
    return x2.reshape(b, seq, d)
```

```python
import functools

import jax
import jax.numpy as jnp
from jax import lax
from jax.experimental import pallas as pl
from jax.experimental.pallas import tpu as pltpu

F32 = jnp.float32
BF16 = jnp.bfloat16

EPS = 1e-6
LANES = 128
BF16_ROWS = 16
ATT_CHUNK = 64
ATT_HEADS = 8
ATT_HEAD_DIM = 128
ATT_WIDTH = ATT_HEADS * ATT_HEAD_DIM
ATT_LEFT = 8
REL_CLIP = 256
ATT_QB = 256
ATT_PAD = ATT_LEFT * ATT_CHUNK
ATT_KW = ATT_QB + ATT_PAD
ATT_EXT = 1024
MASK_NEG = -1e30
SSM_GROUP = 16
SSM_GROUPS = 48
SSM_WIDTH = SSM_GROUP * SSM_GROUPS
SSM_STATE = 64
SSM_T = LANES
SSM_SUB = 16
SSM_GROUPS_PER_STEP = 8
ML_HEADS = 4
ML_HEAD_DIM = 256
ML_WIDTH = ML_HEADS * ML_HEAD_DIM
ML_CONV = 4
ML_T = 256
ML_NEG = -1e30
MEM_HEADS = 4
ROW_GROUP = 4096
NORM_ROWS = 256
COL_TILE = 256
DOT_ROWS = 1024

VMEM_LIMIT = 56 * 1024 * 1024

_NT = (((1,), (1,)), ((), ()))
_TN = (((0,), (0,)), ((), ()))


def _cparams(sem):
    return pltpu.CompilerParams(dimension_semantics=sem, vmem_limit_bytes=VMEM_LIMIT)


def _rms(x, g):
    return x * lax.rsqrt(jnp.mean(x * x, axis=-1, keepdims=True) + EPS) * g


def _resident(shape, index_map):
    return pl.BlockSpec(shape, index_map, pipeline_mode=pl.Buffered(1))


def _fill_specs(d, n_fill):
    row_blk = lambda h, s: h * n_fill + jnp.minimum(s, n_fill - 1)
    return [pl.BlockSpec((NORM_ROWS, d // 2), lambda h, s, c=c: (row_blk(h, s), c)) for c in range(2)]


def _fill_norm(xa_ref, xb_ref, g_ref, hn_ref, step):
    r0 = pl.multiple_of(step * NORM_ROWS, NORM_ROWS)
    x = jnp.concatenate([xa_ref[...], xb_ref[...]], axis=1)
    if x.dtype != BF16:
        x = _rms(x, g_ref[...]).astype(BF16)
    hn_ref[pl.ds(r0, NORM_ROWS), :] = x


def _in_proj_body(xa_ref, xb_ref, g_ref, wm_ref, wg_ref, we_ref, p_ref, u_ref, gt_ref, if_ref, hn_ref,
                  *, n_fill, n_att, n_ssm, n_ml, if_cols):
    rows = hn_ref.shape[0]
    s = pl.program_id(1)
    t = s - n_fill
    t_gate = t - (n_att + n_ssm + n_ml)

    @pl.when(s < n_fill)
    def _():
        _fill_norm(xa_ref, xb_ref, g_ref, hn_ref, s)

    def project(wt, o_ref):
        for r in range(0, rows, DOT_ROWS):
            o_ref[r:r + DOT_ROWS, :] = lax.dot_general(
                hn_ref[r:r + DOT_ROWS, :], wt, _NT, preferred_element_type=F32).astype(o_ref.dtype)

    in_ssm = (t >= n_att) & (t < n_att + n_ssm)

    @pl.when((t >= 0) & (t_gate < 0) & jnp.logical_not(in_ssm))
    def _():
        project(wm_ref[0].astype(BF16), p_ref)

    @pl.when(in_ssm)
    def _():
        wt = jnp.concatenate([wm_ref[0], wg_ref[0][:BF16_ROWS]], axis=0).astype(BF16)
        for r in range(0, rows, DOT_ROWS):
            ut = lax.dot_general(wt, hn_ref[r:r + DOT_ROWS, :], _NT, preferred_element_type=F32)
            u_ref[:, r // SSM_T:(r + DOT_ROWS) // SSM_T, :] = ut[:COL_TILE, :].reshape(
                COL_TILE, DOT_ROWS // SSM_T, SSM_T)
            if_ref[:, r:r + DOT_ROWS] = ut[COL_TILE:, :]

    @pl.when(t_gate >= 0)
    def _():
        wt = jnp.concatenate([wg_ref[0][if_cols:], we_ref[0]], axis=0)
        project(wt.astype(BF16), gt_ref)


def _in_proj(x2, g, w_in_t, layer):
    m, d = x2.shape
    rows = min(ROW_GROUP, m)
    n_fill = rows // NORM_ROWS
    n_att = 3 * ATT_WIDTH // COL_TILE
    n_ssm = SSM_WIDTH // COL_TILE
    n_ml = 3 * ML_WIDTH // COL_TILE
    n_main = n_att + n_ssm + n_ml
    if_cols = 2 * ML_HEADS
    gate_w = w_in_t.shape[1] - n_main * COL_TILE - if_cols
    n_gate = gate_w // COL_TILE
    assert gate_w % COL_TILE == 0 and if_cols <= BF16_ROWS and COL_TILE % if_cols == 0
    ext0 = (n_main + 1) * COL_TILE // if_cols
    ext_step = COL_TILE // if_cols

    def t_of(s):
        return s - n_fill

    def p_col(s):
        t = t_of(s)
        return jnp.where(t < n_att, jnp.clip(t, 0, n_att - 1), jnp.clip(t - n_ssm, n_att, n_att + n_ml - 1))

    def g_idx(s):
        return jnp.clip(t_of(s) - n_main, 0, n_gate - 1)

    body = functools.partial(_in_proj_body, n_fill=n_fill, n_att=n_att, n_ssm=n_ssm, n_ml=n_ml,
                             if_cols=if_cols)
    return pl.pallas_call(
        body,
        grid=(m // rows, n_fill + n_main + n_gate),
        in_specs=_fill_specs(d, n_fill) + [
                  pl.BlockSpec((1, d), lambda h, s: (0, 0)),
                  pl.BlockSpec((1, COL_TILE, d), lambda h, s: (layer, jnp.clip(t_of(s), 0, n_main - 1), 0)),
                  pl.BlockSpec((1, COL_TILE, d), lambda h, s: (layer, n_main + g_idx(s), 0)),
                  pl.BlockSpec((1, if_cols, d), lambda h, s: (layer, ext0 + ext_step * g_idx(s), 0))],
        out_specs=[pl.BlockSpec((rows, COL_TILE), lambda h, s: (h, p_col(s))),
                   pl.BlockSpec((COL_TILE, rows // SSM_T, SSM_T),
                                lambda h, s: (jnp.clip(t_of(s) - n_att, 0, n_ssm - 1), h, 0)),
                   pl.BlockSpec((rows, COL_TILE), lambda h, s: (h, g_idx(s))),
                   pl.BlockSpec((BF16_ROWS, rows), lambda h, s: (0, h))],
        out_shape=[jax.ShapeDtypeStruct((m, (n_att + n_ml) * COL_TILE), BF16),
                   jax.ShapeDtypeStruct((SSM_WIDTH, m // SSM_T, SSM_T), F32),
                   jax.ShapeDtypeStruct((m, gate_w), BF16),
                   jax.ShapeDtypeStruct((BF16_ROWS, m), F32)],
        scratch_shapes=[pltpu.VMEM((rows, d), BF16)],
        compiler_params=_cparams(("parallel", "arbitrary")),
        name="in_proj",
    )(x2, x2, g.reshape(1, d), w_in_t, w_in_t, w_in_t)


def _norm_proj_body(x_ref, g_ref, w_ref, o_ref, hn_ref):
    @pl.when(pl.program_id(1) == 0)
    def _():
        hn_ref[...] = _rms(x_ref[...], g_ref[...]).astype(BF16)

    o_ref[...] = jnp.dot(hn_ref[...], w_ref[0].astype(BF16),
                         preferred_element_type=F32).astype(o_ref.dtype)


def _norm_proj(x2, g, w, layer, tm, tn):
    m, d = x2.shape
    n = w.shape[2]
    tm = min(tm, m)
    return pl.pallas_call(
        _norm_proj_body,
        grid=(m // tm, n // tn),
        in_specs=[pl.BlockSpec((tm, d), lambda i, j: (i, 0)),
                  pl.BlockSpec((1, d), lambda i, j: (0, 0)),
                  pl.BlockSpec((1, d, tn), lambda i, j: (layer, 0, j))],
        out_specs=pl.BlockSpec((tm, tn), lambda i, j: (i, j)),
        out_shape=jax.ShapeDtypeStruct((m, n), BF16),
        scratch_shapes=[pltpu.VMEM((tm, d), BF16)],
        compiler_params=_cparams(("parallel", "arbitrary")),
        name="norm_proj",
    )(x2, g.reshape(1, d), w)


def _att_bias_body(ext_ref, o_ref):
    ext = ext_ref[0]
    base = pltpu.roll(ext, ATT_EXT - (ATT_QB - 1), axis=1)
    slab = jnp.broadcast_to(base, (ATT_QB, ATT_EXT))
    tab = pltpu.roll(slab, 0, 1, stride=1, stride_axis=0)[:, :ATT_KW]
    qc = lax.broadcasted_iota(jnp.int32, (ATT_QB, ATT_KW), 0) // ATT_CHUNK
    kc = lax.broadcasted_iota(jnp.int32, (ATT_QB, ATT_KW), 1) // ATT_CHUNK
    o_ref[0] = jnp.where((kc >= qc) & (kc <= qc + ATT_LEFT), tab, MASK_NEG)


def _att_bias(rel_bias):
    h = rel_bias.shape[0]
    lead = ATT_QB - 1 + ATT_PAD - REL_CLIP
    tail = ATT_EXT - lead - rel_bias.shape[1]
    ext = jnp.concatenate([jnp.broadcast_to(rel_bias[:, :1], (h, lead)), rel_bias,
                           jnp.broadcast_to(rel_bias[:, -1:], (h, tail))], axis=1)
    return pl.pallas_call(
        _att_bias_body,
        grid=(h,),
        in_specs=[pl.BlockSpec((1, 1, ATT_EXT), lambda i: (i, 0, 0))],
        out_specs=pl.BlockSpec((1, ATT_QB, ATT_KW), lambda i: (i, 0, 0)),
        out_shape=jax.ShapeDtypeStruct((h, ATT_QB, ATT_KW), F32),
        compiler_params=_cparams(("parallel",)),
        name="att_bias",
    )(ext.reshape(h, 1, ATT_EXT))


def _attn_body(q_ref, k_ref, v_ref, gq_ref, gk_ref, bias_ref, o_ref, kpad, vpad, *, seq):
    kpad[0:ATT_PAD, :] = jnp.zeros((ATT_PAD, ATT_HEAD_DIM), BF16)
    vpad[0:ATT_PAD, :] = jnp.zeros((ATT_PAD, ATT_HEAD_DIM), BF16)
    kpad[ATT_PAD:, :] = _rms(k_ref[0].astype(F32), gk_ref[...]).astype(BF16)
    vpad[ATT_PAD:, :] = v_ref[0]
    bias = bias_ref[0]
    scale = ATT_HEAD_DIM ** -0.5
    for qb in range(seq // ATT_QB):
        r0 = qb * ATT_QB
        qn = _rms(q_ref[0, r0:r0 + ATT_QB, :].astype(F32), gq_ref[...]).astype(BF16)
        kw = kpad[r0:r0 + ATT_KW, :]
        s = lax.dot_general(qn, kw, _NT, preferred_element_type=F32) * scale + bias
        if r0 < ATT_PAD:
            col = lax.broadcasted_iota(jnp.int32, (ATT_QB, ATT_KW), 1)
            s = jnp.where(col + r0 >= ATT_PAD, s, MASK_NEG)
        m = jnp.max(s, axis=-1, keepdims=True)
        p = jnp.exp(s - m)
        l = jnp.sum(p, axis=-1, keepdims=True)
        o = jnp.dot(p.astype(BF16), vpad[r0:r0 + ATT_KW, :], preferred_element_type=F32)
        o_ref[0, r0:r0 + ATT_QB, :] = (o / l).astype(o_ref.dtype)


def _chunk_attention(proj3, gq, gk, bias, layer):
    b, seq, _ = proj3.shape
    bias0 = layer * ATT_HEADS
    blk = (1, seq, ATT_HEAD_DIM)
    return pl.pallas_call(
        functools.partial(_attn_body, seq=seq),
        grid=(b, ATT_HEADS),
        in_specs=[pl.BlockSpec(blk, lambda i, h: (i, 0, h)),
                  pl.BlockSpec(blk, lambda i, h: (i, 0, ATT_HEADS + h)),
                  pl.BlockSpec(blk, lambda i, h: (i, 0, 2 * ATT_HEADS + h)),
                  pl.BlockSpec((1, ATT_HEAD_DIM), lambda i, h: (0, 0)),
                  pl.BlockSpec((1, ATT_HEAD_DIM), lambda i, h: (0, 0)),
                  pl.BlockSpec((1, ATT_QB, ATT_KW), lambda i, h: (bias0 + h, 0, 0))],
        out_specs=pl.BlockSpec(blk, lambda i, h: (i, 0, h)),
        out_shape=jax.ShapeDtypeStruct((b, seq, ATT_WIDTH), BF16),
        scratch_shapes=[pltpu.VMEM((seq + ATT_PAD, ATT_HEAD_DIM), BF16),
                        pltpu.VMEM((seq + ATT_PAD, ATT_HEAD_DIM), BF16)],
        compiler_params=_cparams(("parallel", "parallel")),
        name="chunk_attention",
    )(proj3, proj3, proj3, gq.reshape(1, -1), gk.reshape(1, -1), bias)


def _cmul(xr, xi, yr, yi):
    return xr * yr - xi * yi, xr * yi + xi * yr


def _s5_perm_tables(pmat, pmat_t):
    width = SSM_GROUP * SSM_T
    blk = SSM_GROUP * SSM_SUB
    sub_bits = SSM_SUB.bit_length() - 1
    t_bits = SSM_T.bit_length() - 1
    blk_bits = blk.bit_length() - 1
    for jb in range(width // blk):
        row = lax.broadcasted_iota(jnp.int32, (width, blk), 0)
        col = lax.broadcasted_iota(jnp.int32, (width, blk), 1) + jb * blk
        s0 = col & (SSM_SUB - 1)
        src = (((col & (blk - 1)) >> sub_bits) << t_bits) + ((col >> blk_bits) << sub_bits) + s0
        pmat[:, jb * blk:(jb + 1) * blk] = jnp.where(row == src, 1.0, 0.0).astype(BF16)
        src_t = (((col & (SSM_T - 1)) >> sub_bits) << blk_bits) + ((col >> t_bits) << sub_bits) + s0
        pmat_t[:, jb * blk:(jb + 1) * blk] = jnp.where(row == src_t, 1.0, 0.0).astype(BF16)


def _s5_group(gi, lr2_ref, li2_ref, ldt_ref, brt_ref, bit_ref, crx_ref, cix_ref,
              xp, yp, *, cpb, nch):
    t_len, sub, n_ch, n_st = SSM_T, SSM_SUB, SSM_GROUP, SSM_STATE
    n_sub = t_len // sub
    dt = jnp.exp(ldt_ref[gi])
    lr2, li2 = lr2_ref[gi], li2_ref[gi]
    first_half = lax.broadcasted_iota(jnp.int32, (1, 2 * n_st), 1) < n_st

    mag = jnp.exp(lr2 * dt)
    ar, ai = mag * jnp.cos(li2 * dt), mag * jnp.sin(li2 * dt)
    den = lr2 * lr2 + li2 * li2
    nr, ni = ar - 1.0, ai
    zr, zi = (nr * lr2 + ni * li2) / den, (ni * lr2 - nr * li2) / den
    bbr = zr * brt_ref[gi] - zi * bit_ref[gi]
    bbi = zr * bit_ref[gi] + zi * brt_ref[gi]

    sub_bits = sub.bit_length() - 1
    t_bits = t_len.bit_length() - 1
    sq = [(ar, ai)]
    while (1 << (len(sq) - 1)) * 2 < t_len * cpb:
        sq.append(_cmul(*sq[-1], *sq[-1]))

    def power(e, base):
        pr = pi = None
        for k in range(sub_bits):
            bit = ((e >> k) & 1) == 1
            fr, fi = jnp.where(bit, base[k][0], 1.0), jnp.where(bit, base[k][1], 0.0)
            pr, pi = (fr, fi) if pr is None else _cmul(pr, pi, fr, fi)
        return pr, pi

    def rotate(x, a):
        return x * a[0] + pltpu.roll(x, n_st, axis=1) * jnp.where(first_half, -a[1], a[1])

    e_row = lax.broadcasted_iota(jnp.int32, (sub, 2 * n_st), 0)
    qr, qi = power(sub - 1 - e_row, sq)
    f_r = jnp.where(first_half, bbr, bbi)
    f_i = jnp.where(first_half, -bbi, bbr)
    fmat = jnp.concatenate([(qr * f_r[mch:mch + 1, :] + qi * f_i[mch:mch + 1, :]).astype(BF16)
                            for mch in range(n_ch)], axis=0)

    sq_rows = jnp.concatenate([v for k in range(sub_bits) for v in sq[k]], axis=0)
    sq_cols = sq_rows.T
    base_c = [(sq_cols[:, 2 * k:2 * k + 1], sq_cols[:, 2 * k + 1:2 * k + 2]) for k in range(sub_bits)]
    tau = lax.broadcasted_iota(jnp.int32, (2 * n_st, n_ch * sub), 1) & (sub - 1)
    p0r, p0i = power(tau, base_c)
    p1r, p1i = _cmul(p0r, p0i, *base_c[0])
    top = lax.broadcasted_iota(jnp.int32, (2 * n_st, 1), 0) < n_st
    crx, cix = crx_ref[gi], cix_ref[gi]

    def c_times(pr, pi):
        return crx * jnp.where(top, pr, -pi) + cix * jnp.where(top, -pi, -pr)

    emat = c_times(p1r, p1i).astype(BF16)

    kflat = jnp.dot(f_r, c_times(p0r, p0i), preferred_element_type=F32, precision=lax.Precision.HIGHEST)
    lane_t = lax.broadcasted_iota(jnp.int32, (sub, n_ch * sub), 1) & (sub - 1)
    causal = lane_t >= lax.broadcasted_iota(jnp.int32, (sub, n_ch * sub), 0)
    bdiag = []
    for mch in range(n_ch):
        tz = pltpu.roll(jnp.broadcast_to(kflat[mch:mch + 1, :], (sub, n_ch * sub)), 0, 1,
                        stride=1, stride_axis=0)
        bdiag.append(jnp.where(causal, tz, 0.0).astype(BF16))
    bdiag = jnp.concatenate(bdiag, axis=0)

    rows = pl.ds(pl.multiple_of(gi * nch, nch), nch)
    xg = xp[rows, :]
    blk = n_ch * sub
    x2 = jnp.concatenate([xg[:, j * blk:(j + 1) * blk] for j in range(n_sub)], axis=0)
    s2 = jnp.dot(x2, fmat, preferred_element_type=F32)
    a_sub = sq[sub_bits]

    def run(state):
        entering = []
        for j in range(n_sub):
            entering.append(state)
            state = rotate(state, a_sub) + s2[j * nch:(j + 1) * nch]
        return entering, state

    _, s_loc = run(jnp.zeros((nch, 2 * n_st), F32))

    cidx = lax.broadcasted_iota(jnp.int32, (nch, 1), 0) % cpb
    xin = jnp.where(cidx >= 1, pltpu.roll(s_loc, 1, axis=0), 0.0)
    d = 1
    while d < cpb:
        sh = pltpu.roll(xin, d, axis=0)
        xin = xin + jnp.where(cidx >= d, rotate(sh, sq[t_bits + d.bit_length() - 1]), 0.0)
        d *= 2

    entering, _ = run(xin)
    xin2 = jnp.concatenate(entering, axis=0).astype(BF16)
    y2 = (jnp.dot(x2, bdiag, preferred_element_type=F32)
          + jnp.dot(xin2, emat, preferred_element_type=F32))
    yp[rows, :] = jnp.concatenate([y2[j * nch:(j + 1) * nch] for j in range(n_sub)], axis=1).astype(BF16)


def _s5_body(u_ref, lr2_ref, li2_ref, ldt_ref, brt_ref, bit_ref, crx_ref, cix_ref, yt_ref,
             pmat, pmat_t, xp, yp, *, cpb):
    n_ch, t_len = SSM_GROUP, SSM_T
    nch = u_ref.shape[1]
    gp = u_ref.shape[0] // n_ch

    @pl.when(pl.program_id(0) == 0)
    def _():
        _s5_perm_tables(pmat, pmat_t)

    xall = jnp.concatenate(
        [jnp.concatenate([u_ref[g * n_ch + mch].astype(BF16) for mch in range(n_ch)], axis=1)
         for g in range(gp)], axis=0)
    xp[...] = jnp.dot(xall, pmat[...], preferred_element_type=F32).astype(BF16)

    def group(gi, carry):
        _s5_group(gi, lr2_ref, li2_ref, ldt_ref, brt_ref, bit_ref, crx_ref, cix_ref,
                  xp, yp, cpb=cpb, nch=nch)
        return carry

    lax.fori_loop(0, gp, group, 0, unroll=4)

    yall = jnp.dot(yp[...], pmat_t[...], preferred_element_type=F32)
    for g in range(gp):
        for nn in range(n_ch):
            yt_ref[g * n_ch + nn] = yall[g * nch:(g + 1) * nch, nn * t_len:(nn + 1) * t_len]


def _s5_params(lam_re, lam_im, log_dt, b_re, b_im, c_re, c_im):
    lg, p = lam_re.shape[0] * lam_re.shape[1], SSM_STATE
    flat = lambda a: a.reshape((lg,) + a.shape[2:])
    dup = lambda a, axis: jnp.concatenate([a, a], axis=axis)
    lr2 = dup(flat(lam_re), 1).reshape(lg, 1, 2 * p)
    li2 = dup(flat(lam_im), 1).reshape(lg, 1, 2 * p)
    ldt = log_dt.reshape(lg, 1, 1)
    brt = dup(jnp.swapaxes(flat(b_re), 1, 2), 2)
    bit = dup(jnp.swapaxes(flat(b_im), 1, 2), 2)
    crx = jnp.repeat(dup(jnp.swapaxes(flat(c_re), 1, 2), 1), SSM_SUB, axis=2)
    cix = jnp.repeat(dup(jnp.swapaxes(flat(c_im), 1, 2), 1), SSM_SUB, axis=2)
    return lr2, li2, ldt, brt, bit, crx, cix


def _s5_scan(u3, params, layer, *, cpb):
    g, p, n, gp = SSM_GROUPS, SSM_STATE, SSM_GROUP, SSM_GROUPS_PER_STEP
    nch = u3.shape[1]
    width = n * SSM_T
    blk = n * SSM_SUB
    step0 = layer * (g // gp)
    spec = lambda *s: pl.BlockSpec((gp,) + s, lambda i: (step0 + i, 0, 0))
    return pl.pallas_call(
        functools.partial(_s5_body, cpb=cpb),
        grid=(g // gp,),
        in_specs=[pl.BlockSpec((gp * n, nch, SSM_T), lambda i: (i, 0, 0)),
                  spec(1, 2 * p), spec(1, 2 * p), spec(1, 1),
                  spec(n, 2 * p), spec(n, 2 * p), spec(2 * p, blk), spec(2 * p, blk)],
        out_specs=pl.BlockSpec((gp * n, nch, SSM_T), lambda i: (i, 0, 0)),
        out_shape=jax.ShapeDtypeStruct((SSM_WIDTH, nch, SSM_T), F32),
        scratch_shapes=[pltpu.VMEM((width, width), BF16),
                        pltpu.VMEM((width, width), BF16),
                        pltpu.VMEM((gp * nch, width), BF16),
                        pltpu.VMEM((gp * nch, width), BF16)],
        compiler_params=_cparams(("arbitrary",)),
        name="s5_scan",
    )(u3, *params)


def _s5_post_body(yt_ref, u_ref, dsk_ref, wgt_ref, bg_ref, o_ref):
    w, nc, t_len = yt_ref.shape
    yt = yt_ref[...].reshape(w, nc * t_len)
    ut = u_ref[...].reshape(w, nc * t_len)
    y = jax.nn.gelu(yt + dsk_ref[...] * ut)
    z = jnp.dot(wgt_ref[0], y.astype(BF16), preferred_element_type=F32) + bg_ref[...]
    o_ref[...] = (y * jax.nn.sigmoid(z)).T.astype(o_ref.dtype)


def _s5_post(yt3, u3, d_skip, w_glu_t, layer, b_glu, chunks):
    w, nch, t_len = yt3.shape
    chunks = min(chunks, nch)
    return pl.pallas_call(
        _s5_post_body,
        grid=(nch // chunks,),
        in_specs=[pl.BlockSpec((w, chunks, t_len), lambda i: (0, i, 0)),
                  pl.BlockSpec((w, chunks, t_len), lambda i: (0, i, 0)),
                  pl.BlockSpec((w, 1), lambda i: (0, 0)),
                  pl.BlockSpec((1, w, w), lambda i: (layer, 0, 0)),
                  pl.BlockSpec((w, 1), lambda i: (0, 0))],
        out_specs=pl.BlockSpec((chunks * t_len, w), lambda i: (i, 0)),
        out_shape=jax.ShapeDtypeStruct((nch * t_len, w), BF16),
        compiler_params=_cparams(("parallel",)),
        name="s5_post",
    )(yt3, u3, d_skip.reshape(w, 1), w_glu_t, b_glu.reshape(w, 1))


def _mlstm_body(xm_ref, vm_ref, om_ref, if_ref, bif_ref, cw_ref, cb_ref, wq_ref, wk_ref,
                gh_ref, sk_ref, o_ref, xc_s, q_s, k_s, cmat, nvec, mrun, ifr_s, ifc_s, *, seq):
    t_len, dh = ML_T, ML_HEAD_DIM
    head = pl.program_id(1)
    xm = xm_ref[0].astype(F32)
    rowi = lax.broadcasted_iota(jnp.int32, (seq, 1), 0)
    cw = cw_ref[...]
    acc = cw[ML_CONV - 1:ML_CONV, :] * xm + cb_ref[...]
    for j in range(1, ML_CONV):
        shifted = jnp.where(rowi >= j, pltpu.roll(xm, j, axis=0), 0.0)
        acc = acc + cw[ML_CONV - 1 - j:ML_CONV - j, :] * shifted
    xc = acc * jax.nn.sigmoid(acc)
    xc_s[...] = xc
    xcb = xc.astype(BF16)
    q_s[...] = jnp.dot(xcb, wq_ref[0, 0], preferred_element_type=F32).astype(BF16)
    k_s[...] = (jnp.dot(xcb, wk_ref[0, 0], preferred_element_type=F32) * (dh ** -0.5)).astype(BF16)

    cmat[...] = jnp.zeros_like(cmat)
    nvec[...] = jnp.zeros_like(nvec)
    mrun[...] = jnp.full_like(mrun, ML_NEG)
    b_i, b_f = bif_ref[0][:, 0:1], bif_ref[0][:, 1:2]
    ifr_s[...] = jnp.concatenate([if_ref[pl.ds(head, 1), :], if_ref[pl.ds(ML_HEADS + head, 1), :],
                                  jnp.zeros((6, seq), F32)], axis=0)
    ifc_s[...] = ifr_s[...].T
    iota_r = lax.broadcasted_iota(jnp.int32, (t_len, t_len), 0)
    iota_c = lax.broadcasted_iota(jnp.int32, (t_len, t_len), 1)
    tri = iota_c <= iota_r
    tri_t = iota_r <= iota_c

    def step(c, carry):
        r0 = pl.multiple_of(c * t_len, t_len)
        rows = pl.ds(r0, t_len)
        ifr = ifr_s[:, rows]
        ifc = ifc_s[rows, :]
        i_col, i_row = ifc[:, 0:1] + b_i, ifr[0:1, :] + b_i
        lf_col = jax.nn.log_sigmoid(ifc[:, 1:2] + b_f)
        lf_row = jax.nn.log_sigmoid(ifr[1:2, :] + b_f)
        bcum_col = jnp.sum(jnp.where(tri, lf_row, 0.0), axis=1, keepdims=True)
        bcum_row = jnp.sum(jnp.where(tri_t, lf_col, 0.0), axis=0, keepdims=True)
        b_last = jnp.sum(lf_row, axis=1, keepdims=True)
        m_prev = mrun[...]
        dmat = jnp.where(tri, bcum_col - bcum_row + i_row, -jnp.inf)
        inter = bcum_col + m_prev
        m_row = jnp.maximum(jnp.max(dmat, axis=1, keepdims=True), inter)
        w_intra = jnp.exp(dmat - m_row)
        w_inter = jnp.exp(inter - m_row)
        qq, kk, vv = q_s[rows, :], k_s[rows, :], vm_ref[0, rows, :]
        s = lax.dot_general(qq, kk, _NT, preferred_element_type=F32) * w_intra
        cm = cmat[...]
        num = (jnp.dot(s.astype(BF16), vv, preferred_element_type=F32)
               + w_inter * lax.dot_general(qq, cm.astype(BF16), _NT, preferred_element_type=F32))
        den = (jnp.sum(s, axis=1, keepdims=True)
               + w_inter * jnp.sum(qq.astype(F32) * nvec[...], axis=1, keepdims=True))
        h = num / jnp.maximum(jnp.abs(den), jnp.exp(-m_row))
        g_col = b_last - bcum_col + i_col
        m_new = jnp.maximum(b_last + m_prev, jnp.max(g_col, axis=0, keepdims=True))
        wg = jnp.exp(g_col - m_new)
        decay = jnp.exp(b_last + m_prev - m_new)
        vw = (vv.astype(F32) * wg).astype(BF16)
        cmat[...] = decay * cm + lax.dot_general(vw, kk, _TN, preferred_element_type=F32)
        nvec[...] = decay * nvec[...] + jnp.sum(wg * kk.astype(F32), axis=0, keepdims=True)
        mrun[...] = m_new
        hn = _rms(h, gh_ref[...]) + sk_ref[...] * xc_s[rows, :]
        o_ref[0, rows, :] = (jax.nn.sigmoid(om_ref[0, rows, :].astype(F32)) * hn).astype(o_ref.dtype)
        return carry

    lax.fori_loop(0, seq // t_len, step, 0, unroll=2)


def _mlstm(proj3, ift, bif, conv_w, conv_b, wq, wk, layer, g_h, skip, col0):
    b, seq, _ = proj3.shape
    dh = ML_HEAD_DIM
    c0 = col0 // dh
    blk = (1, seq, dh)
    vec = pl.BlockSpec((1, dh), lambda i, h: (0, h))
    wspec = pl.BlockSpec((1, 1, dh, dh), lambda i, h: (layer, h, 0, 0))
    return pl.pallas_call(
        functools.partial(_mlstm_body, seq=seq),
        grid=(b, ML_HEADS),
        in_specs=[pl.BlockSpec(blk, lambda i, h: (i, 0, c0 + h)),
                  pl.BlockSpec(blk, lambda i, h: (i, 0, c0 + ML_HEADS + h)),
                  pl.BlockSpec(blk, lambda i, h: (i, 0, c0 + 2 * ML_HEADS + h)),
                  pl.BlockSpec((ift.shape[0], seq), lambda i, h: (0, i)),
                  pl.BlockSpec((1, 1, 2), lambda i, h: (h, 0, 0)),
                  pl.BlockSpec((ML_CONV, dh), lambda i, h: (0, h)),
                  vec, wspec, wspec, vec, vec],
        out_specs=pl.BlockSpec(blk, lambda i, h: (i, 0, h)),
        out_shape=jax.ShapeDtypeStruct((b, seq, ML_WIDTH), BF16),
        scratch_shapes=[pltpu.VMEM((seq, dh), F32),
                        pltpu.VMEM((seq, dh), BF16),
                        pltpu.VMEM((seq, dh), BF16),
                        pltpu.VMEM((dh, dh), F32),
                        pltpu.VMEM((1, dh), F32),
                        pltpu.VMEM((1, 1), F32),
                        pltpu.VMEM((8, seq), F32),
                        pltpu.VMEM((seq, 8), F32)],
        compiler_params=_cparams(("parallel", "parallel")),
        name="mlstm",
    )(proj3, proj3, proj3, ift, bif, conv_w, conv_b.reshape(1, -1), wq, wk,
      g_h.reshape(1, -1), skip.reshape(1, -1))


def _merge_body(x_ref, ya_ref, ys_ref, ym_ref, ga_ref, gs_ref, gm_ref, bg_ref,
                wa_ref, ws_ref, wm_ref, wo_ref, o_ref):
    d = x_ref.shape[1]
    bg = bg_ref[...]

    def branch(y_ref, w_ref, g_ref, k):
        gate = jax.nn.sigmoid(g_ref[...].astype(F32) + bg[:, k * d:(k + 1) * d])
        return gate * jnp.dot(y_ref[...], w_ref[0], preferred_element_type=F32)

    merged = branch(ya_ref, wa_ref, ga_ref, 0) + branch(ys_ref, ws_ref, gs_ref, 1) \
        + branch(ym_ref, wm_ref, gm_ref, 2)
    o_ref[...] = x_ref[...] + jnp.dot(merged.astype(BF16), wo_ref[0], preferred_element_type=F32)


def _merge(x2, ya, ys, ym, gates, b_gate, w_a, w_s, w_m, w_o, layer, tm):
    m, d = x2.shape
    tm = min(tm, m)
    row = lambda w: pl.BlockSpec((tm, w), lambda i: (i, 0))
    gate = lambda k: pl.BlockSpec((tm, d), lambda i: (i, k))
    wres = lambda w: _resident((1,) + w.shape[1:], lambda i: (layer, 0, 0))
    return pl.pallas_call(
        _merge_body,
        grid=(m // tm,),
        in_specs=[row(d), row(ya.shape[1]), row(ys.shape[1]), row(ym.shape[1]),
                  gate(0), gate(1), gate(2), _resident((1, 3 * d), lambda i: (0, 0)),
                  wres(w_a), wres(w_s), wres(w_m), wres(w_o)],
        out_specs=row(d),
        out_shape=jax.ShapeDtypeStruct((m, d), F32),
        compiler_params=_cparams(("parallel",)),
        name="merge",
    )(x2, ya, ys, ym, gates, gates, gates, b_gate.reshape(1, -1), w_a, w_s, w_m, w_o)


def _xattn_body(x_ref, g_ref, wq_ref, kv_ref, gq_ref, gk_ref, wo_ref, gn_ref, o_ref, hn_ref, q_s, att_s):
    d = x_ref.shape[1]
    dh = d // MEM_HEADS
    scale = dh ** -0.5
    hn = _rms(x_ref[...], g_ref[...]).astype(BF16)
    q_s[...] = jnp.dot(hn, wq_ref[0], preferred_element_type=F32)
    for h in range(MEM_HEADS):
        cols = slice(h * dh, (h + 1) * dh)
        qn = _rms(q_s[:, cols], gq_ref[...]).astype(BF16)
        kn = _rms(kv_ref[0, :, cols].astype(F32), gk_ref[...]).astype(BF16)
        s = lax.dot_general(qn, kn, _NT, preferred_element_type=F32) * scale
        p = jnp.exp(s - jnp.max(s, axis=-1, keepdims=True))
        l = jnp.sum(p, axis=-1, keepdims=True)
        v = kv_ref[0, :, d + h * dh:d + (h + 1) * dh]
        att_s[:, cols] = (jnp.dot(p.astype(BF16), v, preferred_element_type=F32) / l).astype(BF16)
    out = x_ref[...] + jnp.dot(att_s[...], wo_ref[0], preferred_element_type=F32)
    o_ref[...] = out
    hn_ref[...] = _rms(out, gn_ref[...]).astype(BF16)


def _xattn(x2, g, w_q, kv3, g_q, g_k, w_o, g_next, layer, seq, tm):
    m, d = x2.shape
    tm = min(tm, seq)
    per_seq = seq // tm
    wres = lambda w: _resident((1,) + w.shape[1:], lambda i: (layer, 0, 0))
    head = pl.BlockSpec((1, d // MEM_HEADS), lambda i: (0, 0))
    return pl.pallas_call(
        _xattn_body,
        grid=(m // tm,),
        in_specs=[pl.BlockSpec((tm, d), lambda i: (i, 0)),
                  pl.BlockSpec((1, d), lambda i: (0, 0)),
                  wres(w_q),
                  pl.BlockSpec((1,) + kv3.shape[1:], lambda i: (i // per_seq, 0, 0)),
                  head, head, wres(w_o),
                  pl.BlockSpec((1, d), lambda i: (0, 0))],
        out_specs=[pl.BlockSpec((tm, d), lambda i: (i, 0)), pl.BlockSpec((tm, d), lambda i: (i, 0))],
        out_shape=[jax.ShapeDtypeStruct((m, d), F32), jax.ShapeDtypeStruct((m, d), BF16)],
        scratch_shapes=[pltpu.VMEM((tm, d), F32), pltpu.VMEM((tm, d), BF16)],
        compiler_params=_cparams(("parallel",)),
        name="mem_attention",
    )(x2, g.reshape(1, d), w_q, kv3, g_q.reshape(1, -1), g_k.reshape(1, -1), w_o, g_next.reshape(1, d))


def _ffn_up_body(xa_ref, xb_ref, g_ref, wg_ref, wu_ref, a_ref, hn_ref, *, n_fill):
    rows = hn_ref.shape[0]
    s = pl.program_id(1)

    @pl.when(s < n_fill)
    def _():
        _fill_norm(xa_ref, xb_ref, g_ref, hn_ref, s)

    @pl.when(s >= n_fill)
    def _():
        wg, wu = wg_ref[0].astype(BF16), wu_ref[0].astype(BF16)
        for r in range(0, rows, DOT_ROWS):
            hn = hn_ref[r:r + DOT_ROWS, :]
            gate = jnp.dot(hn, wg, preferred_element_type=F32)
            up = jnp.dot(hn, wu, preferred_element_type=F32)
            a_ref[r:r + DOT_ROWS, :] = (gate * jax.nn.sigmoid(gate) * up).astype(a_ref.dtype)


def _ffn_up(x2, g, w_gu, layer):
    m, d = x2.shape
    f = w_gu.shape[2] // 2
    rows = min(ROW_GROUP, m)
    n_fill = rows // NORM_ROWS
    n_tiles = f // COL_TILE
    tile = lambda s: jnp.clip(s - n_fill, 0, n_tiles - 1)
    return pl.pallas_call(
        functools.partial(_ffn_up_body, n_fill=n_fill),
        grid=(m // rows, n_fill + n_tiles),
        in_specs=_fill_specs(d, n_fill) + [
                  pl.BlockSpec((1, d), lambda h, s: (0, 0)),
                  pl.BlockSpec((1, d, COL_TILE), lambda h, s: (layer, 0, tile(s))),
                  pl.BlockSpec((1, d, COL_TILE), lambda h, s: (layer, 0, n_tiles + tile(s)))],
        out_specs=pl.BlockSpec((rows, COL_TILE), lambda h, s: (h, tile(s))),
        out_shape=jax.ShapeDtypeStruct((m, f), BF16),
        scratch_shapes=[pltpu.VMEM((rows, d), BF16)],
        compiler_params=_cparams(("parallel", "arbitrary")),
        name="swiglu_up",
    )(x2, x2, g.reshape(1, d), w_gu, w_gu)


def _ffn_down_body(x_ref, a_ref, wa_ref, wb2_ref, *rest, n_load, k_tile):
    if len(rest) == 4:
        gn_ref, o_ref, hn_ref, wb_ref = rest
    else:
        (o_ref, wb_ref), gn_ref, hn_ref = rest, None, None
    s = pl.program_id(0)
    half = wa_ref.shape[2]

    @pl.when(s < n_load)
    def _():
        r0 = pl.multiple_of(s * k_tile, k_tile)
        wb_ref[pl.ds(r0, k_tile), :half] = wa_ref[0].astype(BF16)
        wb_ref[pl.ds(r0, k_tile), half:] = wb2_ref[0].astype(BF16)

    @pl.when(s >= n_load)
    def _():
        out = x_ref[...] + jnp.dot(a_ref[...], wb_ref[...], preferred_element_type=F32)
        o_ref[...] = out
        if hn_ref is not None:
            hn_ref[...] = _rms(out, gn_ref[...]).astype(BF16)


def _ffn_down(x2, act, w_down, layer, tm, k_tile, g_next=None):
    m, d = x2.shape
    f = act.shape[1]
    tm = min(tm, m)
    n_load = f // k_tile
    blk = lambda s: jnp.maximum(s - n_load, 0)
    row = pl.BlockSpec((tm, d), lambda s: (blk(s), 0))
    in_specs = [row,
                pl.BlockSpec((tm, f), lambda s: (blk(s), 0)),
                pl.BlockSpec((1, k_tile, d // 2), lambda s: (layer, jnp.minimum(s, n_load - 1), 0)),
                pl.BlockSpec((1, k_tile, d // 2), lambda s: (layer, jnp.minimum(s, n_load - 1), 1))]
    args = [x2, act, w_down, w_down]
    out_specs, out_shape = row, jax.ShapeDtypeStruct((m, d), F32)
    if g_next is not None:
        in_specs.append(pl.BlockSpec((1, d), lambda s: (0, 0)))
        args.append(g_next.reshape(1, d))
        out_specs, out_shape = [row, row], [out_shape, jax.ShapeDtypeStruct((m, d), BF16)]
    return pl.pallas_call(
        functools.partial(_ffn_down_body, n_load=n_load, k_tile=k_tile),
        grid=(n_load + m // tm,),
        in_specs=in_specs,
        out_specs=out_specs,
        out_shape=out_shape,
        scratch_shapes=[pltpu.VMEM((f, d), BF16)],
        compiler_params=_cparams(("arbitrary",)),
        name="swiglu_down",
    )(*args)


def kernel(x, mem, g_mem, norm_mix, w_in, b_gate, g_qa, g_ka, rel_bias, lam_re, lam_im, log_dt, b_re, b_im, c_re, c_im, d_skip, w_glu, b_glu, conv_w, conv_b, wq_m, wk_m, b_i, b_f, g_hm, skip_m, w_br_a, w_br_s, w_br_m, w_out, norm_x, w_xq, w_xkv, g_xq, g_xk, w_xo, norm_ffn, w_gu, w_down):
    b, seq, d = x.shape
    depth = w_in.shape[0]
    tokens = b * seq
    assert seq % ATT_QB == 0 and seq % ML_T == 0 and seq % SSM_T == 0
    assert tokens % min(ROW_GROUP, tokens) == 0

    w_glu_t = jnp.swapaxes(w_glu, 1, 2).astype(BF16)
    wq_b, wk_b = wq_m.astype(BF16), wk_m.astype(BF16)
    wa_b, ws_b, wm_b, wo_b = (w.astype(BF16) for w in (w_br_a, w_br_s, w_br_m, w_out))
    wxq_b, wxo_b = w_xq.astype(BF16), w_xo.astype(BF16)
    w_in_t = jnp.swapaxes(w_in, 1, 2)
    s5_params = _s5_params(lam_re, lam_im, log_dt, b_re, b_im, c_re, c_im)
    att_bias = _att_bias(rel_bias.reshape(depth * ATT_HEADS, -1))
    bif_all = jnp.stack([b_i, b_f], axis=-1).reshape(depth, ML_HEADS, 1, 2)

    x2 = x.reshape(tokens, d)
    mem2 = mem.reshape(b * mem.shape[1], d)
    hn_mix = x2
    for i in range(depth):
        proj, u3, gates, ift = _in_proj(hn_mix, norm_mix[i], w_in_t, i)
        proj3 = proj.reshape(b, seq, -1)

        ya = _chunk_attention(proj3, g_qa[i], g_ka[i], att_bias, i)

        yt3 = _s5_scan(u3, s5_params, i, cpb=seq // SSM_T)
        ys = _s5_post(yt3, u3, d_skip[i], w_glu_t, i, b_glu[i], chunks=8)

        ym = _mlstm(proj3, ift, bif_all[i], conv_w[i], conv_b[i], wq_b, wk_b, i, g_hm[i], skip_m[i],
                    3 * ATT_WIDTH)

        x2 = _merge(x2, ya.reshape(tokens, -1), ys, ym.reshape(tokens, -1), gates, b_gate[i],
                    wa_b, ws_b, wm_b, wo_b, i, tm=256)

        kv = _norm_proj(mem2, g_mem, w_xkv, i, tm=1024, tn=512)
        x2, hn_ffn = _xattn(x2, norm_x[i], wxq_b, kv.reshape(b, mem.shape[1], 2 * d), g_xq[i], g_xk[i],
                            wxo_b, norm_ffn[i], i, seq, tm=512)

        act = _ffn_up(hn_ffn, norm_ffn[i], w_gu, i)
        if i + 1 < depth:
            x2, hn_mix = _ffn_down(x2, act, w_down, i, tm=256, k_tile=256, g_next=norm_mix[i + 1])
        else:
            x2 = _ffn_down(x2, act, w_down, i, tm=256, k_tile=256)
    return x2.reshape(b, seq, d)
```

```python
import functools

import jax
import jax.numpy as jnp
from jax import lax
from jax.experimental import pallas as pl
from jax.experimental.pallas import tpu as pltpu

F32 = jnp.float32
BF16 = jnp.bfloat16

EPS = 1e-6
LANES = 128
BF16_ROWS = 16
ATT_CHUNK = 64
ATT_HEADS = 8
ATT_HEAD_DIM = 128
ATT_WIDTH = ATT_HEADS * ATT_HEAD_DIM
ATT_LEFT = 8
REL_CLIP = 256
ATT_QB = 256
ATT_PAD = ATT_LEFT * ATT_CHUNK
ATT_KW = ATT_QB + ATT_PAD
ATT_EXT = 1024
MASK_NEG = -1e30
SSM_GROUP = 16
SSM_GROUPS = 48
SSM_WIDTH = SSM_GROUP * SSM_GROUPS
SSM_STATE = 64
SSM_T = LANES
SSM_SUB = 16
SSM_GROUPS_PER_STEP = 8
ML_HEADS = 4
ML_HEAD_DIM = 256
ML_WIDTH = ML_HEADS * ML_HEAD_DIM
ML_CONV = 4
ML_T = 256
ML_NEG = -1e30
MEM_HEADS = 4
ROW_GROUP = 4096
NORM_ROWS = 256
COL_TILE = 256
DOT_ROWS = 1024

VMEM_LIMIT = 56 * 1024 * 1024

_NT = (((1,), (1,)), ((), ()))
_TN = (((0,), (0,)), ((), ()))


def _cparams(sem):
    return pltpu.CompilerParams(dimension_semantics=sem, vmem_limit_bytes=VMEM_LIMIT)


def _rms(x, g):
    return x * lax.rsqrt(jnp.mean(x * x, axis=-1, keepdims=True) + EPS) * g


def _resident(shape, index_map):
    return pl.BlockSpec(shape, index_map, pipeline_mode=pl.Buffered(1))


def _fill_specs(d, n_fill):
    row_blk = lambda h, s: h * n_fill + jnp.minimum(s, n_fill - 1)
    return [pl.BlockSpec((NORM_ROWS, d // 2), lambda h, s, c=c: (row_blk(h, s), c)) for c in range(2)]


def _fill_norm(xa_ref, xb_ref, g_ref, hn_ref, step):
    r0 = pl.multiple_of(step * NORM_ROWS, NORM_ROWS)
    x = jnp.concatenate([xa_ref[...], xb_ref[...]], axis=1)
    if x.dtype != BF16:
        x = _rms(x, g_ref[...]).astype(BF16)
    hn_ref[pl.ds(r0, NORM_ROWS), :] = x


def _in_proj_body(xa_ref, xb_ref, g_ref, wm_ref, wg_ref, we_ref, p_ref, u_ref, gt_ref, if_ref, hn_ref,
                  *, n_fill, n_att, n_ssm, n_ml, if_cols):
    rows = hn_ref.shape[0]
    s = pl.program_id(1)
    t = s - n_fill
    t_gate = t - (n_att + n_ssm + n_ml)

    @pl.when(s < n_fill)
    def _():
        _fill_norm(xa_ref, xb_ref, g_ref, hn_ref, s)

    def project(wt, o_ref):
        for r in range(0, rows, DOT_ROWS):
            o_ref[r:r + DOT_ROWS, :] = lax.dot_general(
                hn_ref[r:r + DOT_ROWS, :], wt, _NT, preferred_element_type=F32).astype(o_ref.dtype)

    in_ssm = (t >= n_att) & (t < n_att + n_ssm)

    @pl.when((t >= 0) & (t_gate < 0) & jnp.logical_not(in_ssm))
    def _():
        project(wm_ref[0].astype(BF16), p_ref)

    @pl.when(in_ssm)
    def _():
        wt = jnp.concatenate([wm_ref[0], wg_ref[0][:BF16_ROWS]], axis=0).astype(BF16)
        for r in range(0, rows, DOT_ROWS):
            ut = lax.dot_general(wt, hn_ref[r:r + DOT_ROWS, :], _NT, preferred_element_type=F32)
            u_ref[:, r // SSM_T:(r + DOT_ROWS) // SSM_T, :] = ut[:COL_TILE, :].reshape(
                COL_TILE, DOT_ROWS // SSM_T, SSM_T)
            if_ref[:, r:r + DOT_ROWS] = ut[COL_TILE:, :]

    @pl.when(t_gate >= 0)
    def _():
        wt = jnp.concatenate([wg_ref[0][if_cols:], we_ref[0]], axis=0)
        project(wt.astype(BF16), gt_ref)


def _in_proj(x2, g, w_in_t, layer):
    m, d = x2.shape
    rows = min(ROW_GROUP, m)
    n_fill = rows // NORM_ROWS
    n_att = 3 * ATT_WIDTH // COL_TILE
    n_ssm = SSM_WIDTH // COL_TILE
    n_ml = 3 * ML_WIDTH // COL_TILE
    n_main = n_att + n_ssm + n_ml
    if_cols = 2 * ML_HEADS
    gate_w = w_in_t.shape[1] - n_main * COL_TILE - if_cols
    n_gate = gate_w // COL_TILE
    assert gate_w % COL_TILE == 0 and if_cols <= BF16_ROWS and COL_TILE % if_cols == 0
    ext0 = (n_main + 1) * COL_TILE // if_cols
    ext_step = COL_TILE // if_cols

    def t_of(s):
        return s - n_fill

    def p_col(s):
        t = t_of(s)
        return jnp.where(t < n_att, jnp.clip(t, 0, n_att - 1), jnp.clip(t - n_ssm, n_att, n_att + n_ml - 1))

    def g_idx(s):
        return jnp.clip(t_of(s) - n_main, 0, n_gate - 1)

    body = functools.partial(_in_proj_body, n_fill=n_fill, n_att=n_att, n_ssm=n_ssm, n_ml=n_ml,
                             if_cols=if_cols)
    return pl.pallas_call(
        body,
        grid=(m // rows, n_fill + n_main + n_gate),
        in_specs=_fill_specs(d, n_fill) + [
                  pl.BlockSpec((1, d), lambda h, s: (0, 0)),
                  pl.BlockSpec((1, COL_TILE, d), lambda h, s: (layer, jnp.clip(t_of(s), 0, n_main - 1), 0)),
                  pl.BlockSpec((1, COL_TILE, d), lambda h, s: (layer, n_main + g_idx(s), 0)),
                  pl.BlockSpec((1, if_cols, d), lambda h, s: (layer, ext0 + ext_step * g_idx(s), 0))],
        out_specs=[pl.BlockSpec((rows, COL_TILE), lambda h, s: (h, p_col(s))),
                   pl.BlockSpec((COL_TILE, rows // SSM_T, SSM_T),
                                lambda h, s: (jnp.clip(t_of(s) - n_att, 0, n_ssm - 1), h, 0)),
                   pl.BlockSpec((rows, COL_TILE), lambda h, s: (h, g_idx(s))),
                   pl.BlockSpec((BF16_ROWS, rows), lambda h, s: (0, h))],
        out_shape=[jax.ShapeDtypeStruct((m, (n_att + n_ml) * COL_TILE), BF16),
                   jax.ShapeDtypeStruct((SSM_WIDTH, m // SSM_T, SSM_T), F32),
                   jax.ShapeDtypeStruct((m, gate_w), BF16),
                   jax.ShapeDtypeStruct((BF16_ROWS, m), F32)],
        scratch_shapes=[pltpu.VMEM((rows, d), BF16)],
        compiler_params=_cparams(("parallel", "arbitrary")),
        name="in_proj",
    )(x2, x2, g.reshape(1, d), w_in_t, w_in_t, w_in_t)


def _norm_proj_body(x_ref, g_ref, w_ref, o_ref, hn_ref):
    @pl.when(pl.program_id(1) == 0)
    def _():
        hn_ref[...] = _rms(x_ref[...], g_ref[...]).astype(BF16)

    o_ref[...] = jnp.dot(hn_ref[...], w_ref[0].astype(BF16),
                         preferred_element_type=F32).astype(o_ref.dtype)


def _norm_proj(x2, g, w, layer, tm, tn):
    m, d = x2.shape
    n = w.shape[2]
    tm = min(tm, m)
    return pl.pallas_call(
        _norm_proj_body,
        grid=(m // tm, n // tn),
        in_specs=[pl.BlockSpec((tm, d), lambda i, j: (i, 0)),
                  pl.BlockSpec((1, d), lambda i, j: (0, 0)),
                  pl.BlockSpec((1, d, tn), lambda i, j: (layer, 0, j))],
        out_specs=pl.BlockSpec((tm, tn), lambda i, j: (i, j)),
        out_shape=jax.ShapeDtypeStruct((m, n), BF16),
        scratch_shapes=[pltpu.VMEM((tm, d), BF16)],
        compiler_params=_cparams(("parallel", "arbitrary")),
        name="norm_proj",
    )(x2, g.reshape(1, d), w)


def _att_bias_body(ext_ref, o_ref):
    ext = ext_ref[0]
    base = pltpu.roll(ext, ATT_EXT - (ATT_QB - 1), axis=1)
    slab = jnp.broadcast_to(base, (ATT_QB, ATT_EXT))
    tab = pltpu.roll(slab, 0, 1, stride=1, stride_axis=0)[:, :ATT_KW]
    qc = lax.broadcasted_iota(jnp.int32, (ATT_QB, ATT_KW), 0) // ATT_CHUNK
    kc = lax.broadcasted_iota(jnp.int32, (ATT_QB, ATT_KW), 1) // ATT_CHUNK
    o_ref[0] = jnp.where((kc >= qc) & (kc <= qc + ATT_LEFT), tab, MASK_NEG)


def _att_bias(rel_bias):
    h = rel_bias.shape[0]
    lead = ATT_QB - 1 + ATT_PAD - REL_CLIP
    tail = ATT_EXT - lead - rel_bias.shape[1]
    ext = jnp.concatenate([jnp.broadcast_to(rel_bias[:, :1], (h, lead)), rel_bias,
                           jnp.broadcast_to(rel_bias[:, -1:], (h, tail))], axis=1)
    return pl.pallas_call(
        _att_bias_body,
        grid=(h,),
        in_specs=[pl.BlockSpec((1, 1, ATT_EXT), lambda i: (i, 0, 0))],
        out_specs=pl.BlockSpec((1, ATT_QB, ATT_KW), lambda i: (i, 0, 0)),
        out_shape=jax.ShapeDtypeStruct((h, ATT_QB, ATT_KW), F32),
        compiler_params=_cparams(("parallel",)),
        name="att_bias",
    )(ext.reshape(h, 1, ATT_EXT))


def _attn_body(q_ref, k_ref, v_ref, gq_ref, gk_ref, bias_ref, o_ref, kpad, vpad, *, seq):
    kpad[0:ATT_PAD, :] = jnp.zeros((ATT_PAD, ATT_HEAD_DIM), BF16)
    vpad[0:ATT_PAD, :] = jnp.zeros((ATT_PAD, ATT_HEAD_DIM), BF16)
    kpad[ATT_PAD:, :] = _rms(k_ref[0].astype(F32), gk_ref[...]).astype(BF16)
    vpad[ATT_PAD:, :] = v_ref[0]
    bias = bias_ref[0]
    scale = ATT_HEAD_DIM ** -0.5
    for qb in range(seq // ATT_QB):
        r0 = qb * ATT_QB
        qn = _rms(q_ref[0, r0:r0 + ATT_QB, :].astype(F32), gq_ref[...]).astype(BF16)
        kw = kpad[r0:r0 + ATT_KW, :]
        s = lax.dot_general(qn, kw, _NT, preferred_element_type=F32) * scale + bias
        if r0 < ATT_PAD:
            col = lax.broadcasted_iota(jnp.int32, (ATT_QB, ATT_KW), 1)
            s = jnp.where(col + r0 >= ATT_PAD, s, MASK_NEG)
        m = jnp.max(s, axis=-1, keepdims=True)
        p = jnp.exp(s - m)
        l = jnp.sum(p, axis=-1, keepdims=True)
        o = jnp.dot(p.astype(BF16), vpad[r0:r0 + ATT_KW, :], preferred_element_type=F32)
        o_ref[0, r0:r0 + ATT_QB, :] = (o / l).astype(o_ref.dtype)


def _chunk_attention(proj3, gq, gk, bias, layer):
    b, seq, _ = proj3.shape
    bias0 = layer * ATT_HEADS
    blk = (1, seq, ATT_HEAD_DIM)
    return pl.pallas_call(
        functools.partial(_attn_body, seq=seq),
        grid=(b, ATT_HEADS),
        in_specs=[pl.BlockSpec(blk, lambda i, h: (i, 0, h)),
                  pl.BlockSpec(blk, lambda i, h: (i, 0, ATT_HEADS + h)),
                  pl.BlockSpec(blk, lambda i, h: (i, 0, 2 * ATT_HEADS + h)),
                  pl.BlockSpec((1, ATT_HEAD_DIM), lambda i, h: (0, 0)),
                  pl.BlockSpec((1, ATT_HEAD_DIM), lambda i, h: (0, 0)),
                  pl.BlockSpec((1, ATT_QB, ATT_KW), lambda i, h: (bias0 + h, 0, 0))],
        out_specs=pl.BlockSpec(blk, lambda i, h: (i, 0, h)),
        out_shape=jax.ShapeDtypeStruct((b, seq, ATT_WIDTH), BF16),
        scratch_shapes=[pltpu.VMEM((seq + ATT_PAD, ATT_HEAD_DIM), BF16),
                        pltpu.VMEM((seq + ATT_PAD, ATT_HEAD_DIM), BF16)],
        compiler_params=_cparams(("parallel", "parallel")),
        name="chunk_attention",
    )(proj3, proj3, proj3, gq.reshape(1, -1), gk.reshape(1, -1), bias)


def _cmul(xr, xi, yr, yi):
    return xr * yr - xi * yi, xr * yi + xi * yr


def _s5_perm_tables(pmat, pmat_t):
    width = SSM_GROUP * SSM_T
    blk = SSM_GROUP * SSM_SUB
    sub_bits = SSM_SUB.bit_length() - 1
    t_bits = SSM_T.bit_length() - 1
    blk_bits = blk.bit_length() - 1
    for jb in range(width // blk):
        row = lax.broadcasted_iota(jnp.int32, (width, blk), 0)
        col = lax.broadcasted_iota(jnp.int32, (width, blk), 1) + jb * blk
        s0 = col & (SSM_SUB - 1)
        src = (((col & (blk - 1)) >> sub_bits) << t_bits) + ((col >> blk_bits) << sub_bits) + s0
        pmat[:, jb * blk:(jb + 1) * blk] = jnp.where(row == src, 1.0, 0.0).astype(BF16)
        src_t = (((col & (SSM_T - 1)) >> sub_bits) << blk_bits) + ((col >> t_bits) << sub_bits) + s0
        pmat_t[:, jb * blk:(jb + 1) * blk] = jnp.where(row == src_t, 1.0, 0.0).astype(BF16)


def _s5_group(gi, lr2_ref, li2_ref, ldt_ref, brt_ref, bit_ref, crx_ref, cix_ref,
              xp, yp, *, cpb, nch):
    t_len, sub, n_ch, n_st = SSM_T, SSM_SUB, SSM_GROUP, SSM_STATE
    n_sub = t_len // sub
    dt = jnp.exp(ldt_ref[gi])
    lr2, li2 = lr2_ref[gi], li2_ref[gi]
    first_half = lax.broadcasted_iota(jnp.int32, (1, 2 * n_st), 1) < n_st

    mag = jnp.exp(lr2 * dt)
    ar, ai = mag * jnp.cos(li2 * dt), mag * jnp.sin(li2 * dt)
    den = lr2 * lr2 + li2 * li2
    nr, ni = ar - 1.0, ai
    zr, zi = (nr * lr2 + ni * li2) / den, (ni * lr2 - nr * li2) / den
    bbr = zr * brt_ref[gi] - zi * bit_ref[gi]
    bbi = zr * bit_ref[gi] + zi * brt_ref[gi]

    sub_bits = sub.bit_length() - 1
    t_bits = t_len.bit_length() - 1
    sq = [(ar, ai)]
    while (1 << (len(sq) - 1)) * 2 < t_len * cpb:
        sq.append(_cmul(*sq[-1], *sq[-1]))

    def power(e, base):
        pr = pi = None
        for k in range(sub_bits):
            bit = ((e >> k) & 1) == 1
            fr, fi = jnp.where(bit, base[k][0], 1.0), jnp.where(bit, base[k][1], 0.0)
            pr, pi = (fr, fi) if pr is None else _cmul(pr, pi, fr, fi)
        return pr, pi

    def rotate(x, a):
        return x * a[0] + pltpu.roll(x, n_st, axis=1) * jnp.where(first_half, -a[1], a[1])

    e_row = lax.broadcasted_iota(jnp.int32, (sub, 2 * n_st), 0)
    qr, qi = power(sub - 1 - e_row, sq)
    f_r = jnp.where(first_half, bbr, bbi)
    f_i = jnp.where(first_half, -bbi, bbr)
    fmat = jnp.concatenate([(qr * f_r[mch:mch + 1, :] + qi * f_i[mch:mch + 1, :]).astype(BF16)
                            for mch in range(n_ch)], axis=0)

    sq_rows = jnp.concatenate([v for k in range(sub_bits) for v in sq[k]], axis=0)
    sq_cols = sq_rows.T
    base_c = [(sq_cols[:, 2 * k:2 * k + 1], sq_cols[:, 2 * k + 1:2 * k + 2]) for k in range(sub_bits)]
    tau = lax.broadcasted_iota(jnp.int32, (2 * n_st, n_ch * sub), 1) & (sub - 1)
    p0r, p0i = power(tau, base_c)
    p1r, p1i = _cmul(p0r, p0i, *base_c[0])
    top = lax.broadcasted_iota(jnp.int32, (2 * n_st, 1), 0) < n_st
    crx, cix = crx_ref[gi], cix_ref[gi]

    def c_times(pr, pi):
        return crx * jnp.where(top, pr, -pi) + cix * jnp.where(top, -pi, -pr)

    emat = c_times(p1r, p1i).astype(BF16)

    kflat = jnp.dot(f_r, c_times(p0r, p0i), preferred_element_type=F32, precision=lax.Precision.HIGHEST)
    lane_t = lax.broadcasted_iota(jnp.int32, (sub, n_ch * sub), 1) & (sub - 1)
    causal = lane_t >= lax.broadcasted_iota(jnp.int32, (sub, n_ch * sub), 0)
    bdiag = []
    for mch in range(n_ch):
        tz = pltpu.roll(jnp.broadcast_to(kflat[mch:mch + 1, :], (sub, n_ch * sub)), 0, 1,
                        stride=1, stride_axis=0)
        bdiag.append(jnp.where(causal, tz, 0.0).astype(BF16))
    bdiag = jnp.concatenate(bdiag, axis=0)

    rows = pl.ds(pl.multiple_of(gi * nch, nch), nch)
    xg = xp[rows, :]
    blk = n_ch * sub
    x2 = jnp.concatenate([xg[:, j * blk:(j + 1) * blk] for j in range(n_sub)], axis=0)
    s2 = jnp.dot(x2, fmat, preferred_element_type=F32)
    a_sub = sq[sub_bits]

    def run(state):
        entering = []
        for j in range(n_sub):
            entering.append(state)
            state = rotate(state, a_sub) + s2[j * nch:(j + 1) * nch]
        return entering, state

    _, s_loc = run(jnp.zeros((nch, 2 * n_st), F32))

    cidx = lax.broadcasted_iota(jnp.int32, (nch, 1), 0) % cpb
    xin = jnp.where(cidx >= 1, pltpu.roll(s_loc, 1, axis=0), 0.0)
    d = 1
    while d < cpb:
        sh = pltpu.roll(xin, d, axis=0)
        xin = xin + jnp.where(cidx >= d, rotate(sh, sq[t_bits + d.bit_length() - 1]), 0.0)
        d *= 2

    entering, _ = run(xin)
    xin2 = jnp.concatenate(entering, axis=0).astype(BF16)
    y2 = (jnp.dot(x2, bdiag, preferred_element_type=F32)
          + jnp.dot(xin2, emat, preferred_element_type=F32))
    yp[rows, :] = jnp.concatenate([y2[j * nch:(j + 1) * nch] for j in range(n_sub)], axis=1).astype(BF16)


def _s5_body(u_ref, lr2_ref, li2_ref, ldt_ref, brt_ref, bit_ref, crx_ref, cix_ref, yt_ref,
             pmat, pmat_t, xp, yp, *, cpb):
    n_ch, t_len = SSM_GROUP, SSM_T
    nch = u_ref.shape[1]
    gp = u_ref.shape[0] // n_ch

    @pl.when(pl.program_id(0) == 0)
    def _():
        _s5_perm_tables(pmat, pmat_t)

    xall = jnp.concatenate(
        [jnp.concatenate([u_ref[g * n_ch + mch].astype(BF16) for mch in range(n_ch)], axis=1)
         for g in range(gp)], axis=0)
    xp[...] = jnp.dot(xall, pmat[...], preferred_element_type=F32).astype(BF16)

    def group(gi, carry):
        _s5_group(gi, lr2_ref, li2_ref, ldt_ref, brt_ref, bit_ref, crx_ref, cix_ref,
                  xp, yp, cpb=cpb, nch=nch)
        return carry

    lax.fori_loop(0, gp, group, 0, unroll=4)

    yall = jnp.dot(yp[...], pmat_t[...], preferred_element_type=F32)
    for g in range(gp):
        for nn in range(n_ch):
            yt_ref[g * n_ch + nn] = yall[g * nch:(g + 1) * nch, nn * t_len:(nn + 1) * t_len]


def _s5_params(lam_re, lam_im, log_dt, b_re, b_im, c_re, c_im):
    lg, p = lam_re.shape[0] * lam_re.shape[1], SSM_STATE
    flat = lambda a: a.reshape((lg,) + a.shape[2:])
    dup = lambda a, axis: jnp.concatenate([a, a], axis=axis)
    lr2 = dup(flat(lam_re), 1).reshape(lg, 1, 2 * p)
    li2 = dup(flat(lam_im), 1).reshape(lg, 1, 2 * p)
    ldt = log_dt.reshape(lg, 1, 1)
    brt = dup(jnp.swapaxes(flat(b_re), 1, 2), 2)
    bit = dup(jnp.swapaxes(flat(b_im), 1, 2), 2)
    spread = jnp.repeat(jnp.eye(SSM_GROUP, dtype=F32), SSM_SUB, axis=1)
    expand = lambda c: jnp.einsum('gnp,nk->gpk', dup(flat(c), 2), spread, precision=lax.Precision.HIGHEST)
    crx, cix = expand(c_re), expand(c_im)
    return lr2, li2, ldt, brt, bit, crx, cix


def _s5_scan(u3, params, layer, *, cpb):
    g, p, n, gp = SSM_GROUPS, SSM_STATE, SSM_GROUP, SSM_GROUPS_PER_STEP
    nch = u3.shape[1]
    width = n * SSM_T
    blk = n * SSM_SUB
    step0 = layer * (g // gp)
    spec = lambda *s: pl.BlockSpec((gp,) + s, lambda i: (step0 + i, 0, 0))
    return pl.pallas_call(
        functools.partial(_s5_body, cpb=cpb),
        grid=(g // gp,),
        in_specs=[pl.BlockSpec((gp * n, nch, SSM_T), lambda i: (i, 0, 0)),
                  spec(1, 2 * p), spec(1, 2 * p), spec(1, 1),
                  spec(n, 2 * p), spec(n, 2 * p), spec(2 * p, blk), spec(2 * p, blk)],
        out_specs=pl.BlockSpec((gp * n, nch, SSM_T), lambda i: (i, 0, 0)),
        out_shape=jax.ShapeDtypeStruct((SSM_WIDTH, nch, SSM_T), F32),
        scratch_shapes=[pltpu.VMEM((width, width), BF16),
                        pltpu.VMEM((width, width), BF16),
                        pltpu.VMEM((gp * nch, width), BF16),
                        pltpu.VMEM((gp * nch, width), BF16)],
        compiler_params=_cparams(("arbitrary",)),
        name="s5_scan",
    )(u3, *params)


def _s5_post_body(yt_ref, u_ref, dsk_ref, wgt_ref, bg_ref, o_ref):
    w, nc, t_len = yt_ref.shape
    yt = yt_ref[...].reshape(w, nc * t_len)
    ut = u_ref[...].reshape(w, nc * t_len)
    y = jax.nn.gelu(yt + dsk_ref[...] * ut)
    z = jnp.dot(wgt_ref[0], y.astype(BF16), preferred_element_type=F32) + bg_ref[...]
    o_ref[...] = (y * jax.nn.sigmoid(z)).T.astype(o_ref.dtype)


def _s5_post(yt3, u3, d_skip, w_glu_t, layer, b_glu, chunks):
    w, nch, t_len = yt3.shape
    chunks = min(chunks, nch)
    return pl.pallas_call(
        _s5_post_body,
        grid=(nch // chunks,),
        in_specs=[pl.BlockSpec((w, chunks, t_len), lambda i: (0, i, 0)),
                  pl.BlockSpec((w, chunks, t_len), lambda i: (0, i, 0)),
                  pl.BlockSpec((w, 1), lambda i: (0, 0)),
                  pl.BlockSpec((1, w, w), lambda i: (layer, 0, 0)),
                  pl.BlockSpec((w, 1), lambda i: (0, 0))],
        out_specs=pl.BlockSpec((chunks * t_len, w), lambda i: (i, 0)),
        out_shape=jax.ShapeDtypeStruct((nch * t_len, w), BF16),
        compiler_params=_cparams(("parallel",)),
        name="s5_post",
    )(yt3, u3, d_skip.reshape(w, 1), w_glu_t, b_glu.reshape(w, 1))


def _mlstm_body(xm_ref, vm_ref, om_ref, if_ref, bif_ref, cw_ref, cb_ref, wq_ref, wk_ref,
                gh_ref, sk_ref, o_ref, xc_s, q_s, k_s, cmat, nvec, mrun, ifr_s, ifc_s, *, seq):
    t_len, dh = ML_T, ML_HEAD_DIM
    head = pl.program_id(1)
    xm = xm_ref[0].astype(F32)
    rowi = lax.broadcasted_iota(jnp.int32, (seq, 1), 0)
    cw = cw_ref[...]
    acc = cw[ML_CONV - 1:ML_CONV, :] * xm + cb_ref[...]
    for j in range(1, ML_CONV):
        shifted = jnp.where(rowi >= j, pltpu.roll(xm, j, axis=0), 0.0)
        acc = acc + cw[ML_CONV - 1 - j:ML_CONV - j, :] * shifted
    xc = acc * jax.nn.sigmoid(acc)
    xc_s[...] = xc
    xcb = xc.astype(BF16)
    q_s[...] = jnp.dot(xcb, wq_ref[0, 0], preferred_element_type=F32).astype(BF16)
    k_s[...] = (jnp.dot(xcb, wk_ref[0, 0], preferred_element_type=F32) * (dh ** -0.5)).astype(BF16)

    cmat[...] = jnp.zeros_like(cmat)
    nvec[...] = jnp.zeros_like(nvec)
    mrun[...] = jnp.full_like(mrun, ML_NEG)
    b_i, b_f = bif_ref[0][:, 0:1], bif_ref[0][:, 1:2]
    ifr_s[...] = jnp.concatenate([if_ref[pl.ds(head, 1), :], if_ref[pl.ds(ML_HEADS + head, 1), :],
                                  jnp.zeros((6, seq), F32)], axis=0)
    ifc_s[...] = ifr_s[...].T
    iota_r = lax.broadcasted_iota(jnp.int32, (t_len, t_len), 0)
    iota_c = lax.broadcasted_iota(jnp.int32, (t_len, t_len), 1)
    tri = iota_c <= iota_r
    tri_t = iota_r <= iota_c

    def step(c, carry):
        r0 = pl.multiple_of(c * t_len, t_len)
        rows = pl.ds(r0, t_len)
        ifr = ifr_s[:, rows]
        ifc = ifc_s[rows, :]
        i_col, i_row = ifc[:, 0:1] + b_i, ifr[0:1, :] + b_i
        lf_col = jax.nn.log_sigmoid(ifc[:, 1:2] + b_f)
        lf_row = jax.nn.log_sigmoid(ifr[1:2, :] + b_f)
        bcum_col = jnp.sum(jnp.where(tri, lf_row, 0.0), axis=1, keepdims=True)
        bcum_row = jnp.sum(jnp.where(tri_t, lf_col, 0.0), axis=0, keepdims=True)
        b_last = jnp.sum(lf_row, axis=1, keepdims=True)
        m_prev = mrun[...]
        dmat = jnp.where(tri, bcum_col - bcum_row + i_row, -jnp.inf)
        inter = bcum_col + m_prev
        m_row = jnp.maximum(jnp.max(dmat, axis=1, keepdims=True), inter)
        w_intra = jnp.exp(dmat - m_row)
        w_inter = jnp.exp(inter - m_row)
        qq, kk, vv = q_s[rows, :], k_s[rows, :], vm_ref[0, rows, :]
        s = lax.dot_general(qq, kk, _NT, preferred_element_type=F32) * w_intra
        cm = cmat[...]
        num = (jnp.dot(s.astype(BF16), vv, preferred_element_type=F32)
               + w_inter * lax.dot_general(qq, cm.astype(BF16), _NT, preferred_element_type=F32))
        den = (jnp.sum(s, axis=1, keepdims=True)
               + w_inter * jnp.sum(qq.astype(F32) * nvec[...], axis=1, keepdims=True))
        h = num / jnp.maximum(jnp.abs(den), jnp.exp(-m_row))
        g_col = b_last - bcum_col + i_col
        m_new = jnp.maximum(b_last + m_prev, jnp.max(g_col, axis=0, keepdims=True))
        wg = jnp.exp(g_col - m_new)
        decay = jnp.exp(b_last + m_prev - m_new)
        vw = (vv.astype(F32) * wg).astype(BF16)
        cmat[...] = decay * cm + lax.dot_general(vw, kk, _TN, preferred_element_type=F32)
        nvec[...] = decay * nvec[...] + jnp.sum(wg * kk.astype(F32), axis=0, keepdims=True)
        mrun[...] = m_new
        hn = _rms(h, gh_ref[...]) + sk_ref[...] * xc_s[rows, :]
        o_ref[0, rows, :] = (jax.nn.sigmoid(om_ref[0, rows, :].astype(F32)) * hn).astype(o_ref.dtype)
        return carry

    lax.fori_loop(0, seq // t_len, step, 0, unroll=2)


def _mlstm(proj3, ift, bif, conv_w, conv_b, wq, wk, layer, g_h, skip, col0):
    b, seq, _ = proj3.shape
    dh = ML_HEAD_DIM
    c0 = col0 // dh
    blk = (1, seq, dh)
    vec = pl.BlockSpec((1, dh), lambda i, h: (0, h))
    wspec = pl.BlockSpec((1, 1, dh, dh), lambda i, h: (layer, h, 0, 0))
    return pl.pallas_call(
        functools.partial(_mlstm_body, seq=seq),
        grid=(b, ML_HEADS),
        in_specs=[pl.BlockSpec(blk, lambda i, h: (i, 0, c0 + h)),
                  pl.BlockSpec(blk, lambda i, h: (i, 0, c0 + ML_HEADS + h)),
                  pl.BlockSpec(blk, lambda i, h: (i, 0, c0 + 2 * ML_HEADS + h)),
                  pl.BlockSpec((ift.shape[0], seq), lambda i, h: (0, i)),
                  pl.BlockSpec((1, 1, 2), lambda i, h: (h, 0, 0)),
                  pl.BlockSpec((ML_CONV, dh), lambda i, h: (0, h)),
                  vec, wspec, wspec, vec, vec],
        out_specs=pl.BlockSpec(blk, lambda i, h: (i, 0, h)),
        out_shape=jax.ShapeDtypeStruct((b, seq, ML_WIDTH), BF16),
        scratch_shapes=[pltpu.VMEM((seq, dh), F32),
                        pltpu.VMEM((seq, dh), BF16),
                        pltpu.VMEM((seq, dh), BF16),
                        pltpu.VMEM((dh, dh), F32),
                        pltpu.VMEM((1, dh), F32),
                        pltpu.VMEM((1, 1), F32),
                        pltpu.VMEM((8, seq), F32),
                        pltpu.VMEM((seq, 8), F32)],
        compiler_params=_cparams(("parallel", "parallel")),
        name="mlstm",
    )(proj3, proj3, proj3, ift, bif, conv_w, conv_b.reshape(1, -1), wq, wk,
      g_h.reshape(1, -1), skip.reshape(1, -1))


def _merge_body(x_ref, ya_ref, ys_ref, ym_ref, ga_ref, gs_ref, gm_ref, bg_ref,
                wa_ref, ws_ref, wm_ref, wo_ref, o_ref):
    d = x_ref.shape[1]
    bg = bg_ref[...]

    def branch(y_ref, w_ref, g_ref, k):
        gate = jax.nn.sigmoid(g_ref[...].astype(F32) + bg[:, k * d:(k + 1) * d])
        return gate * jnp.dot(y_ref[...], w_ref[0], preferred_element_type=F32)

    merged = branch(ya_ref, wa_ref, ga_ref, 0) + branch(ys_ref, ws_ref, gs_ref, 1) \
        + branch(ym_ref, wm_ref, gm_ref, 2)
    o_ref[...] = x_ref[...] + jnp.dot(merged.astype(BF16), wo_ref[0], preferred_element_type=F32)


def _merge(x2, ya, ys, ym, gates, b_gate, w_a, w_s, w_m, w_o, layer, tm):
    m, d = x2.shape
    tm = min(tm, m)
    row = lambda w: pl.BlockSpec((tm, w), lambda i: (i, 0))
    gate = lambda k: pl.BlockSpec((tm, d), lambda i: (i, k))
    wres = lambda w: _resident((1,) + w.shape[1:], lambda i: (layer, 0, 0))
    return pl.pallas_call(
        _merge_body,
        grid=(m // tm,),
        in_specs=[row(d), row(ya.shape[1]), row(ys.shape[1]), row(ym.shape[1]),
                  gate(0), gate(1), gate(2), _resident((1, 3 * d), lambda i: (0, 0)),
                  wres(w_a), wres(w_s), wres(w_m), wres(w_o)],
        out_specs=row(d),
        out_shape=jax.ShapeDtypeStruct((m, d), F32),
        compiler_params=_cparams(("parallel",)),
        name="merge",
    )(x2, ya, ys, ym, gates, gates, gates, b_gate.reshape(1, -1), w_a, w_s, w_m, w_o)


def _xattn_body(x_ref, g_ref, wq_ref, kv_ref, gq_ref, gk_ref, wo_ref, gn_ref, o_ref, hn_ref, q_s, att_s):
    d = x_ref.shape[1]
    dh = d // MEM_HEADS
    scale = dh ** -0.5
    hn = _rms(x_ref[...], g_ref[...]).astype(BF16)
    q_s[...] = jnp.dot(hn, wq_ref[0], preferred_element_type=F32)
    for h in range(MEM_HEADS):
        cols = slice(h * dh, (h + 1) * dh)
        qn = _rms(q_s[:, cols], gq_ref[...]).astype(BF16)
        kn = _rms(kv_ref[0, :, cols].astype(F32), gk_ref[...]).astype(BF16)
        s = lax.dot_general(qn, kn, _NT, preferred_element_type=F32) * scale
        p = jnp.exp(s - jnp.max(s, axis=-1, keepdims=True))
        l = jnp.sum(p, axis=-1, keepdims=True)
        v = kv_ref[0, :, d + h * dh:d + (h + 1) * dh]
        att_s[:, cols] = (jnp.dot(p.astype(BF16), v, preferred_element_type=F32) / l).astype(BF16)
    out = x_ref[...] + jnp.dot(att_s[...], wo_ref[0], preferred_element_type=F32)
    o_ref[...] = out
    hn_ref[...] = _rms(out, gn_ref[...]).astype(BF16)


def _xattn(x2, g, w_q, kv3, g_q, g_k, w_o, g_next, layer, seq, tm):
    m, d = x2.shape
    tm = min(tm, seq)
    per_seq = seq // tm
    wres = lambda w: _resident((1,) + w.shape[1:], lambda i: (layer, 0, 0))
    head = pl.BlockSpec((1, d // MEM_HEADS), lambda i: (0, 0))
    return pl.pallas_call(
        _xattn_body,
        grid=(m // tm,),
        in_specs=[pl.BlockSpec((tm, d), lambda i: (i, 0)),
                  pl.BlockSpec((1, d), lambda i: (0, 0)),
                  wres(w_q),
                  pl.BlockSpec((1,) + kv3.shape[1:], lambda i: (i // per_seq, 0, 0)),
                  head, head, wres(w_o),
                  pl.BlockSpec((1, d), lambda i: (0, 0))],
        out_specs=[pl.BlockSpec((tm, d), lambda i: (i, 0)), pl.BlockSpec((tm, d), lambda i: (i, 0))],
        out_shape=[jax.ShapeDtypeStruct((m, d), F32), jax.ShapeDtypeStruct((m, d), BF16)],
        scratch_shapes=[pltpu.VMEM((tm, d), F32), pltpu.VMEM((tm, d), BF16)],
        compiler_params=_cparams(("parallel",)),
        name="mem_attention",
    )(x2, g.reshape(1, d), w_q, kv3, g_q.reshape(1, -1), g_k.reshape(1, -1), w_o, g_next.reshape(1, d))


def _ffn_up_body(xa_ref, xb_ref, g_ref, wg_ref, wu_ref, a_ref, hn_ref, *, n_fill):
    rows = hn_ref.shape[0]
    s = pl.program_id(1)

    @pl.when(s < n_fill)
    def _():
        _fill_norm(xa_ref, xb_ref, g_ref, hn_ref, s)

    @pl.when(s >= n_fill)
    def _():
        wg, wu = wg_ref[0].astype(BF16), wu_ref[0].astype(BF16)
        for r in range(0, rows, DOT_ROWS):
            hn = hn_ref[r:r + DOT_ROWS, :]
            gate = jnp.dot(hn, wg, preferred_element_type=F32)
            up = jnp.dot(hn, wu, preferred_element_type=F32)
            a_ref[r:r + DOT_ROWS, :] = (gate * jax.nn.sigmoid(gate) * up).astype(a_ref.dtype)


def _ffn_up(x2, g, w_gu, layer):
    m, d = x2.shape
    f = w_gu.shape[2] // 2
    rows = min(ROW_GROUP, m)
    n_fill = rows // NORM_ROWS
    n_tiles = f // COL_TILE
    tile = lambda s: jnp.clip(s - n_fill, 0, n_tiles - 1)
    return pl.pallas_call(
        functools.partial(_ffn_up_body, n_fill=n_fill),
        grid=(m // rows, n_fill + n_tiles),
        in_specs=_fill_specs(d, n_fill) + [
                  pl.BlockSpec((1, d), lambda h, s: (0, 0)),
                  pl.BlockSpec((1, d, COL_TILE), lambda h, s: (layer, 0, tile(s))),
                  pl.BlockSpec((1, d, COL_TILE), lambda h, s: (layer, 0, n_tiles + tile(s)))],
        out_specs=pl.BlockSpec((rows, COL_TILE), lambda h, s: (h, tile(s))),
        out_shape=jax.ShapeDtypeStruct((m, f), BF16),
        scratch_shapes=[pltpu.VMEM((rows, d), BF16)],
        compiler_params=_cparams(("parallel", "arbitrary")),
        name="swiglu_up",
    )(x2, x2, g.reshape(1, d), w_gu, w_gu)


def _ffn_down_body(x_ref, a_ref, wa_ref, wb2_ref, *rest, n_load, k_tile):
    if len(rest) == 4:
        gn_ref, o_ref, hn_ref, wb_ref = rest
    else:
        (o_ref, wb_ref), gn_ref, hn_ref = rest, None, None
    s = pl.program_id(0)
    half = wa_ref.shape[2]

    @pl.when(s < n_load)
    def _():
        r0 = pl.multiple_of(s * k_tile, k_tile)
        wb_ref[pl.ds(r0, k_tile), :half] = wa_ref[0].astype(BF16)
        wb_ref[pl.ds(r0, k_tile), half:] = wb2_ref[0].astype(BF16)

    @pl.when(s >= n_load)
    def _():
        out = x_ref[...] + jnp.dot(a_ref[...], wb_ref[...], preferred_element_type=F32)
        o_ref[...] = out
        if hn_ref is not None:
            hn_ref[...] = _rms(out, gn_ref[...]).astype(BF16)


def _ffn_down(x2, act, w_down, layer, tm, k_tile, g_next=None):
    m, d = x2.shape
    f = act.shape[1]
    tm = min(tm, m)
    n_load = f // k_tile
    blk = lambda s: jnp.maximum(s - n_load, 0)
    row = pl.BlockSpec((tm, d), lambda s: (blk(s), 0))
    in_specs = [row,
                pl.BlockSpec((tm, f), lambda s: (blk(s), 0)),
                pl.BlockSpec((1, k_tile, d // 2), lambda s: (layer, jnp.minimum(s, n_load - 1), 0)),
                pl.BlockSpec((1, k_tile, d // 2), lambda s: (layer, jnp.minimum(s, n_load - 1), 1))]
    args = [x2, act, w_down, w_down]
    out_specs, out_shape = row, jax.ShapeDtypeStruct((m, d), F32)
    if g_next is not None:
        in_specs.append(pl.BlockSpec((1, d), lambda s: (0, 0)))
        args.append(g_next.reshape(1, d))
        out_specs, out_shape = [row, row], [out_shape, jax.ShapeDtypeStruct((m, d), BF16)]
    return pl.pallas_call(
        functools.partial(_ffn_down_body, n_load=n_load, k_tile=k_tile),
        grid=(n_load + m // tm,),
        in_specs=in_specs,
        out_specs=out_specs,
        out_shape=out_shape,
        scratch_shapes=[pltpu.VMEM((f, d), BF16)],
        compiler_params=_cparams(("arbitrary",)),
        name="swiglu_down",
    )(*args)


def kernel(x, mem, g_mem, norm_mix, w_in, b_gate, g_qa, g_ka, rel_bias, lam_re, lam_im, log_dt, b_re, b_im, c_re, c_im, d_skip, w_glu, b_glu, conv_w, conv_b, wq_m, wk_m, b_i, b_f, g_hm, skip_m, w_br_a, w_br_s, w_br_m, w_out, norm_x, w_xq, w_xkv, g_xq, g_xk, w_xo, norm_ffn, w_gu, w_down):
    b, seq, d = x.shape
    depth = w_in.shape[0]
    tokens = b * seq
    assert seq % ATT_QB == 0 and seq % ML_T == 0 and seq % SSM_T == 0
    assert tokens % min(ROW_GROUP, tokens) == 0

    w_glu_t = jnp.swapaxes(w_glu, 1, 2).astype(BF16)
    wq_b, wk_b = wq_m.astype(BF16), wk_m.astype(BF16)
    wa_b, ws_b, wm_b, wo_b = (w.astype(BF16) for w in (w_br_a, w_br_s, w_br_m, w_out))
    wxq_b, wxo_b = w_xq.astype(BF16), w_xo.astype(BF16)
    w_in_t = jnp.swapaxes(w_in, 1, 2)
    s5_params = _s5_params(lam_re, lam_im, log_dt, b_re, b_im, c_re, c_im)
    att_bias = _att_bias(rel_bias.reshape(depth * ATT_HEADS, -1))
    bif_all = jnp.stack([b_i, b_f], axis=-1).reshape(depth, ML_HEADS, 1, 2)

    x2 = x.reshape(tokens, d)
    mem2 = mem.reshape(b * mem.shape[1], d)
    hn_mix = x2
    for i in range(depth):
        proj, u3, gates, ift = _in_proj(hn_mix, norm_mix[i], w_in_t, i)
        proj3 = proj.reshape(b, seq, -1)

        ya = _chunk_attention(proj3, g_qa[i], g_ka[i], att_bias, i)

        yt3 = _s5_scan(u3, s5_params, i, cpb=seq // SSM_T)
        ys = _s5_post(yt3, u3, d_skip[i], w_glu_t, i, b_glu[i], chunks=8)

        ym = _mlstm(proj3, ift, bif_all[i], conv_w[i], conv_b[i], wq_b, wk_b, i, g_hm[i], skip_m[i],
                    3 * ATT_WIDTH)

        x2 = _merge(x2, ya.reshape(tokens, -1), ys, ym.reshape(tokens, -1), gates, b_gate[i],
                    wa_b, ws_b, wm_b, wo_b, i, tm=256)

        kv = _norm_proj(mem2, g_mem, w_xkv, i, tm=1024, tn=512)
        x2, hn_ffn = _xattn(x2, norm_x[i], wxq_b, kv.reshape(b, mem.shape[1], 2 * d), g_xq[i], g_xk[i],
                            wxo_b, norm_ffn[i], i, seq, tm=512)

        act = _ffn_up(hn_ffn, norm_ffn[i], w_gu, i)
        if i + 1 < depth:
            x2, hn_mix = _ffn_down(x2, act, w_down, i, tm=256, k_tile=256, g_next=norm_mix[i + 1])
        else:
            x2 = _ffn_down(x2, act, w_down, i, tm=256, k_tile=256)
    return x2.reshape(b, seq, d)
```

```python
import functools

import jax
import jax.numpy as jnp
from jax import lax
from jax.experimental import pallas as pl
from jax.experimental.pallas import tpu as pltpu

F32 = jnp.float32
BF16 = jnp.bfloat16

EPS = 1e-6
LANES = 128
BF16_ROWS = 16
ATT_CHUNK = 64
ATT_HEADS = 8
ATT_HEAD_DIM = 128
ATT_WIDTH = ATT_HEADS * ATT_HEAD_DIM
ATT_LEFT = 8
REL_CLIP = 256
ATT_QB = 256
ATT_PAD = ATT_LEFT * ATT_CHUNK
ATT_KW = ATT_QB + ATT_PAD
ATT_EXT = 1024
MASK_NEG = -1e30
SSM_GROUP = 16
SSM_GROUPS = 48
SSM_WIDTH = SSM_GROUP * SSM_GROUPS
SSM_STATE = 64
SSM_T = LANES
SSM_SUB = 16
SSM_GROUPS_PER_STEP = 8
ML_HEADS = 4
ML_HEAD_DIM = 256
ML_WIDTH = ML_HEADS * ML_HEAD_DIM
ML_CONV = 4
ML_T = 256
ML_NEG = -1e30
MEM_HEADS = 4
ROW_GROUP = 4096
NORM_ROWS = 256
COL_TILE = 256
DOT_ROWS = 1024

VMEM_LIMIT = 56 * 1024 * 1024

_NT = (((1,), (1,)), ((), ()))
_TN = (((0,), (0,)), ((), ()))


def _cparams(sem):
    return pltpu.CompilerParams(dimension_semantics=sem, vmem_limit_bytes=VMEM_LIMIT)


def _rms(x, g):
    return x * lax.rsqrt(jnp.mean(x * x, axis=-1, keepdims=True) + EPS) * g


def _resident(shape, index_map):
    return pl.BlockSpec(shape, index_map, pipeline_mode=pl.Buffered(1))


def _fill_specs(d, n_fill):
    row_blk = lambda h, s: h * n_fill + jnp.minimum(s, n_fill - 1)
    return [pl.BlockSpec((NORM_ROWS, d // 2), lambda h, s, c=c: (row_blk(h, s), c)) for c in range(2)]


def _fill_norm(xa_ref, xb_ref, g_ref, hn_ref, step):
    r0 = pl.multiple_of(step * NORM_ROWS, NORM_ROWS)
    x = jnp.concatenate([xa_ref[...], xb_ref[...]], axis=1)
    if x.dtype != BF16:
        x = _rms(x, g_ref[...]).astype(BF16)
    hn_ref[pl.ds(r0, NORM_ROWS), :] = x


def _in_proj_body(xa_ref, xb_ref, g_ref, wm_ref, wg_ref, we_ref, p_ref, u_ref, gt_ref, if_ref, hn_ref,
                  *, n_fill, n_att, n_ssm, n_ml, if_cols):
    rows = hn_ref.shape[0]
    s = pl.program_id(1)
    t = s - n_fill
    t_gate = t - (n_att + n_ssm + n_ml)

    @pl.when(s < n_fill)
    def _():
        _fill_norm(xa_ref, xb_ref, g_ref, hn_ref, s)

    def project(wt, o_ref):
        for r in range(0, rows, DOT_ROWS):
            o_ref[r:r + DOT_ROWS, :] = lax.dot_general(
                hn_ref[r:r + DOT_ROWS, :], wt, _NT, preferred_element_type=F32).astype(o_ref.dtype)

    in_ssm = (t >= n_att) & (t < n_att + n_ssm)

    @pl.when((t >= 0) & (t_gate < 0) & jnp.logical_not(in_ssm))
    def _():
        project(wm_ref[0].astype(BF16), p_ref)

    @pl.when(in_ssm)
    def _():
        wt = jnp.concatenate([wm_ref[0], wg_ref[0][:BF16_ROWS]], axis=0).astype(BF16)
        for r in range(0, rows, DOT_ROWS):
            ut = lax.dot_general(wt, hn_ref[r:r + DOT_ROWS, :], _NT, preferred_element_type=F32)
            u_ref[:, r // SSM_T:(r + DOT_ROWS) // SSM_T, :] = ut[:COL_TILE, :].reshape(
                COL_TILE, DOT_ROWS // SSM_T, SSM_T)
            if_ref[:, r:r + DOT_ROWS] = ut[COL_TILE:, :]

    @pl.when(t_gate >= 0)
    def _():
        wt = jnp.concatenate([wg_ref[0][if_cols:], we_ref[0]], axis=0)
        project(wt.astype(BF16), gt_ref)


def _in_proj(x2, g, w_in_t, layer):
    m, d = x2.shape
    rows = min(ROW_GROUP, m)
    n_fill = rows // NORM_ROWS
    n_att = 3 * ATT_WIDTH // COL_TILE
    n_ssm = SSM_WIDTH // COL_TILE
    n_ml = 3 * ML_WIDTH // COL_TILE
    n_main = n_att + n_ssm + n_ml
    if_cols = 2 * ML_HEADS
    gate_w = w_in_t.shape[1] - n_main * COL_TILE - if_cols
    n_gate = gate_w // COL_TILE
    assert gate_w % COL_TILE == 0 and if_cols <= BF16_ROWS and COL_TILE % if_cols == 0
    ext0 = (n_main + 1) * COL_TILE // if_cols
    ext_step = COL_TILE // if_cols

    def t_of(s):
        return s - n_fill

    def p_col(s):
        t = t_of(s)
        return jnp.where(t < n_att, jnp.clip(t, 0, n_att - 1), jnp.clip(t - n_ssm, n_att, n_att + n_ml - 1))

    def g_idx(s):
        return jnp.clip(t_of(s) - n_main, 0, n_gate - 1)

    body = functools.partial(_in_proj_body, n_fill=n_fill, n_att=n_att, n_ssm=n_ssm, n_ml=n_ml,
                             if_cols=if_cols)
    return pl.pallas_call(
        body,
        grid=(m // rows, n_fill + n_main + n_gate),
        in_specs=_fill_specs(d, n_fill) + [
                  pl.BlockSpec((1, d), lambda h, s: (0, 0)),
                  pl.BlockSpec((1, COL_TILE, d), lambda h, s: (layer, jnp.clip(t_of(s), 0, n_main - 1), 0)),
                  pl.BlockSpec((1, COL_TILE, d), lambda h, s: (layer, n_main + g_idx(s), 0)),
                  pl.BlockSpec((1, if_cols, d), lambda h, s: (layer, ext0 + ext_step * g_idx(s), 0))],
        out_specs=[pl.BlockSpec((rows, COL_TILE), lambda h, s: (h, p_col(s))),
                   pl.BlockSpec((COL_TILE, rows // SSM_T, SSM_T),
                                lambda h, s: (jnp.clip(t_of(s) - n_att, 0, n_ssm - 1), h, 0)),
                   pl.BlockSpec((rows, COL_TILE), lambda h, s: (h, g_idx(s))),
                   pl.BlockSpec((BF16_ROWS, rows), lambda h, s: (0, h))],
        out_shape=[jax.ShapeDtypeStruct((m, (n_att + n_ml) * COL_TILE), BF16),
                   jax.ShapeDtypeStruct((SSM_WIDTH, m // SSM_T, SSM_T), F32),
                   jax.ShapeDtypeStruct((m, gate_w), BF16),
                   jax.ShapeDtypeStruct((BF16_ROWS, m), F32)],
        scratch_shapes=[pltpu.VMEM((rows, d), BF16)],
        compiler_params=_cparams(("parallel", "arbitrary")),
        name="in_proj",
    )(x2, x2, g.reshape(1, d), w_in_t, w_in_t, w_in_t)


def _norm_proj_body(x_ref, g_ref, w_ref, o_ref, hn_ref):
    @pl.when(pl.program_id(1) == 0)
    def _():
        hn_ref[...] = _rms(x_ref[...], g_ref[...]).astype(BF16)

    o_ref[...] = jnp.dot(hn_ref[...], w_ref[0].astype(BF16),
                         preferred_element_type=F32).astype(o_ref.dtype)


def _norm_proj(x2, g, w, layer, tm, tn):
    m, d = x2.shape
    n = w.shape[2]
    tm = min(tm, m)
    return pl.pallas_call(
        _norm_proj_body,
        grid=(m // tm, n // tn),
        in_specs=[pl.BlockSpec((tm, d), lambda i, j: (i, 0)),
                  pl.BlockSpec((1, d), lambda i, j: (0, 0)),
                  pl.BlockSpec((1, d, tn), lambda i, j: (layer, 0, j))],
        out_specs=pl.BlockSpec((tm, tn), lambda i, j: (i, j)),
        out_shape=jax.ShapeDtypeStruct((m, n), BF16),
        scratch_shapes=[pltpu.VMEM((tm, d), BF16)],
        compiler_params=_cparams(("parallel", "arbitrary")),
        name="norm_proj",
    )(x2, g.reshape(1, d), w)


def _att_bias_body(ext_ref, o_ref):
    ext = ext_ref[0]
    base = pltpu.roll(ext, ATT_EXT - (ATT_QB - 1), axis=1)
    slab = jnp.broadcast_to(base, (ATT_QB, ATT_EXT))
    tab = pltpu.roll(slab, 0, 1, stride=1, stride_axis=0)[:, :ATT_KW]
    qc = lax.broadcasted_iota(jnp.int32, (ATT_QB, ATT_KW), 0) // ATT_CHUNK
    kc = lax.broadcasted_iota(jnp.int32, (ATT_QB, ATT_KW), 1) // ATT_CHUNK
    o_ref[0] = jnp.where((kc >= qc) & (kc <= qc + ATT_LEFT), tab, MASK_NEG)


def _att_bias(rel_bias):
    h = rel_bias.shape[0]
    lead = ATT_QB - 1 + ATT_PAD - REL_CLIP
    tail = ATT_EXT - lead - rel_bias.shape[1]
    ext = jnp.concatenate([jnp.broadcast_to(rel_bias[:, :1], (h, lead)), rel_bias,
                           jnp.broadcast_to(rel_bias[:, -1:], (h, tail))], axis=1)
    return pl.pallas_call(
        _att_bias_body,
        grid=(h,),
        in_specs=[pl.BlockSpec((1, 1, ATT_EXT), lambda i: (i, 0, 0))],
        out_specs=pl.BlockSpec((1, ATT_QB, ATT_KW), lambda i: (i, 0, 0)),
        out_shape=jax.ShapeDtypeStruct((h, ATT_QB, ATT_KW), F32),
        compiler_params=_cparams(("parallel",)),
        name="att_bias",
    )(ext.reshape(h, 1, ATT_EXT))


def _attn_body(q_ref, k_ref, v_ref, gq_ref, gk_ref, bias_ref, o_ref, kpad, vpad, *, seq):
    kpad[0:ATT_PAD, :] = jnp.zeros((ATT_PAD, ATT_HEAD_DIM), BF16)
    vpad[0:ATT_PAD, :] = jnp.zeros((ATT_PAD, ATT_HEAD_DIM), BF16)
    kpad[ATT_PAD:, :] = _rms(k_ref[0].astype(F32), gk_ref[...]).astype(BF16)
    vpad[ATT_PAD:, :] = v_ref[0]
    bias = bias_ref[0]
    scale = ATT_HEAD_DIM ** -0.5
    for qb in range(seq // ATT_QB):
        r0 = qb * ATT_QB
        qn = _rms(q_ref[0, r0:r0 + ATT_QB, :].astype(F32), gq_ref[...]).astype(BF16)
        kw = kpad[r0:r0 + ATT_KW, :]
        s = lax.dot_general(qn, kw, _NT, preferred_element_type=F32) * scale + bias
        if r0 < ATT_PAD:
            col = lax.broadcasted_iota(jnp.int32, (ATT_QB, ATT_KW), 1)
            s = jnp.where(col + r0 >= ATT_PAD, s, MASK_NEG)
        m = jnp.max(s, axis=-1, keepdims=True)
        p = jnp.exp(s - m)
        l = jnp.sum(p, axis=-1, keepdims=True)
        o = jnp.dot(p.astype(BF16), vpad[r0:r0 + ATT_KW, :], preferred_element_type=F32)
        o_ref[0, r0:r0 + ATT_QB, :] = (o / l).astype(o_ref.dtype)


def _chunk_attention(proj3, gq, gk, bias, layer):
    b, seq, _ = proj3.shape
    bias0 = layer * ATT_HEADS
    blk = (1, seq, ATT_HEAD_DIM)
    return pl.pallas_call(
        functools.partial(_attn_body, seq=seq),
        grid=(b, ATT_HEADS),
        in_specs=[pl.BlockSpec(blk, lambda i, h: (i, 0, h)),
                  pl.BlockSpec(blk, lambda i, h: (i, 0, ATT_HEADS + h)),
                  pl.BlockSpec(blk, lambda i, h: (i, 0, 2 * ATT_HEADS + h)),
                  pl.BlockSpec((1, ATT_HEAD_DIM), lambda i, h: (0, 0)),
                  pl.BlockSpec((1, ATT_HEAD_DIM), lambda i, h: (0, 0)),
                  pl.BlockSpec((1, ATT_QB, ATT_KW), lambda i, h: (bias0 + h, 0, 0))],
        out_specs=pl.BlockSpec(blk, lambda i, h: (i, 0, h)),
        out_shape=jax.ShapeDtypeStruct((b, seq, ATT_WIDTH), BF16),
        scratch_shapes=[pltpu.VMEM((seq + ATT_PAD, ATT_HEAD_DIM), BF16),
                        pltpu.VMEM((seq + ATT_PAD, ATT_HEAD_DIM), BF16)],
        compiler_params=_cparams(("parallel", "parallel")),
        name="chunk_attention",
    )(proj3, proj3, proj3, gq.reshape(1, -1), gk.reshape(1, -1), bias)


def _cmul(xr, xi, yr, yi):
    return xr * yr - xi * yi, xr * yi + xi * yr


def _s5_perm_tables(pmat, pmat_t):
    width = SSM_GROUP * SSM_T
    blk = SSM_GROUP * SSM_SUB
    sub_bits = SSM_SUB.bit_length() - 1
    t_bits = SSM_T.bit_length() - 1
    blk_bits = blk.bit_length() - 1
    for jb in range(width // blk):
        row = lax.broadcasted_iota(jnp.int32, (width, blk), 0)
        col = lax.broadcasted_iota(jnp.int32, (width, blk), 1) + jb * blk
        s0 = col & (SSM_SUB - 1)
        src = (((col & (blk - 1)) >> sub_bits) << t_bits) + ((col >> blk_bits) << sub_bits) + s0
        pmat[:, jb * blk:(jb + 1) * blk] = jnp.where(row == src, 1.0, 0.0).astype(BF16)
        src_t = (((col & (SSM_T - 1)) >> sub_bits) << blk_bits) + ((col >> t_bits) << sub_bits) + s0
        pmat_t[:, jb * blk:(jb + 1) * blk] = jnp.where(row == src_t, 1.0, 0.0).astype(BF16)


def _s5_group(gi, lr2_ref, li2_ref, ldt_ref, brt_ref, bit_ref, crx_ref, cix_ref,
              xp, yp, *, cpb, nch):
    t_len, sub, n_ch, n_st = SSM_T, SSM_SUB, SSM_GROUP, SSM_STATE
    n_sub = t_len // sub
    dt = jnp.exp(ldt_ref[gi])
    lr2, li2 = lr2_ref[gi], li2_ref[gi]
    first_half = lax.broadcasted_iota(jnp.int32, (1, 2 * n_st), 1) < n_st

    mag = jnp.exp(lr2 * dt)
    ar, ai = mag * jnp.cos(li2 * dt), mag * jnp.sin(li2 * dt)
    den = lr2 * lr2 + li2 * li2
    nr, ni = ar - 1.0, ai
    zr, zi = (nr * lr2 + ni * li2) / den, (ni * lr2 - nr * li2) / den
    bbr = zr * brt_ref[gi] - zi * bit_ref[gi]
    bbi = zr * bit_ref[gi] + zi * brt_ref[gi]

    sub_bits = sub.bit_length() - 1
    t_bits = t_len.bit_length() - 1
    sq = [(ar, ai)]
    while (1 << (len(sq) - 1)) * 2 < t_len * cpb:
        sq.append(_cmul(*sq[-1], *sq[-1]))

    def power(e, base):
        pr = pi = None
        for k in range(sub_bits):
            bit = ((e >> k) & 1) == 1
            fr, fi = jnp.where(bit, base[k][0], 1.0), jnp.where(bit, base[k][1], 0.0)
            pr, pi = (fr, fi) if pr is None else _cmul(pr, pi, fr, fi)
        return pr, pi

    def rotate(x, a):
        return x * a[0] + pltpu.roll(x, n_st, axis=1) * jnp.where(first_half, -a[1], a[1])

    e_row = lax.broadcasted_iota(jnp.int32, (sub, 2 * n_st), 0)
    qr, qi = power(sub - 1 - e_row, sq)
    f_r = jnp.where(first_half, bbr, bbi)
    f_i = jnp.where(first_half, -bbi, bbr)
    fmat = jnp.concatenate([(qr * f_r[mch:mch + 1, :] + qi * f_i[mch:mch + 1, :]).astype(BF16)
                            for mch in range(n_ch)], axis=0)

    sq_rows = jnp.concatenate([v for k in range(sub_bits) for v in sq[k]], axis=0)
    sq_cols = sq_rows.T
    base_c = [(sq_cols[:, 2 * k:2 * k + 1], sq_cols[:, 2 * k + 1:2 * k + 2]) for k in range(sub_bits)]
    tau = lax.broadcasted_iota(jnp.int32, (2 * n_st, n_ch * sub), 1) & (sub - 1)
    p0r, p0i = power(tau, base_c)
    p1r, p1i = _cmul(p0r, p0i, *base_c[0])
    top = lax.broadcasted_iota(jnp.int32, (2 * n_st, 1), 0) < n_st
    crx, cix = crx_ref[gi], cix_ref[gi]

    def c_times(pr, pi):
        return crx * jnp.where(top, pr, -pi) + cix * jnp.where(top, -pi, -pr)

    emat = c_times(p1r, p1i).astype(BF16)

    kflat = jnp.dot(f_r, c_times(p0r, p0i), preferred_element_type=F32, precision=lax.Precision.HIGHEST)
    lane_t = lax.broadcasted_iota(jnp.int32, (sub, n_ch * sub), 1) & (sub - 1)
    causal = lane_t >= lax.broadcasted_iota(jnp.int32, (sub, n_ch * sub), 0)
    bdiag = []
    for mch in range(n_ch):
        tz = pltpu.roll(jnp.broadcast_to(kflat[mch:mch + 1, :], (sub, n_ch * sub)), 0, 1,
                        stride=1, stride_axis=0)
        bdiag.append(jnp.where(causal, tz, 0.0).astype(BF16))
    bdiag = jnp.concatenate(bdiag, axis=0)

    rows = pl.ds(pl.multiple_of(gi * nch, nch), nch)
    xg = xp[rows, :]
    blk = n_ch * sub
    x2 = jnp.concatenate([xg[:, j * blk:(j + 1) * blk] for j in range(n_sub)], axis=0)
    s2 = jnp.dot(x2, fmat, preferred_element_type=F32)
    a_sub = sq[sub_bits]

    def run(state):
        entering = []
        for j in range(n_sub):
            entering.append(state)
            state = rotate(state, a_sub) + s2[j * nch:(j + 1) * nch]
        return entering, state

    _, s_loc = run(jnp.zeros((nch, 2 * n_st), F32))

    cidx = lax.broadcasted_iota(jnp.int32, (nch, 1), 0) % cpb
    xin = jnp.where(cidx >= 1, pltpu.roll(s_loc, 1, axis=0), 0.0)
    d = 1
    while d < cpb:
        sh = pltpu.roll(xin, d, axis=0)
        xin = xin + jnp.where(cidx >= d, rotate(sh, sq[t_bits + d.bit_length() - 1]), 0.0)
        d *= 2

    entering, _ = run(xin)
    xin2 = jnp.concatenate(entering, axis=0).astype(BF16)
    y2 = (jnp.dot(x2, bdiag, preferred_element_type=F32)
          + jnp.dot(xin2, emat, preferred_element_type=F32))
    yp[rows, :] = jnp.concatenate([y2[j * nch:(j + 1) * nch] for j in range(n_sub)], axis=1).astype(BF16)


def _s5_body(u_ref, lr2_ref, li2_ref, ldt_ref, brt_ref, bit_ref, crx_ref, cix_ref, yt_ref,
             pmat, pmat_t, xp, yp, *, cpb):
    n_ch, t_len = SSM_GROUP, SSM_T
    nch = u_ref.shape[1]
    gp = u_ref.shape[0] // n_ch

    @pl.when(pl.program_id(0) == 0)
    def _():
        _s5_perm_tables(pmat, pmat_t)

    xall = jnp.concatenate(
        [jnp.concatenate([u_ref[g * n_ch + mch].astype(BF16) for mch in range(n_ch)], axis=1)
         for g in range(gp)], axis=0)
    xp[...] = jnp.dot(xall, pmat[...], preferred_element_type=F32).astype(BF16)

    def group(gi, carry):
        _s5_group(gi, lr2_ref, li2_ref, ldt_ref, brt_ref, bit_ref, crx_ref, cix_ref,
                  xp, yp, cpb=cpb, nch=nch)
        return carry

    lax.fori_loop(0, gp, group, 0, unroll=4)

    yall = jnp.dot(yp[...], pmat_t[...], preferred_element_type=F32)
    for g in range(gp):
        for nn in range(n_ch):
            yt_ref[g * n_ch + nn] = yall[g * nch:(g + 1) * nch, nn * t_len:(nn + 1) * t_len]


def _s5_params(lam_re, lam_im, log_dt, b_re, b_im, c_re, c_im):
    lg, p = lam_re.shape[0] * lam_re.shape[1], SSM_STATE
    flat = lambda a: a.reshape((lg,) + a.shape[2:])
    dup = lambda a, axis: jnp.concatenate([a, a], axis=axis)
    lr2 = dup(flat(lam_re), 1).reshape(lg, 1, 2 * p)
    li2 = dup(flat(lam_im), 1).reshape(lg, 1, 2 * p)
    ldt = log_dt.reshape(lg, 1, 1)
    brt = dup(jnp.swapaxes(flat(b_re), 1, 2), 2)
    bit = dup(jnp.swapaxes(flat(b_im), 1, 2), 2)
    crx = jnp.repeat(dup(jnp.swapaxes(flat(c_re), 1, 2), 1), SSM_SUB, axis=2)
    cix = jnp.repeat(dup(jnp.swapaxes(flat(c_im), 1, 2), 1), SSM_SUB, axis=2)
    return lr2, li2, ldt, brt, bit, crx, cix


def _s5_scan(u3, params, layer, *, cpb):
    g, p, n, gp = SSM_GROUPS, SSM_STATE, SSM_GROUP, SSM_GROUPS_PER_STEP
    nch = u3.shape[1]
    width = n * SSM_T
    blk = n * SSM_SUB
    step0 = layer * (g // gp)
    spec = lambda *s: pl.BlockSpec((gp,) + s, lambda i: (step0 + i, 0, 0))
    return pl.pallas_call(
        functools.partial(_s5_body, cpb=cpb),
        grid=(g // gp,),
        in_specs=[pl.BlockSpec((gp * n, nch, SSM_T), lambda i: (i, 0, 0)),
                  spec(1, 2 * p), spec(1, 2 * p), spec(1, 1),
                  spec(n, 2 * p), spec(n, 2 * p), spec(2 * p, blk), spec(2 * p, blk)],
        out_specs=pl.BlockSpec((gp * n, nch, SSM_T), lambda i: (i, 0, 0)),
        out_shape=jax.ShapeDtypeStruct((SSM_WIDTH, nch, SSM_T), F32),
        scratch_shapes=[pltpu.VMEM((width, width), BF16),
                        pltpu.VMEM((width, width), BF16),
                        pltpu.VMEM((gp * nch, width), BF16),
                        pltpu.VMEM((gp * nch, width), BF16)],
        compiler_params=_cparams(("arbitrary",)),
        name="s5_scan",
    )(u3, *params)


def _s5_post_body(yt_ref, u_ref, dsk_ref, wgt_ref, bg_ref, o_ref):
    w, nc, t_len = yt_ref.shape
    yt = yt_ref[...].reshape(w, nc * t_len)
    ut = u_ref[...].reshape(w, nc * t_len)
    y = jax.nn.gelu(yt + dsk_ref[...] * ut)
    z = jnp.dot(wgt_ref[0], y.astype(BF16), preferred_element_type=F32) + bg_ref[...]
    o_ref[...] = (y * jax.nn.sigmoid(z)).T.astype(o_ref.dtype)


def _s5_post(yt3, u3, d_skip, w_glu_t, layer, b_glu, chunks):
    w, nch, t_len = yt3.shape
    chunks = min(chunks, nch)
    return pl.pallas_call(
        _s5_post_body,
        grid=(nch // chunks,),
        in_specs=[pl.BlockSpec((w, chunks, t_len), lambda i: (0, i, 0)),
                  pl.BlockSpec((w, chunks, t_len), lambda i: (0, i, 0)),
                  pl.BlockSpec((w, 1), lambda i: (0, 0)),
                  pl.BlockSpec((1, w, w), lambda i: (layer, 0, 0)),
                  pl.BlockSpec((w, 1), lambda i: (0, 0))],
        out_specs=pl.BlockSpec((chunks * t_len, w), lambda i: (i, 0)),
        out_shape=jax.ShapeDtypeStruct((nch * t_len, w), BF16),
        compiler_params=_cparams(("parallel",)),
        name="s5_post",
    )(yt3, u3, d_skip.reshape(w, 1), w_glu_t, b_glu.reshape(w, 1))


def _mlstm_body(xm_ref, vm_ref, om_ref, if_ref, bif_ref, cw_ref, cb_ref, wq_ref, wk_ref,
                gh_ref, sk_ref, o_ref, xc_s, q_s, k_s, cmat, nvec, mrun, ifr_s, ifc_s, *, seq):
    t_len, dh = ML_T, ML_HEAD_DIM
    head = pl.program_id(1)
    xm = xm_ref[0].astype(F32)
    rowi = lax.broadcasted_iota(jnp.int32, (seq, 1), 0)
    cw = cw_ref[...]
    acc = cw[ML_CONV - 1:ML_CONV, :] * xm + cb_ref[...]
    for j in range(1, ML_CONV):
        shifted = jnp.where(rowi >= j, pltpu.roll(xm, j, axis=0), 0.0)
        acc = acc + cw[ML_CONV - 1 - j:ML_CONV - j, :] * shifted
    xc = acc * jax.nn.sigmoid(acc)
    xc_s[...] = xc
    xcb = xc.astype(BF16)
    q_s[...] = jnp.dot(xcb, wq_ref[0, 0], preferred_element_type=F32).astype(BF16)
    k_s[...] = (jnp.dot(xcb, wk_ref[0, 0], preferred_element_type=F32) * (dh ** -0.5)).astype(BF16)

    cmat[...] = jnp.zeros_like(cmat)
    nvec[...] = jnp.zeros_like(nvec)
    mrun[...] = jnp.full_like(mrun, ML_NEG)
    b_i, b_f = bif_ref[0][:, 0:1], bif_ref[0][:, 1:2]
    ifr_s[...] = jnp.concatenate([if_ref[pl.ds(head, 1), :], if_ref[pl.ds(ML_HEADS + head, 1), :],
                                  jnp.zeros((6, seq), F32)], axis=0)
    ifc_s[...] = ifr_s[...].T
    iota_r = lax.broadcasted_iota(jnp.int32, (t_len, t_len), 0)
    iota_c = lax.broadcasted_iota(jnp.int32, (t_len, t_len), 1)
    tri = iota_c <= iota_r
    tri_t = iota_r <= iota_c

    def step(c, carry):
        r0 = pl.multiple_of(c * t_len, t_len)
        rows = pl.ds(r0, t_len)
        ifr = ifr_s[:, rows]
        ifc = ifc_s[rows, :]
        i_col, i_row = ifc[:, 0:1] + b_i, ifr[0:1, :] + b_i
        lf_col = jax.nn.log_sigmoid(ifc[:, 1:2] + b_f)
        lf_row = jax.nn.log_sigmoid(ifr[1:2, :] + b_f)
        bcum_col = jnp.sum(jnp.where(tri, lf_row, 0.0), axis=1, keepdims=True)
        bcum_row = jnp.sum(jnp.where(tri_t, lf_col, 0.0), axis=0, keepdims=True)
        b_last = jnp.sum(lf_row, axis=1, keepdims=True)
        m_prev = mrun[...]
        dmat = jnp.where(tri, bcum_col - bcum_row + i_row, -jnp.inf)
        inter = bcum_col + m_prev
        m_row = jnp.maximum(jnp.max(dmat, axis=1, keepdims=True), inter)
        w_intra = jnp.exp(dmat - m_row)
        w_inter = jnp.exp(inter - m_row)
        qq, kk, vv = q_s[rows, :], k_s[rows, :], vm_ref[0, rows, :]
        s = lax.dot_general(qq, kk, _NT, preferred_element_type=F32) * w_intra
        cm = cmat[...]
        num = (jnp.dot(s.astype(BF16), vv, preferred_element_type=F32)
               + w_inter * lax.dot_general(qq, cm.astype(BF16), _NT, preferred_element_type=F32))
        den = (jnp.sum(s, axis=1, keepdims=True)
               + w_inter * jnp.sum(qq.astype(F32) * nvec[...], axis=1, keepdims=True))
        h = num / jnp.maximum(jnp.abs(den), jnp.exp(-m_row))
        g_col = b_last - bcum_col + i_col
        m_new = jnp.maximum(b_last + m_prev, jnp.max(g_col, axis=0, keepdims=True))
        wg = jnp.exp(g_col - m_new)
        decay = jnp.exp(b_last + m_prev - m_new)
        vw = (vv.astype(F32) * wg).astype(BF16)
        cmat[...] = decay * cm + lax.dot_general(vw, kk, _TN, preferred_element_type=F32)
        nvec[...] = decay * nvec[...] + jnp.sum(wg * kk.astype(F32), axis=0, keepdims=True)
        mrun[...] = m_new
        hn = _rms(h, gh_ref[...]) + sk_ref[...] * xc_s[rows, :]
        o_ref[0, rows, :] = (jax.nn.sigmoid(om_ref[0, rows, :].astype(F32)) * hn).astype(o_ref.dtype)
        return carry

    lax.fori_loop(0, seq // t_len, step, 0, unroll=2)


def _mlstm(proj3, ift, bif, conv_w, conv_b, wq, wk, layer, g_h, skip, col0):
    b, seq, _ = proj3.shape
    dh = ML_HEAD_DIM
    c0 = col0 // dh
    blk = (1, seq, dh)
    vec = pl.BlockSpec((1, dh), lambda i, h: (0, h))
    wspec = pl.BlockSpec((1, 1, dh, dh), lambda i, h: (layer, h, 0, 0))
    return pl.pallas_call(
        functools.partial(_mlstm_body, seq=seq),
        grid=(b, ML_HEADS),
        in_specs=[pl.BlockSpec(blk, lambda i, h: (i, 0, c0 + h)),
                  pl.BlockSpec(blk, lambda i, h: (i, 0, c0 + ML_HEADS + h)),
                  pl.BlockSpec(blk, lambda i, h: (i, 0, c0 + 2 * ML_HEADS + h)),
                  pl.BlockSpec((ift.shape[0], seq), lambda i, h: (0, i)),
                  pl.BlockSpec((1, 1, 2), lambda i, h: (h, 0, 0)),
                  pl.BlockSpec((ML_CONV, dh), lambda i, h: (0, h)),
                  vec, wspec, wspec, vec, vec],
        out_specs=pl.BlockSpec(blk, lambda i, h: (i, 0, h)),
        out_shape=jax.ShapeDtypeStruct((b, seq, ML_WIDTH), BF16),
        scratch_shapes=[pltpu.VMEM((seq, dh), F32),
                        pltpu.VMEM((seq, dh), BF16),
                        pltpu.VMEM((seq, dh), BF16),
                        pltpu.VMEM((dh, dh), F32),
                        pltpu.VMEM((1, dh), F32),
                        pltpu.VMEM((1, 1), F32),
                        pltpu.VMEM((8, seq), F32),
                        pltpu.VMEM((seq, 8), F32)],
        compiler_params=_cparams(("parallel", "parallel")),
        name="mlstm",
    )(proj3, proj3, proj3, ift, bif, conv_w, conv_b.reshape(1, -1), wq, wk,
      g_h.reshape(1, -1), skip.reshape(1, -1))


def _merge_body(x_ref, ya_ref, ys_ref, ym_ref, ga_ref, gs_ref, gm_ref, bg_ref,
                wa_ref, ws_ref, wm_ref, wo_ref, o_ref):
    d = x_ref.shape[1]
    bg = bg_ref[...]

    def branch(y_ref, w_ref, g_ref, k):
        gate = jax.nn.sigmoid(g_ref[...].astype(F32) + bg[:, k * d:(k + 1) * d])
        return gate * jnp.dot(y_ref[...], w_ref[0], preferred_element_type=F32)

    merged = branch(ya_ref, wa_ref, ga_ref, 0) + branch(ys_ref, ws_ref, gs_ref, 1) \
        + branch(ym_ref, wm_ref, gm_ref, 2)
    o_ref[...] = x_ref[...] + jnp.dot(merged.astype(BF16), wo_ref[0], preferred_element_type=F32)


def _merge(x2, ya, ys, ym, gates, b_gate, w_a, w_s, w_m, w_o, layer, tm):
    m, d = x2.shape
    tm = min(tm, m)
    row = lambda w: pl.BlockSpec((tm, w), lambda i: (i, 0))
    gate = lambda k: pl.BlockSpec((tm, d), lambda i: (i, k))
    wres = lambda w: _resident((1,) + w.shape[1:], lambda i: (layer, 0, 0))
    return pl.pallas_call(
        _merge_body,
        grid=(m // tm,),
        in_specs=[row(d), row(ya.shape[1]), row(ys.shape[1]), row(ym.shape[1]),
                  gate(0), gate(1), gate(2), _resident((1, 3 * d), lambda i: (0, 0)),
                  wres(w_a), wres(w_s), wres(w_m), wres(w_o)],
        out_specs=row(d),
        out_shape=jax.ShapeDtypeStruct((m, d), F32),
        compiler_params=_cparams(("parallel",)),
        name="merge",
    )(x2, ya, ys, ym, gates, gates, gates, b_gate.reshape(1, -1), w_a, w_s, w_m, w_o)


def _xattn_body(x_ref, g_ref, wq_ref, kv_ref, gq_ref, gk_ref, wo_ref, gn_ref, o_ref, hn_ref, q_s, att_s):
    d = x_ref.shape[1]
    dh = d // MEM_HEADS
    scale = dh ** -0.5
    hn = _rms(x_ref[...], g_ref[...]).astype(BF16)
    q_s[...] = jnp.dot(hn, wq_ref[0], preferred_element_type=F32)
    for h in range(MEM_HEADS):
        cols = slice(h * dh, (h + 1) * dh)
        qn = _rms(q_s[:, cols], gq_ref[...]).astype(BF16)
        kn = _rms(kv_ref[0, :, cols].astype(F32), gk_ref[...]).astype(BF16)
        s = lax.dot_general(qn, kn, _NT, preferred_element_type=F32) * scale
        p = jnp.exp(s - jnp.max(s, axis=-1, keepdims=True))
        l = jnp.sum(p, axis=-1, keepdims=True)
        v = kv_ref[0, :, d + h * dh:d + (h + 1) * dh]
        att_s[:, cols] = (jnp.dot(p.astype(BF16), v, preferred_element_type=F32) / l).astype(BF16)
    out = x_ref[...] + jnp.dot(att_s[...], wo_ref[0], preferred_element_type=F32)
    o_ref[...] = out
    hn_ref[...] = _rms(out, gn_ref[...]).astype(BF16)


def _xattn(x2, g, w_q, kv3, g_q, g_k, w_o, g_next, layer, seq, tm):
    m, d = x2.shape
    tm = min(tm, seq)
    per_seq = seq // tm
    wres = lambda w: _resident((1,) + w.shape[1:], lambda i: (layer, 0, 0))
    head = pl.BlockSpec((1, d // MEM_HEADS), lambda i: (0, 0))
    return pl.pallas_call(
        _xattn_body,
        grid=(m // tm,),
        in_specs=[pl.BlockSpec((tm, d), lambda i: (i, 0)),
                  pl.BlockSpec((1, d), lambda i: (0, 0)),
                  wres(w_q),
                  pl.BlockSpec((1,) + kv3.shape[1:], lambda i: (i // per_seq, 0, 0)),
                  head, head, wres(w_o),
                  pl.BlockSpec((1, d), lambda i: (0, 0))],
        out_specs=[pl.BlockSpec((tm, d), lambda i: (i, 0)), pl.BlockSpec((tm, d), lambda i: (i, 0))],
        out_shape=[jax.ShapeDtypeStruct((m, d), F32), jax.ShapeDtypeStruct((m, d), BF16)],
        scratch_shapes=[pltpu.VMEM((tm, d), F32), pltpu.VMEM((tm, d), BF16)],
        compiler_params=_cparams(("parallel",)),
        name="mem_attention",
    )(x2, g.reshape(1, d), w_q, kv3, g_q.reshape(1, -1), g_k.reshape(1, -1), w_o, g_next.reshape(1, d))


def _ffn_up_body(xa_ref, xb_ref, g_ref, wg_ref, wu_ref, wd_ref, a_ref, wdb_ref, hn_ref, *, n_fill):
    rows = hn_ref.shape[0]
    s = pl.program_id(1)

    @pl.when(s < n_fill)
    def _():
        _fill_norm(xa_ref, xb_ref, g_ref, hn_ref, s)

    @pl.when(s >= n_fill)
    def _():
        wdb_ref[...] = wd_ref[0].astype(BF16)
        wg, wu = wg_ref[0].astype(BF16), wu_ref[0].astype(BF16)
        for r in range(0, rows, DOT_ROWS):
            hn = hn_ref[r:r + DOT_ROWS, :]
            gate = jnp.dot(hn, wg, preferred_element_type=F32)
            up = jnp.dot(hn, wu, preferred_element_type=F32)
            a_ref[r:r + DOT_ROWS, :] = (gate * jax.nn.sigmoid(gate) * up).astype(a_ref.dtype)


def _ffn_up(x2, g, w_gu, w_down, layer):
    m, d = x2.shape
    f = w_gu.shape[2] // 2
    rows = min(ROW_GROUP, m)
    n_fill = rows // NORM_ROWS
    n_tiles = f // COL_TILE
    groups = m // rows
    slab = f // (groups * n_tiles)
    assert slab * groups * n_tiles == f and slab % BF16_ROWS == 0
    tile = lambda s: jnp.clip(s - n_fill, 0, n_tiles - 1)
    return pl.pallas_call(
        functools.partial(_ffn_up_body, n_fill=n_fill),
        grid=(groups, n_fill + n_tiles),
        in_specs=_fill_specs(d, n_fill) + [
                  pl.BlockSpec((1, d), lambda h, s: (0, 0)),
                  pl.BlockSpec((1, d, COL_TILE), lambda h, s: (layer, 0, tile(s))),
                  pl.BlockSpec((1, d, COL_TILE), lambda h, s: (layer, 0, n_tiles + tile(s))),
                  pl.BlockSpec((1, slab, d), lambda h, s: (layer, h * n_tiles + tile(s), 0))],
        out_specs=[pl.BlockSpec((rows, COL_TILE), lambda h, s: (h, tile(s))),
                   pl.BlockSpec((slab, d), lambda h, s: (h * n_tiles + tile(s), 0))],
        out_shape=[jax.ShapeDtypeStruct((m, f), BF16), jax.ShapeDtypeStruct((f, d), BF16)],
        scratch_shapes=[pltpu.VMEM((rows, d), BF16)],
        compiler_params=_cparams(("arbitrary", "arbitrary")),
        name="swiglu_up",
    )(x2, x2, g.reshape(1, d), w_gu, w_gu, w_down)


def _ffn_down_body(x_ref, a_ref, w_ref, *rest):
    if len(rest) == 3:
        gn_ref, o_ref, hn_ref = rest
    else:
        (o_ref,), gn_ref, hn_ref = rest, None, None
    out = x_ref[...] + jnp.dot(a_ref[...], w_ref[...], preferred_element_type=F32)
    o_ref[...] = out
    if hn_ref is not None:
        hn_ref[...] = _rms(out, gn_ref[...]).astype(BF16)


def _ffn_down(x2, act, w_down_b, tm, g_next=None):
    m, d = x2.shape
    f = act.shape[1]
    tm = min(tm, m)
    row = pl.BlockSpec((tm, d), lambda i: (i, 0))
    in_specs = [row, pl.BlockSpec((tm, f), lambda i: (i, 0)), _resident((f, d), lambda i: (0, 0))]
    args = [x2, act, w_down_b]
    out_specs, out_shape = row, jax.ShapeDtypeStruct((m, d), F32)
    if g_next is not None:
        in_specs.append(pl.BlockSpec((1, d), lambda i: (0, 0)))
        args.append(g_next.reshape(1, d))
        out_specs, out_shape = [row, row], [out_shape, jax.ShapeDtypeStruct((m, d), BF16)]
    return pl.pallas_call(
        _ffn_down_body,
        grid=(m // tm,),
        in_specs=in_specs,
        out_specs=out_specs,
        out_shape=out_shape,
        compiler_params=_cparams(("parallel",)),
        name="swiglu_down",
    )(*args)


def kernel(x, mem, g_mem, norm_mix, w_in, b_gate, g_qa, g_ka, rel_bias, lam_re, lam_im, log_dt, b_re, b_im, c_re, c_im, d_skip, w_glu, b_glu, conv_w, conv_b, wq_m, wk_m, b_i, b_f, g_hm, skip_m, w_br_a, w_br_s, w_br_m, w_out, norm_x, w_xq, w_xkv, g_xq, g_xk, w_xo, norm_ffn, w_gu, w_down):
    b, seq, d = x.shape
    depth = w_in.shape[0]
    tokens = b * seq
    assert seq % ATT_QB == 0 and seq % ML_T == 0 and seq % SSM_T == 0
    assert tokens % min(ROW_GROUP, tokens) == 0

    w_glu_t = jnp.swapaxes(w_glu, 1, 2).astype(BF16)
    wq_b, wk_b = wq_m.astype(BF16), wk_m.astype(BF16)
    wa_b, ws_b, wm_b, wo_b = (w.astype(BF16) for w in (w_br_a, w_br_s, w_br_m, w_out))
    wxq_b, wxo_b = w_xq.astype(BF16), w_xo.astype(BF16)
    w_in_t = jnp.swapaxes(w_in, 1, 2)
    s5_params = _s5_params(lam_re, lam_im, log_dt, b_re, b_im, c_re, c_im)
    att_bias = _att_bias(rel_bias.reshape(depth * ATT_HEADS, -1))
    bif_all = jnp.stack([b_i, b_f], axis=-1).reshape(depth, ML_HEADS, 1, 2)

    x2 = x.reshape(tokens, d)
    mem2 = mem.reshape(b * mem.shape[1], d)
    hn_mix = x2
    for i in range(depth):
        proj, u3, gates, ift = _in_proj(hn_mix, norm_mix[i], w_in_t, i)
        proj3 = proj.reshape(b, seq, -1)

        ya = _chunk_attention(proj3, g_qa[i], g_ka[i], att_bias, i)

        yt3 = _s5_scan(u3, s5_params, i, cpb=seq // SSM_T)
        ys = _s5_post(yt3, u3, d_skip[i], w_glu_t, i, b_glu[i], chunks=8)

        ym = _mlstm(proj3, ift, bif_all[i], conv_w[i], conv_b[i], wq_b, wk_b, i, g_hm[i], skip_m[i],
                    3 * ATT_WIDTH)

        x2 = _merge(x2, ya.reshape(tokens, -1), ys, ym.reshape(tokens, -1), gates, b_gate[i],
                    wa_b, ws_b, wm_b, wo_b, i, tm=256)

        kv = _norm_proj(mem2, g_mem, w_xkv, i, tm=1024, tn=512)
        x2, hn_ffn = _xattn(x2, norm_x[i], wxq_b, kv.reshape(b, mem.shape[1], 2 * d), g_xq[i], g_xk[i],
                            wxo_b, norm_ffn[i], i, seq, tm=512)

        act, w_down_b = _ffn_up(hn_ffn, norm_ffn[i], w_gu, w_down, i)
        if i + 1 < depth:
            x2, hn_mix = _ffn_down(x2, act, w_down_b, tm=256, g_next=norm_mix[i + 1])
        else:
            x2 = _ffn_down(x2, act, w_down_b, tm=256)
    return x2.reshape(b, seq, d)
```

```python
import functools

import jax
import jax.numpy as jnp
from jax import lax
from jax.experimental import pallas as pl
from jax.experimental.pallas import tpu as pltpu

F32 = jnp.float32
BF16 = jnp.bfloat16

EPS = 1e-6
LANES = 128
BF16_ROWS = 16
ATT_CHUNK = 64
ATT_HEADS = 8
ATT_HEAD_DIM = 128
ATT_WIDTH = ATT_HEADS * ATT_HEAD_DIM
ATT_LEFT = 8
REL_CLIP = 256
ATT_QB = 256
ATT_PAD = ATT_LEFT * ATT_CHUNK
ATT_KW = ATT_QB + ATT_PAD
ATT_EXT = 1024
MASK_NEG = -1e30
SSM_GROUP = 16
SSM_GROUPS = 48
SSM_WIDTH = SSM_GROUP * SSM_GROUPS
SSM_STATE = 64
SSM_T = LANES
SSM_SUB = 16
SSM_GROUPS_PER_STEP = 8
ML_HEADS = 4
ML_HEAD_DIM = 256
ML_WIDTH = ML_HEADS * ML_HEAD_DIM
ML_CONV = 4
ML_T = 256
ML_NEG = -1e30
MEM_HEADS = 4
ROW_GROUP = 4096
NORM_ROWS = 512
COL_TILE = 256
DOT_ROWS = 1024

VMEM_LIMIT = 56 * 1024 * 1024

_NT = (((1,), (1,)), ((), ()))
_TN = (((0,), (0,)), ((), ()))


def _cparams(sem):
    return pltpu.CompilerParams(dimension_semantics=sem, vmem_limit_bytes=VMEM_LIMIT)


def _rms(x, g):
    return x * lax.rsqrt(jnp.mean(x * x, axis=-1, keepdims=True) + EPS) * g


def _resident(shape, index_map):
    return pl.BlockSpec(shape, index_map, pipeline_mode=pl.Buffered(1))


def _fill_specs(d, n_fill):
    row_blk = lambda h, s: h * n_fill + jnp.minimum(s, n_fill - 1)
    return [pl.BlockSpec((NORM_ROWS, d // 2), lambda h, s, c=c: (row_blk(h, s), c)) for c in range(2)]


def _fill_norm(xa_ref, xb_ref, g_ref, hn_ref, step):
    r0 = pl.multiple_of(step * NORM_ROWS, NORM_ROWS)
    x = jnp.concatenate([xa_ref[...], xb_ref[...]], axis=1)
    if x.dtype != BF16:
        x = _rms(x, g_ref[...]).astype(BF16)
    hn_ref[pl.ds(r0, NORM_ROWS), :] = x


def _in_proj_body(xa_ref, xb_ref, g_ref, wm_ref, wg_ref, we_ref, p_ref, u_ref, gt_ref, if_ref, hn_ref,
                  *, n_fill, n_att, n_ssm, n_ml, if_cols):
    rows = hn_ref.shape[0]
    s = pl.program_id(1)
    t = s - n_fill
    t_gate = t - (n_att + n_ssm + n_ml)

    @pl.when(s < n_fill)
    def _():
        _fill_norm(xa_ref, xb_ref, g_ref, hn_ref, s)

    def project(wt, o_ref):
        for r in range(0, rows, DOT_ROWS):
            o_ref[r:r + DOT_ROWS, :] = lax.dot_general(
                hn_ref[r:r + DOT_ROWS, :], wt, _NT, preferred_element_type=F32).astype(o_ref.dtype)

    in_ssm = (t >= n_att) & (t < n_att + n_ssm)

    @pl.when((t >= 0) & (t_gate < 0) & jnp.logical_not(in_ssm))
    def _():
        project(wm_ref[0].astype(BF16), p_ref)

    @pl.when(in_ssm)
    def _():
        wt = jnp.concatenate([wm_ref[0], wg_ref[0][:BF16_ROWS]], axis=0).astype(BF16)
        for r in range(0, rows, DOT_ROWS):
            ut = lax.dot_general(wt, hn_ref[r:r + DOT_ROWS, :], _NT, preferred_element_type=F32)
            u_ref[:, r // SSM_T:(r + DOT_ROWS) // SSM_T, :] = ut[:COL_TILE, :].reshape(
                COL_TILE, DOT_ROWS // SSM_T, SSM_T)
            if_ref[:, r:r + DOT_ROWS] = ut[COL_TILE:, :]

    @pl.when(t_gate >= 0)
    def _():
        wt = jnp.concatenate([wg_ref[0][if_cols:], we_ref[0]], axis=0)
        project(wt.astype(BF16), gt_ref)


def _in_proj(x2, g, w_in_t, layer):
    m, d = x2.shape
    rows = min(ROW_GROUP, m)
    n_fill = rows // NORM_ROWS
    n_att = 3 * ATT_WIDTH // COL_TILE
    n_ssm = SSM_WIDTH // COL_TILE
    n_ml = 3 * ML_WIDTH // COL_TILE
    n_main = n_att + n_ssm + n_ml
    if_cols = 2 * ML_HEADS
    gate_w = w_in_t.shape[1] - n_main * COL_TILE - if_cols
    n_gate = gate_w // COL_TILE
    assert gate_w % COL_TILE == 0 and if_cols <= BF16_ROWS and COL_TILE % if_cols == 0
    ext0 = (n_main + 1) * COL_TILE // if_cols
    ext_step = COL_TILE // if_cols

    def t_of(s):
        return s - n_fill

    def p_col(s):
        t = t_of(s)
        return jnp.where(t < n_att, jnp.clip(t, 0, n_att - 1), jnp.clip(t - n_ssm, n_att, n_att + n_ml - 1))

    def g_idx(s):
        return jnp.clip(t_of(s) - n_main, 0, n_gate - 1)

    body = functools.partial(_in_proj_body, n_fill=n_fill, n_att=n_att, n_ssm=n_ssm, n_ml=n_ml,
                             if_cols=if_cols)
    return pl.pallas_call(
        body,
        grid=(m // rows, n_fill + n_main + n_gate),
        in_specs=_fill_specs(d, n_fill) + [
                  pl.BlockSpec((1, d), lambda h, s: (0, 0)),
                  pl.BlockSpec((1, COL_TILE, d), lambda h, s: (layer, jnp.clip(t_of(s), 0, n_main - 1), 0)),
                  pl.BlockSpec((1, COL_TILE, d), lambda h, s: (layer, n_main + g_idx(s), 0)),
                  pl.BlockSpec((1, if_cols, d), lambda h, s: (layer, ext0 + ext_step * g_idx(s), 0))],
        out_specs=[pl.BlockSpec((rows, COL_TILE), lambda h, s: (h, p_col(s))),
                   pl.BlockSpec((COL_TILE, rows // SSM_T, SSM_T),
                                lambda h, s: (jnp.clip(t_of(s) - n_att, 0, n_ssm - 1), h, 0)),
                   pl.BlockSpec((rows, COL_TILE), lambda h, s: (h, g_idx(s))),
                   pl.BlockSpec((BF16_ROWS, rows), lambda h, s: (0, h))],
        out_shape=[jax.ShapeDtypeStruct((m, (n_att + n_ml) * COL_TILE), BF16),
                   jax.ShapeDtypeStruct((SSM_WIDTH, m // SSM_T, SSM_T), F32),
                   jax.ShapeDtypeStruct((m, gate_w), BF16),
                   jax.ShapeDtypeStruct((BF16_ROWS, m), F32)],
        scratch_shapes=[pltpu.VMEM((rows, d), BF16)],
        compiler_params=_cparams(("parallel", "arbitrary")),
        name="in_proj",
    )(x2, x2, g.reshape(1, d), w_in_t, w_in_t, w_in_t)


def _norm_proj_body(x_ref, g_ref, w_ref, o_ref, hn_ref):
    @pl.when(pl.program_id(1) == 0)
    def _():
        hn_ref[...] = _rms(x_ref[...], g_ref[...]).astype(BF16)

    o_ref[...] = jnp.dot(hn_ref[...], w_ref[0].astype(BF16),
                         preferred_element_type=F32).astype(o_ref.dtype)


def _norm_proj(x2, g, w, layer, tm, tn):
    m, d = x2.shape
    n = w.shape[2]
    tm = min(tm, m)
    return pl.pallas_call(
        _norm_proj_body,
        grid=(m // tm, n // tn),
        in_specs=[pl.BlockSpec((tm, d), lambda i, j: (i, 0)),
                  pl.BlockSpec((1, d), lambda i, j: (0, 0)),
                  pl.BlockSpec((1, d, tn), lambda i, j: (layer, 0, j))],
        out_specs=pl.BlockSpec((tm, tn), lambda i, j: (i, j)),
        out_shape=jax.ShapeDtypeStruct((m, n), BF16),
        scratch_shapes=[pltpu.VMEM((tm, d), BF16)],
        compiler_params=_cparams(("parallel", "arbitrary")),
        name="norm_proj",
    )(x2, g.reshape(1, d), w)


def _att_bias_body(ext_ref, o_ref):
    ext = ext_ref[0]
    base = pltpu.roll(ext, ATT_EXT - (ATT_QB - 1), axis=1)
    slab = jnp.broadcast_to(base, (ATT_QB, ATT_EXT))
    tab = pltpu.roll(slab, 0, 1, stride=1, stride_axis=0)[:, :ATT_KW]
    qc = lax.broadcasted_iota(jnp.int32, (ATT_QB, ATT_KW), 0) // ATT_CHUNK
    kc = lax.broadcasted_iota(jnp.int32, (ATT_QB, ATT_KW), 1) // ATT_CHUNK
    o_ref[0] = jnp.where((kc >= qc) & (kc <= qc + ATT_LEFT), tab, MASK_NEG)


def _att_bias(rel_bias):
    h = rel_bias.shape[0]
    lead = ATT_QB - 1 + ATT_PAD - REL_CLIP
    tail = ATT_EXT - lead - rel_bias.shape[1]
    ext = jnp.concatenate([jnp.broadcast_to(rel_bias[:, :1], (h, lead)), rel_bias,
                           jnp.broadcast_to(rel_bias[:, -1:], (h, tail))], axis=1)
    return pl.pallas_call(
        _att_bias_body,
        grid=(h,),
        in_specs=[pl.BlockSpec((1, 1, ATT_EXT), lambda i: (i, 0, 0))],
        out_specs=pl.BlockSpec((1, ATT_QB, ATT_KW), lambda i: (i, 0, 0)),
        out_shape=jax.ShapeDtypeStruct((h, ATT_QB, ATT_KW), F32),
        compiler_params=_cparams(("parallel",)),
        name="att_bias",
    )(ext.reshape(h, 1, ATT_EXT))


def _attn_body(q_ref, k_ref, v_ref, gq_ref, gk_ref, bias_ref, o_ref, kpad, vpad, *, seq):
    kpad[0:ATT_PAD, :] = jnp.zeros((ATT_PAD, ATT_HEAD_DIM), BF16)
    vpad[0:ATT_PAD, :] = jnp.zeros((ATT_PAD, ATT_HEAD_DIM), BF16)
    kpad[ATT_PAD:, :] = _rms(k_ref[0].astype(F32), gk_ref[...]).astype(BF16)
    vpad[ATT_PAD:, :] = v_ref[0]
    bias = bias_ref[0]
    scale = ATT_HEAD_DIM ** -0.5
    for qb in range(seq // ATT_QB):
        r0 = qb * ATT_QB
        qn = _rms(q_ref[0, r0:r0 + ATT_QB, :].astype(F32), gq_ref[...]).astype(BF16)
        kw = kpad[r0:r0 + ATT_KW, :]
        s = lax.dot_general(qn, kw, _NT, preferred_element_type=F32) * scale + bias
        if r0 < ATT_PAD:
            col = lax.broadcasted_iota(jnp.int32, (ATT_QB, ATT_KW), 1)
            s = jnp.where(col + r0 >= ATT_PAD, s, MASK_NEG)
        m = jnp.max(s, axis=-1, keepdims=True)
        p = jnp.exp(s - m)
        l = jnp.sum(p, axis=-1, keepdims=True)
        o = jnp.dot(p.astype(BF16), vpad[r0:r0 + ATT_KW, :], preferred_element_type=F32)
        o_ref[0, r0:r0 + ATT_QB, :] = (o / l).astype(o_ref.dtype)


def _chunk_attention(proj3, gq, gk, bias, layer):
    b, seq, _ = proj3.shape
    bias0 = layer * ATT_HEADS
    blk = (1, seq, ATT_HEAD_DIM)
    return pl.pallas_call(
        functools.partial(_attn_body, seq=seq),
        grid=(b, ATT_HEADS),
        in_specs=[pl.BlockSpec(blk, lambda i, h: (i, 0, h)),
                  pl.BlockSpec(blk, lambda i, h: (i, 0, ATT_HEADS + h)),
                  pl.BlockSpec(blk, lambda i, h: (i, 0, 2 * ATT_HEADS + h)),
                  pl.BlockSpec((1, ATT_HEAD_DIM), lambda i, h: (0, 0)),
                  pl.BlockSpec((1, ATT_HEAD_DIM), lambda i, h: (0, 0)),
                  pl.BlockSpec((1, ATT_QB, ATT_KW), lambda i, h: (bias0 + h, 0, 0))],
        out_specs=pl.BlockSpec(blk, lambda i, h: (i, 0, h)),
        out_shape=jax.ShapeDtypeStruct((b, seq, ATT_WIDTH), BF16),
        scratch_shapes=[pltpu.VMEM((seq + ATT_PAD, ATT_HEAD_DIM), BF16),
                        pltpu.VMEM((seq + ATT_PAD, ATT_HEAD_DIM), BF16)],
        compiler_params=_cparams(("parallel", "parallel")),
        name="chunk_attention",
    )(proj3, proj3, proj3, gq.reshape(1, -1), gk.reshape(1, -1), bias)


def _cmul(xr, xi, yr, yi):
    return xr * yr - xi * yi, xr * yi + xi * yr


def _s5_perm_tables(pmat, pmat_t):
    width = SSM_GROUP * SSM_T
    blk = SSM_GROUP * SSM_SUB
    sub_bits = SSM_SUB.bit_length() - 1
    t_bits = SSM_T.bit_length() - 1
    blk_bits = blk.bit_length() - 1
    for jb in range(width // blk):
        row = lax.broadcasted_iota(jnp.int32, (width, blk), 0)
        col = lax.broadcasted_iota(jnp.int32, (width, blk), 1) + jb * blk
        s0 = col & (SSM_SUB - 1)
        src = (((col & (blk - 1)) >> sub_bits) << t_bits) + ((col >> blk_bits) << sub_bits) + s0
        pmat[:, jb * blk:(jb + 1) * blk] = jnp.where(row == src, 1.0, 0.0).astype(BF16)
        src_t = (((col & (SSM_T - 1)) >> sub_bits) << blk_bits) + ((col >> t_bits) << sub_bits) + s0
        pmat_t[:, jb * blk:(jb + 1) * blk] = jnp.where(row == src_t, 1.0, 0.0).astype(BF16)


def _s5_group(gi, lr2_ref, li2_ref, ldt_ref, brt_ref, bit_ref, crx_ref, cix_ref,
              xp, yp, *, cpb, nch):
    t_len, sub, n_ch, n_st = SSM_T, SSM_SUB, SSM_GROUP, SSM_STATE
    n_sub = t_len // sub
    dt = jnp.exp(ldt_ref[gi])
    lr2, li2 = lr2_ref[gi], li2_ref[gi]
    first_half = lax.broadcasted_iota(jnp.int32, (1, 2 * n_st), 1) < n_st

    mag = jnp.exp(lr2 * dt)
    ar, ai = mag * jnp.cos(li2 * dt), mag * jnp.sin(li2 * dt)
    den = lr2 * lr2 + li2 * li2
    nr, ni = ar - 1.0, ai
    zr, zi = (nr * lr2 + ni * li2) / den, (ni * lr2 - nr * li2) / den
    bbr = zr * brt_ref[gi] - zi * bit_ref[gi]
    bbi = zr * bit_ref[gi] + zi * brt_ref[gi]

    sub_bits = sub.bit_length() - 1
    t_bits = t_len.bit_length() - 1
    sq = [(ar, ai)]
    while (1 << (len(sq) - 1)) * 2 < t_len * cpb:
        sq.append(_cmul(*sq[-1], *sq[-1]))

    def power(e, base):
        pr = pi = None
        for k in range(sub_bits):
            bit = ((e >> k) & 1) == 1
            fr, fi = jnp.where(bit, base[k][0], 1.0), jnp.where(bit, base[k][1], 0.0)
            pr, pi = (fr, fi) if pr is None else _cmul(pr, pi, fr, fi)
        return pr, pi

    def rotate(x, a):
        return x * a[0] + pltpu.roll(x, n_st, axis=1) * jnp.where(first_half, -a[1], a[1])

    e_row = lax.broadcasted_iota(jnp.int32, (sub, 2 * n_st), 0)
    qr, qi = power(sub - 1 - e_row, sq)
    f_r = jnp.where(first_half, bbr, bbi)
    f_i = jnp.where(first_half, -bbi, bbr)
    fmat = jnp.concatenate([(qr * f_r[mch:mch + 1, :] + qi * f_i[mch:mch + 1, :]).astype(BF16)
                            for mch in range(n_ch)], axis=0)

    sq_rows = jnp.concatenate([v for k in range(sub_bits) for v in sq[k]], axis=0)
    sq_cols = sq_rows.T
    base_c = [(sq_cols[:, 2 * k:2 * k + 1], sq_cols[:, 2 * k + 1:2 * k + 2]) for k in range(sub_bits)]
    tau = lax.broadcasted_iota(jnp.int32, (2 * n_st, n_ch * sub), 1) & (sub - 1)
    p0r, p0i = power(tau, base_c)
    p1r, p1i = _cmul(p0r, p0i, *base_c[0])
    top = lax.broadcasted_iota(jnp.int32, (2 * n_st, 1), 0) < n_st
    crx, cix = crx_ref[gi], cix_ref[gi]

    def c_times(pr, pi):
        return crx * jnp.where(top, pr, -pi) + cix * jnp.where(top, -pi, -pr)

    emat = c_times(p1r, p1i).astype(BF16)

    kflat = jnp.dot(f_r, c_times(p0r, p0i), preferred_element_type=F32, precision=lax.Precision.HIGHEST)
    lane_t = lax.broadcasted_iota(jnp.int32, (sub, n_ch * sub), 1) & (sub - 1)
    causal = lane_t >= lax.broadcasted_iota(jnp.int32, (sub, n_ch * sub), 0)
    bdiag = []
    for mch in range(n_ch):
        tz = pltpu.roll(jnp.broadcast_to(kflat[mch:mch + 1, :], (sub, n_ch * sub)), 0, 1,
                        stride=1, stride_axis=0)
        bdiag.append(jnp.where(causal, tz, 0.0).astype(BF16))
    bdiag = jnp.concatenate(bdiag, axis=0)

    rows = pl.ds(pl.multiple_of(gi * nch, nch), nch)
    xg = xp[rows, :]
    blk = n_ch * sub
    x2 = jnp.concatenate([xg[:, j * blk:(j + 1) * blk] for j in range(n_sub)], axis=0)
    s2 = jnp.dot(x2, fmat, preferred_element_type=F32)
    a_sub = sq[sub_bits]

    def run(state):
        entering = []
        for j in range(n_sub):
            entering.append(state)
            state = rotate(state, a_sub) + s2[j * nch:(j + 1) * nch]
        return entering, state

    _, s_loc = run(jnp.zeros((nch, 2 * n_st), F32))

    cidx = lax.broadcasted_iota(jnp.int32, (nch, 1), 0) % cpb
    xin = jnp.where(cidx >= 1, pltpu.roll(s_loc, 1, axis=0), 0.0)
    d = 1
    while d < cpb:
        sh = pltpu.roll(xin, d, axis=0)
        xin = xin + jnp.where(cidx >= d, rotate(sh, sq[t_bits + d.bit_length() - 1]), 0.0)
        d *= 2

    entering, _ = run(xin)
    xin2 = jnp.concatenate(entering, axis=0).astype(BF16)
    y2 = (jnp.dot(x2, bdiag, preferred_element_type=F32)
          + jnp.dot(xin2, emat, preferred_element_type=F32))
    yp[rows, :] = jnp.concatenate([y2[j * nch:(j + 1) * nch] for j in range(n_sub)], axis=1).astype(BF16)


def _s5_body(u_ref, lr2_ref, li2_ref, ldt_ref, brt_ref, bit_ref, crx_ref, cix_ref, yt_ref,
             pmat, pmat_t, xp, yp, *, cpb):
    n_ch, t_len = SSM_GROUP, SSM_T
    nch = u_ref.shape[1]
    gp = u_ref.shape[0] // n_ch

    @pl.when(pl.program_id(0) == 0)
    def _():
        _s5_perm_tables(pmat, pmat_t)

    xall = jnp.concatenate(
        [jnp.concatenate([u_ref[g * n_ch + mch].astype(BF16) for mch in range(n_ch)], axis=1)
         for g in range(gp)], axis=0)
    xp[...] = jnp.dot(xall, pmat[...], preferred_element_type=F32).astype(BF16)

    def group(gi, carry):
        _s5_group(gi, lr2_ref, li2_ref, ldt_ref, brt_ref, bit_ref, crx_ref, cix_ref,
                  xp, yp, cpb=cpb, nch=nch)
        return carry

    lax.fori_loop(0, gp, group, 0, unroll=4)

    yall = jnp.dot(yp[...], pmat_t[...], preferred_element_type=F32)
    for g in range(gp):
        for nn in range(n_ch):
            yt_ref[g * n_ch + nn] = yall[g * nch:(g + 1) * nch, nn * t_len:(nn + 1) * t_len]


def _s5_params(lam_re, lam_im, log_dt, b_re, b_im, c_re, c_im):
    lg, p = lam_re.shape[0] * lam_re.shape[1], SSM_STATE
    flat = lambda a: a.reshape((lg,) + a.shape[2:])
    dup = lambda a, axis: jnp.concatenate([a, a], axis=axis)
    lr2 = dup(flat(lam_re), 1).reshape(lg, 1, 2 * p)
    li2 = dup(flat(lam_im), 1).reshape(lg, 1, 2 * p)
    ldt = log_dt.reshape(lg, 1, 1)
    brt = dup(jnp.swapaxes(flat(b_re), 1, 2), 2)
    bit = dup(jnp.swapaxes(flat(b_im), 1, 2), 2)
    crx = jnp.repeat(dup(jnp.swapaxes(flat(c_re), 1, 2), 1), SSM_SUB, axis=2)
    cix = jnp.repeat(dup(jnp.swapaxes(flat(c_im), 1, 2), 1), SSM_SUB, axis=2)
    return lr2, li2, ldt, brt, bit, crx, cix


def _s5_scan(u3, params, layer, *, cpb):
    g, p, n, gp = SSM_GROUPS, SSM_STATE, SSM_GROUP, SSM_GROUPS_PER_STEP
    nch = u3.shape[1]
    width = n * SSM_T
    blk = n * SSM_SUB
    step0 = layer * (g // gp)
    spec = lambda *s: pl.BlockSpec((gp,) + s, lambda i: (step0 + i, 0, 0))
    return pl.pallas_call(
        functools.partial(_s5_body, cpb=cpb),
        grid=(g // gp,),
        in_specs=[pl.BlockSpec((gp * n, nch, SSM_T), lambda i: (i, 0, 0)),
                  spec(1, 2 * p), spec(1, 2 * p), spec(1, 1),
                  spec(n, 2 * p), spec(n, 2 * p), spec(2 * p, blk), spec(2 * p, blk)],
        out_specs=pl.BlockSpec((gp * n, nch, SSM_T), lambda i: (i, 0, 0)),
        out_shape=jax.ShapeDtypeStruct((SSM_WIDTH, nch, SSM_T), F32),
        scratch_shapes=[pltpu.VMEM((width, width), BF16),
                        pltpu.VMEM((width, width), BF16),
                        pltpu.VMEM((gp * nch, width), BF16),
                        pltpu.VMEM((gp * nch, width), BF16)],
        compiler_params=_cparams(("arbitrary",)),
        name="s5_scan",
    )(u3, *params)


def _s5_post_body(yt_ref, u_ref, dsk_ref, wgt_ref, bg_ref, o_ref):
    w, nc, t_len = yt_ref.shape
    yt = yt_ref[...].reshape(w, nc * t_len)
    ut = u_ref[...].reshape(w, nc * t_len)
    y = jax.nn.gelu(yt + dsk_ref[...] * ut)
    z = jnp.dot(wgt_ref[0], y.astype(BF16), preferred_element_type=F32) + bg_ref[...]
    o_ref[...] = (y * jax.nn.sigmoid(z)).T.astype(o_ref.dtype)


def _s5_post(yt3, u3, d_skip, w_glu_t, layer, b_glu, chunks):
    w, nch, t_len = yt3.shape
    chunks = min(chunks, nch)
    return pl.pallas_call(
        _s5_post_body,
        grid=(nch // chunks,),
        in_specs=[pl.BlockSpec((w, chunks, t_len), lambda i: (0, i, 0)),
                  pl.BlockSpec((w, chunks, t_len), lambda i: (0, i, 0)),
                  pl.BlockSpec((w, 1), lambda i: (0, 0)),
                  pl.BlockSpec((1, w, w), lambda i: (layer, 0, 0)),
                  pl.BlockSpec((w, 1), lambda i: (0, 0))],
        out_specs=pl.BlockSpec((chunks * t_len, w), lambda i: (i, 0)),
        out_shape=jax.ShapeDtypeStruct((nch * t_len, w), BF16),
        compiler_params=_cparams(("parallel",)),
        name="s5_post",
    )(yt3, u3, d_skip.reshape(w, 1), w_glu_t, b_glu.reshape(w, 1))


def _mlstm_body(xm_ref, vm_ref, om_ref, if_ref, bif_ref, cw_ref, cb_ref, wq_ref, wk_ref,
                gh_ref, sk_ref, o_ref, xc_s, q_s, k_s, cmat, nvec, mrun, ifr_s, ifc_s, *, seq):
    t_len, dh = ML_T, ML_HEAD_DIM
    head = pl.program_id(1)
    xm = xm_ref[0].astype(F32)
    rowi = lax.broadcasted_iota(jnp.int32, (seq, 1), 0)
    cw = cw_ref[...]
    acc = cw[ML_CONV - 1:ML_CONV, :] * xm + cb_ref[...]
    for j in range(1, ML_CONV):
        shifted = jnp.where(rowi >= j, pltpu.roll(xm, j, axis=0), 0.0)
        acc = acc + cw[ML_CONV - 1 - j:ML_CONV - j, :] * shifted
    xc = acc * jax.nn.sigmoid(acc)
    xc_s[...] = xc
    xcb = xc.astype(BF16)
    q_s[...] = jnp.dot(xcb, wq_ref[0, 0], preferred_element_type=F32).astype(BF16)
    k_s[...] = (jnp.dot(xcb, wk_ref[0, 0], preferred_element_type=F32) * (dh ** -0.5)).astype(BF16)

    cmat[...] = jnp.zeros_like(cmat)
    nvec[...] = jnp.zeros_like(nvec)
    mrun[...] = jnp.full_like(mrun, ML_NEG)
    b_i, b_f = bif_ref[0][:, 0:1], bif_ref[0][:, 1:2]
    ifr_s[...] = jnp.concatenate([if_ref[pl.ds(head, 1), :], if_ref[pl.ds(ML_HEADS + head, 1), :],
                                  jnp.zeros((6, seq), F32)], axis=0)
    ifc_s[...] = ifr_s[...].T
    iota_r = lax.broadcasted_iota(jnp.int32, (t_len, t_len), 0)
    iota_c = lax.broadcasted_iota(jnp.int32, (t_len, t_len), 1)
    tri = iota_c <= iota_r
    tri_t = iota_r <= iota_c

    def step(c, carry):
        r0 = pl.multiple_of(c * t_len, t_len)
        rows = pl.ds(r0, t_len)
        ifr = ifr_s[:, rows]
        ifc = ifc_s[rows, :]
        i_col, i_row = ifc[:, 0:1] + b_i, ifr[0:1, :] + b_i
        lf_col = jax.nn.log_sigmoid(ifc[:, 1:2] + b_f)
        lf_row = jax.nn.log_sigmoid(ifr[1:2, :] + b_f)
        bcum_col = jnp.sum(jnp.where(tri, lf_row, 0.0), axis=1, keepdims=True)
        bcum_row = jnp.sum(jnp.where(tri_t, lf_col, 0.0), axis=0, keepdims=True)
        b_last = jnp.sum(lf_row, axis=1, keepdims=True)
        m_prev = mrun[...]
        dmat = jnp.where(tri, bcum_col - bcum_row + i_row, -jnp.inf)
        inter = bcum_col + m_prev
        m_row = jnp.maximum(jnp.max(dmat, axis=1, keepdims=True), inter)
        w_intra = jnp.exp(dmat - m_row)
        w_inter = jnp.exp(inter - m_row)
        qq, kk, vv = q_s[rows, :], k_s[rows, :], vm_ref[0, rows, :]
        s = lax.dot_general(qq, kk, _NT, preferred_element_type=F32) * w_intra
        cm = cmat[...]
        num = (jnp.dot(s.astype(BF16), vv, preferred_element_type=F32)
               + w_inter * lax.dot_general(qq, cm.astype(BF16), _NT, preferred_element_type=F32))
        den = (jnp.sum(s, axis=1, keepdims=True)
               + w_inter * jnp.sum(qq.astype(F32) * nvec[...], axis=1, keepdims=True))
        h = num / jnp.maximum(jnp.abs(den), jnp.exp(-m_row))
        g_col = b_last - bcum_col + i_col
        m_new = jnp.maximum(b_last + m_prev, jnp.max(g_col, axis=0, keepdims=True))
        wg = jnp.exp(g_col - m_new)
        decay = jnp.exp(b_last + m_prev - m_new)
        vw = (vv.astype(F32) * wg).astype(BF16)
        cmat[...] = decay * cm + lax.dot_general(vw, kk, _TN, preferred_element_type=F32)
        nvec[...] = decay * nvec[...] + jnp.sum(wg * kk.astype(F32), axis=0, keepdims=True)
        mrun[...] = m_new
        hn = _rms(h, gh_ref[...]) + sk_ref[...] * xc_s[rows, :]
        o_ref[0, rows, :] = (jax.nn.sigmoid(om_ref[0, rows, :].astype(F32)) * hn).astype(o_ref.dtype)
        return carry

    lax.fori_loop(0, seq // t_len, step, 0, unroll=2)


def _mlstm(proj3, ift, bif, conv_w, conv_b, wq, wk, layer, g_h, skip, col0):
    b, seq, _ = proj3.shape
    dh = ML_HEAD_DIM
    c0 = col0 // dh
    blk = (1, seq, dh)
    vec = pl.BlockSpec((1, dh), lambda i, h: (0, h))
    wspec = pl.BlockSpec((1, 1, dh, dh), lambda i, h: (layer, h, 0, 0))
    return pl.pallas_call(
        functools.partial(_mlstm_body, seq=seq),
        grid=(b, ML_HEADS),
        in_specs=[pl.BlockSpec(blk, lambda i, h: (i, 0, c0 + h)),
                  pl.BlockSpec(blk, lambda i, h: (i, 0, c0 + ML_HEADS + h)),
                  pl.BlockSpec(blk, lambda i, h: (i, 0, c0 + 2 * ML_HEADS + h)),
                  pl.BlockSpec((ift.shape[0], seq), lambda i, h: (0, i)),
                  pl.BlockSpec((1, 1, 2), lambda i, h: (h, 0, 0)),
                  pl.BlockSpec((ML_CONV, dh), lambda i, h: (0, h)),
                  vec, wspec, wspec, vec, vec],
        out_specs=pl.BlockSpec(blk, lambda i, h: (i, 0, h)),
        out_shape=jax.ShapeDtypeStruct((b, seq, ML_WIDTH), BF16),
        scratch_shapes=[pltpu.VMEM((seq, dh), F32),
                        pltpu.VMEM((seq, dh), BF16),
                        pltpu.VMEM((seq, dh), BF16),
                        pltpu.VMEM((dh, dh), F32),
                        pltpu.VMEM((1, dh), F32),
                        pltpu.VMEM((1, 1), F32),
                        pltpu.VMEM((8, seq), F32),
                        pltpu.VMEM((seq, 8), F32)],
        compiler_params=_cparams(("parallel", "parallel")),
        name="mlstm",
    )(proj3, proj3, proj3, ift, bif, conv_w, conv_b.reshape(1, -1), wq, wk,
      g_h.reshape(1, -1), skip.reshape(1, -1))


def _merge_body(x_ref, ya_ref, ys_ref, ym_ref, ga_ref, gs_ref, gm_ref, bg_ref,
                wa_ref, ws_ref, wm_ref, wo_ref, o_ref):
    d = x_ref.shape[1]
    bg = bg_ref[...]

    def branch(y_ref, w_ref, g_ref, k):
        gate = jax.nn.sigmoid(g_ref[...].astype(F32) + bg[:, k * d:(k + 1) * d])
        return gate * jnp.dot(y_ref[...], w_ref[0], preferred_element_type=F32)

    merged = branch(ya_ref, wa_ref, ga_ref, 0) + branch(ys_ref, ws_ref, gs_ref, 1) \
        + branch(ym_ref, wm_ref, gm_ref, 2)
    o_ref[...] = x_ref[...] + jnp.dot(merged.astype(BF16), wo_ref[0], preferred_element_type=F32)


def _merge(x2, ya, ys, ym, gates, b_gate, w_a, w_s, w_m, w_o, layer, tm):
    m, d = x2.shape
    tm = min(tm, m)
    row = lambda w: pl.BlockSpec((tm, w), lambda i: (i, 0))
    gate = lambda k: pl.BlockSpec((tm, d), lambda i: (i, k))
    wres = lambda w: _resident((1,) + w.shape[1:], lambda i: (layer, 0, 0))
    return pl.pallas_call(
        _merge_body,
        grid=(m // tm,),
        in_specs=[row(d), row(ya.shape[1]), row(ys.shape[1]), row(ym.shape[1]),
                  gate(0), gate(1), gate(2), _resident((1, 3 * d), lambda i: (0, 0)),
                  wres(w_a), wres(w_s), wres(w_m), wres(w_o)],
        out_specs=row(d),
        out_shape=jax.ShapeDtypeStruct((m, d), F32),
        compiler_params=_cparams(("parallel",)),
        name="merge",
    )(x2, ya, ys, ym, gates, gates, gates, b_gate.reshape(1, -1), w_a, w_s, w_m, w_o)


def _xattn_body(x_ref, g_ref, wq_ref, kv_ref, gq_ref, gk_ref, wo_ref, gn_ref, o_ref, hn_ref, q_s, att_s):
    d = x_ref.shape[1]
    dh = d // MEM_HEADS
    scale = dh ** -0.5
    hn = _rms(x_ref[...], g_ref[...]).astype(BF16)
    q_s[...] = jnp.dot(hn, wq_ref[0], preferred_element_type=F32)
    for h in range(MEM_HEADS):
        cols = slice(h * dh, (h + 1) * dh)
        qn = _rms(q_s[:, cols], gq_ref[...]).astype(BF16)
        kn = _rms(kv_ref[0, :, cols].astype(F32), gk_ref[...]).astype(BF16)
        s = lax.dot_general(qn, kn, _NT, preferred_element_type=F32) * scale
        p = jnp.exp(s - jnp.max(s, axis=-1, keepdims=True))
        l = jnp.sum(p, axis=-1, keepdims=True)
        v = kv_ref[0, :, d + h * dh:d + (h + 1) * dh]
        att_s[:, cols] = (jnp.dot(p.astype(BF16), v, preferred_element_type=F32) / l).astype(BF16)
    out = x_ref[...] + jnp.dot(att_s[...], wo_ref[0], preferred_element_type=F32)
    o_ref[...] = out
    hn_ref[...] = _rms(out, gn_ref[...]).astype(BF16)


def _xattn(x2, g, w_q, kv3, g_q, g_k, w_o, g_next, layer, seq, tm):
    m, d = x2.shape
    tm = min(tm, seq)
    per_seq = seq // tm
    wres = lambda w: _resident((1,) + w.shape[1:], lambda i: (layer, 0, 0))
    head = pl.BlockSpec((1, d // MEM_HEADS), lambda i: (0, 0))
    return pl.pallas_call(
        _xattn_body,
        grid=(m // tm,),
        in_specs=[pl.BlockSpec((tm, d), lambda i: (i, 0)),
                  pl.BlockSpec((1, d), lambda i: (0, 0)),
                  wres(w_q),
                  pl.BlockSpec((1,) + kv3.shape[1:], lambda i: (i // per_seq, 0, 0)),
                  head, head, wres(w_o),
                  pl.BlockSpec((1, d), lambda i: (0, 0))],
        out_specs=[pl.BlockSpec((tm, d), lambda i: (i, 0)), pl.BlockSpec((tm, d), lambda i: (i, 0))],
        out_shape=[jax.ShapeDtypeStruct((m, d), F32), jax.ShapeDtypeStruct((m, d), BF16)],
        scratch_shapes=[pltpu.VMEM((tm, d), F32), pltpu.VMEM((tm, d), BF16)],
        compiler_params=_cparams(("parallel",)),
        name="mem_attention",
    )(x2, g.reshape(1, d), w_q, kv3, g_q.reshape(1, -1), g_k.reshape(1, -1), w_o, g_next.reshape(1, d))


def _ffn_up_body(xa_ref, xb_ref, g_ref, wg_ref, wu_ref, wd_ref, a_ref, wdb_ref, hn_ref, *, n_fill):
    rows = hn_ref.shape[0]
    s = pl.program_id(1)

    @pl.when(s < n_fill)
    def _():
        _fill_norm(xa_ref, xb_ref, g_ref, hn_ref, s)

    @pl.when(s >= n_fill)
    def _():
        wdb_ref[...] = wd_ref[0].astype(BF16)
        wg, wu = wg_ref[0].astype(BF16), wu_ref[0].astype(BF16)
        for r in range(0, rows, DOT_ROWS):
            hn = hn_ref[r:r + DOT_ROWS, :]
            gate = jnp.dot(hn, wg, preferred_element_type=F32)
            up = jnp.dot(hn, wu, preferred_element_type=F32)
            a_ref[r:r + DOT_ROWS, :] = (gate * jax.nn.sigmoid(gate) * up).astype(a_ref.dtype)


def _ffn_up(x2, g, w_gu, w_down, layer):
    m, d = x2.shape
    f = w_gu.shape[2] // 2
    rows = min(ROW_GROUP, m)
    n_fill = rows // NORM_ROWS
    n_tiles = f // COL_TILE
    groups = m // rows
    slab = f // (groups * n_tiles)
    assert slab * groups * n_tiles == f and slab % BF16_ROWS == 0
    tile = lambda s: jnp.clip(s - n_fill, 0, n_tiles - 1)
    return pl.pallas_call(
        functools.partial(_ffn_up_body, n_fill=n_fill),
        grid=(groups, n_fill + n_tiles),
        in_specs=_fill_specs(d, n_fill) + [
                  pl.BlockSpec((1, d), lambda h, s: (0, 0)),
                  pl.BlockSpec((1, d, COL_TILE), lambda h, s: (layer, 0, tile(s))),
                  pl.BlockSpec((1, d, COL_TILE), lambda h, s: (layer, 0, n_tiles + tile(s))),
                  pl.BlockSpec((1, slab, d), lambda h, s: (layer, h * n_tiles + tile(s), 0))],
        out_specs=[pl.BlockSpec((rows, COL_TILE), lambda h, s: (h, tile(s))),
                   pl.BlockSpec((slab, d), lambda h, s: (h * n_tiles + tile(s), 0))],
        out_shape=[jax.ShapeDtypeStruct((m, f), BF16), jax.ShapeDtypeStruct((f, d), BF16)],
        scratch_shapes=[pltpu.VMEM((rows, d), BF16)],
        compiler_params=_cparams(("arbitrary", "arbitrary")),
        name="swiglu_up",
    )(x2, x2, g.reshape(1, d), w_gu, w_gu, w_down)


def _ffn_down_body(x_ref, a_ref, w_ref, *rest):
    if len(rest) == 3:
        gn_ref, o_ref, hn_ref = rest
    else:
        (o_ref,), gn_ref, hn_ref = rest, None, None
    out = x_ref[...] + jnp.dot(a_ref[...], w_ref[...], preferred_element_type=F32)
    o_ref[...] = out
    if hn_ref is not None:
        hn_ref[...] = _rms(out, gn_ref[...]).astype(BF16)


def _ffn_down(x2, act, w_down_b, tm, g_next=None):
    m, d = x2.shape
    f = act.shape[1]
    tm = min(tm, m)
    row = pl.BlockSpec((tm, d), lambda i: (i, 0))
    in_specs = [row, pl.BlockSpec((tm, f), lambda i: (i, 0)), _resident((f, d), lambda i: (0, 0))]
    args = [x2, act, w_down_b]
    out_specs, out_shape = row, jax.ShapeDtypeStruct((m, d), F32)
    if g_next is not None:
        in_specs.append(pl.BlockSpec((1, d), lambda i: (0, 0)))
        args.append(g_next.reshape(1, d))
        out_specs, out_shape = [row, row], [out_shape, jax.ShapeDtypeStruct((m, d), BF16)]
    return pl.pallas_call(
        _ffn_down_body,
        grid=(m // tm,),
        in_specs=in_specs,
        out_specs=out_specs,
        out_shape=out_shape,
        compiler_params=_cparams(("parallel",)),
        name="swiglu_down",
    )(*args)


def kernel(x, mem, g_mem, norm_mix, w_in, b_gate, g_qa, g_ka, rel_bias, lam_re, lam_im, log_dt, b_re, b_im, c_re, c_im, d_skip, w_glu, b_glu, conv_w, conv_b, wq_m, wk_m, b_i, b_f, g_hm, skip_m, w_br_a, w_br_s, w_br_m, w_out, norm_x, w_xq, w_xkv, g_xq, g_xk, w_xo, norm_ffn, w_gu, w_down):
    b, seq, d = x.shape
    depth = w_in.shape[0]
    tokens = b * seq
    assert seq % ATT_QB == 0 and seq % ML_T == 0 and seq % SSM_T == 0
    assert tokens % min(ROW_GROUP, tokens) == 0

    w_glu_t = jnp.swapaxes(w_glu, 1, 2).astype(BF16)
    wq_b, wk_b = wq_m.astype(BF16), wk_m.astype(BF16)
    wa_b, ws_b, wm_b, wo_b = (w.astype(BF16) for w in (w_br_a, w_br_s, w_br_m, w_out))
    wxq_b, wxo_b = w_xq.astype(BF16), w_xo.astype(BF16)
    w_in_t = jnp.swapaxes(w_in, 1, 2)
    s5_params = _s5_params(lam_re, lam_im, log_dt, b_re, b_im, c_re, c_im)
    att_bias = _att_bias(rel_bias.reshape(depth * ATT_HEADS, -1))
    bif_all = jnp.stack([b_i, b_f], axis=-1).reshape(depth, ML_HEADS, 1, 2)

    x2 = x.reshape(tokens, d)
    mem2 = mem.reshape(b * mem.shape[1], d)
    hn_mix = x2
    for i in range(depth):
        proj, u3, gates, ift = _in_proj(hn_mix, norm_mix[i], w_in_t, i)
        proj3 = proj.reshape(b, seq, -1)

        ya = _chunk_attention(proj3, g_qa[i], g_ka[i], att_bias, i)

        yt3 = _s5_scan(u3, s5_params, i, cpb=seq // SSM_T)
        ys = _s5_post(yt3, u3, d_skip[i], w_glu_t, i, b_glu[i], chunks=8)

        ym = _mlstm(proj3, ift, bif_all[i], conv_w[i], conv_b[i], wq_b, wk_b, i, g_hm[i], skip_m[i],
                    3 * ATT_WIDTH)

        x2 = _merge(x2, ya.reshape(tokens, -1), ys, ym.reshape(tokens, -1), gates, b_gate[i],
                    wa_b, ws_b, wm_b, wo_b, i, tm=256)

        kv = _norm_proj(mem2, g_mem, w_xkv, i, tm=1024, tn=512)
        x2, hn_ffn = _xattn(x2, norm_x[i], wxq_b, kv.reshape(b, mem.shape[1], 2 * d), g_xq[i], g_xk[i],
                            wxo_b, norm_ffn[i], i, seq, tm=512)

        act, w_down_b = _ffn_up(hn_ffn, norm_ffn[i], w_gu, w_down, i)
        if i + 1 < depth:
            x2, hn_mix = _ffn_down(x2, act, w_down_b, tm=256, g_next=norm_mix[i + 1])
        else:
            x2 = _ffn_down(x2, act, w_down_b, tm=256)
    return x2.reshape(b, seq, d)
```
